```python
import jax, jax.numpy as jnp
from jax import lax
import numpy as np

D_MODEL = 1024
BATCH = 8
SEQ = 4096
DEPTH = 1
DEC_BATCH = 128
DEC_SEQ = 1
PAST_LEN = 16384
PAGE_SIZE = 128

M_HEADS = 4
M_V_DIM = D_MODEL // M_HEADS
M_QK_DIM = M_V_DIM // 2
M_CHUNK = 64
CONV_W = 4
QK_CH = 2 * M_HEADS * M_QK_DIM
A_HEADS = 16
A_KV_HEADS = 2
A_HEAD_DIM = D_MODEL // A_HEADS
A_GROUP = A_HEADS // A_KV_HEADS
WINDOW = 128
D_FF = ((8 * D_MODEL // 3 + 255) // 256) * 256
P_DIM = 256
LN_EPS = 1e-5
RMS_EPS = 1e-6
ALPHA = (2 * DEPTH) ** 0.25
BETA = (8 * DEPTH) ** -0.25
SPLIT_SIZES = (QK_CH, M_HEADS * M_V_DIM, M_HEADS, M_HEADS, M_HEADS * M_V_DIM,
               A_HEADS * A_HEAD_DIM, A_KV_HEADS * A_HEAD_DIM, A_KV_HEADS * A_HEAD_DIM,
               D_MODEL, D_MODEL)
IN_WIDTH = sum(SPLIT_SIZES)

kernel_name = 'hybrid_mlstm_swa_sink_deepnorm_step'


def layer_norm(x, g, b):
    xf = x.astype(jnp.float32)
    mu = jnp.mean(xf, -1, keepdims=True)
    var = jnp.mean(jnp.square(xf - mu), -1, keepdims=True)
    return ((xf - mu) * lax.rsqrt(var + LN_EPS) * g + b).astype(x.dtype)


def split_points():
    return [int(s) for s in np.cumsum(SPLIT_SIZES)[:-1]]


def causal_dwconv(x, buf, w, b):
    ext = jnp.concatenate([buf.astype(x.dtype), x], axis=1)
    y = lax.conv_general_dilated(ext, w[:, None, :].astype(x.dtype), window_strides=(1,),
                                 padding='VALID', dimension_numbers=('NWC', 'WIO', 'NWC'),
                                 feature_group_count=x.shape[-1])
    return jax.nn.silu(y + b), ext[:, -(CONV_W - 1):]


def mlstm_chunkwise(q, k, v, i_pre, f_pre, C0, n0, m0):
    B, S, H, DK = q.shape
    L = M_CHUNK if S % M_CHUNK == 0 else S
    nc = S // L
    f32 = jnp.float32

    def to_chunks(a):
        a = a.astype(f32).reshape((B, nc, L) + a.shape[2:])
        return jnp.moveaxis(a, 1, 0)

    xs = (to_chunks(q), to_chunks(k) * (DK ** -0.5), to_chunks(v), to_chunks(i_pre), to_chunks(f_pre))
    causal = jnp.tril(jnp.ones((L, L), dtype=bool))

    def step(carry, inp):
        C, n, m = carry
        qc, kc, vc, ic, fc = inp
        bcum = jnp.swapaxes(jnp.cumsum(jax.nn.log_sigmoid(fc), axis=1), 1, 2)
        logi = jnp.swapaxes(ic, 1, 2)
        dmat = bcum[..., :, None] - bcum[..., None, :] + logi[..., None, :]
        dmat = jnp.where(causal, dmat, -jnp.inf)
        m_inter = bcum + m[..., None]
        m_t = jnp.maximum(m_inter, jnp.max(dmat, -1))
        w = jnp.exp(dmat - m_t[..., None]) * jnp.einsum('bthd,bshd->bhts', qc, kc)
        inter = jnp.exp(m_inter - m_t)
        num = (jnp.einsum('bhts,bshv->bthv', w, vc)
               + jnp.swapaxes(inter, 1, 2)[..., None] * jnp.einsum('bthd,bhdv->bthv', qc, C))
        den = jnp.sum(w, -1) + inter * jnp.einsum('bthd,bhd->bht', qc, n)
        denom = jnp.maximum(jnp.abs(den), jnp.exp(-m_t))
        h = num / jnp.swapaxes(denom, 1, 2)[..., None]
        m_new = m_t[..., -1]
        decay_prev = jnp.exp(bcum[..., -1] + m - m_new)
        w_state = jnp.exp(bcum[..., -1:] - bcum + logi - m_new[..., None])
        C_new = decay_prev[..., None, None] * C + jnp.einsum('bhs,bshd,bshv->bhdv', w_state, kc, vc)
        n_new = decay_prev[..., None] * n + jnp.einsum('bhs,bshd->bhd', w_state, kc)
        return (C_new, n_new, m_new), h

    (C, n, m), hs = lax.scan(step, (C0.astype(f32), n0.astype(f32), m0.astype(f32)), xs)
    h = jnp.moveaxis(hs, 0, 1).reshape(B, S, H, hs.shape[-1])
    return h, C, n, m


def window_attention(q, k, v, k_buf, v_buf, sinks, start):
    B, S = q.shape[0], q.shape[1]
    Q = WINDOW if S % WINDOW == 0 else S
    nb = S // Q
    k_ext = jnp.concatenate([k_buf.astype(k.dtype), k], axis=1)
    v_ext = jnp.concatenate([v_buf.astype(v.dtype), v], axis=1)
    idx = (jnp.arange(nb) * Q)[:, None] + jnp.arange(WINDOW + Q)[None, :]
    kb = k_ext[:, idx]
    vb = v_ext[:, idx]
    qb = q.reshape(B, nb, Q, A_KV_HEADS, A_GROUP, A_HEAD_DIM)
    q_pos = start + jnp.arange(S).reshape(nb, Q)
    k_pos = start - WINDOW + idx
    delta = q_pos[:, :, None] - k_pos[:, None, :]
    mask = (k_pos[:, None, :] >= 0) & (delta >= 0) & (delta <= WINDOW)
    s = jnp.einsum('bnqkgd,bnskd->bnkgqs', qb, kb).astype(jnp.float32) * (A_HEAD_DIM ** -0.5)
    s = jnp.where(mask[:, None, None], s, -jnp.inf)
    sink = sinks.astype(jnp.float32).reshape(A_KV_HEADS, A_GROUP)[:, :, None, None]
    mx = jnp.maximum(jnp.max(s, -1, keepdims=True), sink)
    e = jnp.exp(s - mx)
    p = e / (jnp.sum(e, -1, keepdims=True) + jnp.exp(sink - mx))
    o = jnp.einsum('bnkgqs,bnskd->bnqkgd', p.astype(v.dtype), vb).reshape(B, S, A_HEADS * A_HEAD_DIM)
    return o, k_ext[:, -WINDOW:], v_ext[:, -WINDOW:]


def decoder_layer(x, p, conv_buf, C0, n0, m0, k_buf, v_buf, start,
                  w_in, b_igate, b_fgate, conv_w, conv_b, m_norm_g, attn_sinks, w_out,
                  ln1_g, ln1_b, w_gate_up, w_down, ln2_g, ln2_b, w_ple, w_ple_gate):
    B, S, _ = x.shape
    z = x @ w_in
    qk, mv, mi, mf, mo, aq, ak, av, gm, ga = jnp.split(z, split_points(), axis=-1)
    qk, conv_state = causal_dwconv(qk, conv_buf, conv_w, conv_b)
    mq, mk = jnp.split(qk, 2, axis=-1)
    h, C, n, m = mlstm_chunkwise(mq.reshape(B, S, M_HEADS, M_QK_DIM), mk.reshape(B, S, M_HEADS, M_QK_DIM),
                                 mv.reshape(B, S, M_HEADS, M_V_DIM), mi + b_igate, mf + b_fgate, C0, n0, m0)
    h = h * lax.rsqrt(jnp.mean(jnp.square(h), -1, keepdims=True) + RMS_EPS)
    y_m = (jax.nn.sigmoid(mo.astype(jnp.float32)) * h.reshape(B, S, D_MODEL) * m_norm_g).astype(x.dtype)
    y_a, k_win, v_win = window_attention(aq.reshape(B, S, A_HEADS, A_HEAD_DIM),
                                         ak.reshape(B, S, A_KV_HEADS, A_HEAD_DIM),
                                         av.reshape(B, S, A_KV_HEADS, A_HEAD_DIM),
                                         k_buf, v_buf, attn_sinks, start)
    mixed = jax.nn.sigmoid(gm) * y_m + jax.nn.sigmoid(ga) * y_a
    x = layer_norm(ALPHA * x + mixed @ w_out, ln1_g, ln1_b)
    g, u = jnp.split(x @ w_gate_up, 2, axis=-1)
    ffn = (jax.nn.silu(g) * u) @ w_down
    ple = jax.nn.sigmoid(x @ w_ple_gate) * (p @ w_ple)
    x = layer_norm(ALPHA * x + ffn + ple, ln2_g, ln2_b)
    return x, (C, n, m, conv_state, k_win, v_win)


def trunk(x, p, C0, n0, m0, conv0, k0, v0, start, ln_in_g, ln_in_b, layer_w):
    x = layer_norm(x, ln_in_g, ln_in_b)
    outs = []
    for l in range(DEPTH):
        x, st = decoder_layer(x, p[l], conv0[l], C0[l], n0[l], m0[l], k0[l], v0[l], start,
                              *[w[l] for w in layer_w])
        outs.append(st)
    C, n, m, conv, kw, vw = [jnp.stack(s) for s in zip(*outs)]
    return x, C, n, m, conv, kw, vw


def setup_inputs(seed: int = 0) -> dict:
    key = jax.random.key(seed)
    ks = jax.random.split(key, 32)
    nrm = jax.random.normal
    f32 = jnp.float32
    D = D_MODEL
    return {
        'x_prompt': nrm(ks[0], (BATCH, SEQ, D), f32),
        'x_sample': nrm(ks[1], (DEC_BATCH, DEC_SEQ, D), f32),
        'state_mlstm_C': 0.3 * nrm(ks[2], (DEPTH, DEC_BATCH, M_HEADS, M_QK_DIM, M_V_DIM), f32),
        'state_mlstm_n': 0.3 * nrm(ks[3], (DEPTH, DEC_BATCH, M_HEADS, M_QK_DIM), f32),
        'state_mlstm_m': nrm(ks[4], (DEPTH, DEC_BATCH, M_HEADS), f32),
        'state_conv': nrm(ks[5], (DEPTH, DEC_BATCH, CONV_W - 1, QK_CH), f32),
        'cache_win_k': nrm(ks[6], (DEPTH, DEC_BATCH, WINDOW, A_KV_HEADS, A_HEAD_DIM), f32),
        'cache_win_v': nrm(ks[7], (DEPTH, DEC_BATCH, WINDOW, A_KV_HEADS, A_HEAD_DIM), f32),
        'p_prompt': nrm(ks[8], (DEPTH, BATCH, SEQ, P_DIM), f32),
        'p_sample': nrm(ks[9], (DEPTH, DEC_BATCH, DEC_SEQ, P_DIM), f32),
        'ln_in_g': 1.0 + 0.02 * nrm(ks[10], (D,), f32),
        'ln_in_b': 0.02 * nrm(ks[11], (D,), f32),
        'w_in': nrm(ks[12], (DEPTH, D, IN_WIDTH), f32) * D ** -0.5,
        'b_igate': -1.0 + 0.5 * nrm(ks[13], (DEPTH, M_HEADS), f32),
        'b_fgate': 3.0 + 0.5 * nrm(ks[14], (DEPTH, M_HEADS), f32),
        'conv_w': nrm(ks[15], (DEPTH, CONV_W, QK_CH), f32) * CONV_W ** -0.5,
        'conv_b': 0.02 * nrm(ks[16], (DEPTH, QK_CH), f32),
        'm_norm_g': 1.0 + 0.02 * nrm(ks[17], (DEPTH, D), f32),
        'attn_sinks': 0.5 * nrm(ks[18], (DEPTH, A_HEADS), f32),
        'w_out': nrm(ks[19], (DEPTH, D, D), f32) * (D ** -0.5 * BETA),
        'ln1_g': 1.0 + 0.02 * nrm(ks[20], (DEPTH, D), f32),
        'ln1_b': 0.02 * nrm(ks[21], (DEPTH, D), f32),
        'w_gate_up': nrm(ks[22], (DEPTH, D, 2 * D_FF), f32) * D ** -0.5,
        'w_down': nrm(ks[23], (DEPTH, D_FF, D), f32) * (D_FF ** -0.5 * BETA),
        'ln2_g': 1.0 + 0.02 * nrm(ks[24], (DEPTH, D), f32),
        'ln2_b': 0.02 * nrm(ks[25], (DEPTH, D), f32),
        'w_ple': nrm(ks[26], (DEPTH, P_DIM, D), f32) * (P_DIM ** -0.5 * BETA),
        'w_ple_gate': nrm(ks[27], (DEPTH, D, D), f32) * D ** -0.5,
    }


def reference(x_prompt, x_sample, state_mlstm_C, state_mlstm_n, state_mlstm_m, state_conv,
              cache_win_k, cache_win_v, p_prompt, p_sample, ln_in_g, ln_in_b, w_in, b_igate, b_fgate,
              conv_w, conv_b, m_norm_g, attn_sinks, w_out, ln1_g, ln1_b, w_gate_up, w_down,
              ln2_g, ln2_b, w_ple, w_ple_gate):
    layer_w = (w_in, b_igate, b_fgate, conv_w, conv_b, m_norm_g, attn_sinks, w_out,
               ln1_g, ln1_b, w_gate_up, w_down, ln2_g, ln2_b, w_ple, w_ple_gate)
    B = x_prompt.shape[0]
    dt = x_prompt.dtype
    f32 = jnp.float32
    y_prompt, C_p, n_p, m_p, conv_p, k_p, v_p = trunk(
        x_prompt, p_prompt,
        jnp.zeros((DEPTH, B, M_HEADS, M_QK_DIM, M_V_DIM), f32),
        jnp.zeros((DEPTH, B, M_HEADS, M_QK_DIM), f32),
        jnp.zeros((DEPTH, B, M_HEADS), f32),
        jnp.zeros((DEPTH, B, CONV_W - 1, QK_CH), dt),
        jnp.zeros((DEPTH, B, WINDOW, A_KV_HEADS, A_HEAD_DIM), dt),
        jnp.zeros((DEPTH, B, WINDOW, A_KV_HEADS, A_HEAD_DIM), dt),
        0, ln_in_g, ln_in_b, layer_w)
    y_sample, C_s, n_s, m_s, conv_s, k_s, v_s = trunk(
        x_sample, p_sample, state_mlstm_C, state_mlstm_n, state_mlstm_m, state_conv,
        cache_win_k, cache_win_v, PAST_LEN, ln_in_g, ln_in_b, layer_w)
    return (y_prompt, y_sample, C_p, n_p, m_p, conv_p, k_p, v_p, C_s, n_s, m_s, conv_s, k_s, v_s)
```

```python
import functools

import jax
import jax.numpy as jnp
from jax import lax
from jax.experimental import pallas as pl
from jax.experimental.pallas import tpu as pltpu

F32 = jnp.float32
BF16 = jnp.bfloat16

D = 1024
MH = 4
DV = D // MH
DK = DV // 2
CONV_W = 4
QK_CH = 2 * MH * DK
AH = 16
KVH = 2
HD = D // AH
GROUP = AH // KVH
WIN = 128
DFF = 2816
PD = 256
LN_EPS = 1e-5
RMS_EPS = 1e-6
ALPHA = 2.0 ** 0.25
LANES = 128
SUBLANES = 8

O_QK = 0
O_MV = O_QK + QK_CH
O_MI = O_MV + D
O_MF = O_MI + MH
O_MO = O_MF + MH
O_AQ = O_MO + D
O_AK = O_AQ + D
O_AV = O_AK + KVH * HD
O_GM = O_AV + KVH * HD
O_GA = O_GM + D
IN_WIDTH = O_GA + D

NEG_INF = float("-inf")


def _dot(a, b):
    return jnp.dot(a, b, preferred_element_type=F32)


def _dot_nt(a, b):
    return lax.dot_general(a, b, (((1,), (1,)), ((), ())), preferred_element_type=F32)


def _dot_tn(a, b):
    return lax.dot_general(a, b, (((0,), (0,)), ((), ())), preferred_element_type=F32)


def _dot_exact(a, b):
    return jnp.dot(a, b, preferred_element_type=F32, precision=lax.Precision.HIGHEST)


def _layer_norm(x, g, b):
    mu = jnp.mean(x, -1, keepdims=True)
    xc = x - mu
    var = jnp.mean(xc * xc, -1, keepdims=True)
    return xc * lax.rsqrt(var + LN_EPS) * g + b


def _sigmoid(x):
    return 1.0 / (1.0 + jnp.exp(-x))


def _log_sigmoid(x):
    return jnp.minimum(x, 0.0) - jnp.log(1.0 + jnp.exp(-jnp.abs(x)))


def _const_spec(shape, grid_rank):
    zeros = (0,) * len(shape)
    if grid_rank == 1:
        imap = lambda i: zeros
    else:
        imap = lambda i, j: zeros
    return pl.BlockSpec(shape, imap, pipeline_mode=pl.Buffered(1))


def _dup_halves(x, lo):
    xr = pltpu.roll(x, HD, 1)
    return jnp.where(lo, x, xr), jnp.where(lo, xr, x)


def _mixer_kernel(x_ref, lng_ref, lnb_ref, wqk_ref, wmv_ref, wif_ref, wift_ref, bif_ref, bift_ref,
                  wmo_ref, waq_ref, wakv_ref, wgm_ref, wga_ref, cw_ref, cb_ref, mng_ref, sink_ref,
                  wout_ref, l1g_ref, l1b_ref,
                  x1_ref, c_ref, n_ref, m_ref, conv_ref, kp_ref, vp_ref,
                  convbuf, kvc, ym_s, ya_s, *, tm, lc):
    s = pl.program_id(1)
    last = pl.num_programs(1) - 1

    @pl.when(s == 0)
    def _init():
        c_ref[...] = jnp.zeros_like(c_ref)
        n_ref[...] = jnp.zeros_like(n_ref)
        m_ref[...] = jnp.zeros_like(m_ref)
        convbuf[0:SUBLANES, :] = jnp.zeros((SUBLANES, QK_CH), F32)
        kvc[...] = jnp.zeros_like(kvc)

    xn = _layer_norm(x_ref[0], lng_ref[...], lnb_ref[...])
    xb = xn.astype(BF16)

    zqk = _dot(xb, wqk_ref[...])
    convbuf[SUBLANES:SUBLANES + tm, :] = zqk
    cw = cw_ref[...]
    y = (cw[3:4] * zqk + cw[2:3] * convbuf[7:7 + tm, :] + cw[1:2] * convbuf[6:6 + tm, :]
         + cw[0:1] * convbuf[5:5 + tm, :] + cb_ref[...])
    qk = y * _sigmoid(y)
    convbuf[0:SUBLANES, :] = convbuf[tm:tm + SUBLANES, :]

    @pl.when(s == last)
    def _conv_out():
        conv_ref[0] = convbuf[SUBLANES - (CONV_W - 1):SUBLANES, :]

    gcol = _dot(xb, wif_ref[...]) + bif_ref[...]
    grow = _dot_nt(wift_ref[...], xb) + bift_ref[...]
    lscol = _log_sigmoid(gcol)
    lsrow = _log_sigmoid(grow)
    vm = _dot(xb, wmv_ref[...])
    mo = _dot(xb, wmo_ref[...])
    mng = mng_ref[...]
    ri = lax.broadcasted_iota(jnp.int32, (lc, lc), 0)
    ci = lax.broadcasted_iota(jnp.int32, (lc, lc), 1)
    causal = ci <= ri
    tril = causal.astype(F32)
    triu = (ri <= ci).astype(F32)
    for c in range(tm // lc):
        r0 = c * lc
        bcol = _dot_exact(tril, lscol[r0:r0 + lc, :])
        brow = _dot_exact(lsrow[:, r0:r0 + lc], triu)
        for h in range(MH):
            qh = qk[r0:r0 + lc, h * DK:(h + 1) * DK]
            kh = qk[r0:r0 + lc, MH * DK + h * DK:MH * DK + (h + 1) * DK] * (DK ** -0.5)
            vh = vm[r0:r0 + lc, h * DV:(h + 1) * DV].astype(BF16)
            qb = qh.astype(BF16)
            bc = bcol[:, MH + h:MH + h + 1]
            br = brow[MH + h:MH + h + 1, :]
            li_c = gcol[r0:r0 + lc, h:h + 1]
            li_r = grow[h:h + 1, r0:r0 + lc]
            m_prev = m_ref[0, h:h + 1, 0:1]
            c_prev = c_ref[0, h]
            n_prev = n_ref[0, h:h + 1, :]

            dmat = jnp.where(causal, bc - br + li_r, NEG_INF)
            m_inter = bc + m_prev
            m_t = jnp.maximum(m_inter, jnp.max(dmat, -1, keepdims=True))
            w = jnp.exp(dmat - m_t) * _dot_nt(qb, kh.astype(BF16))
            inter = jnp.exp(m_inter - m_t)
            num = _dot(w.astype(BF16), vh) + inter * _dot(qb, c_prev.astype(BF16))
            den = jnp.sum(w, -1, keepdims=True) + inter * jnp.sum(qh * n_prev, -1, keepdims=True)
            denom = jnp.maximum(jnp.abs(den), jnp.exp(-m_t))
            hh = num / denom
            hn = hh * lax.rsqrt(jnp.mean(hh * hh, -1, keepdims=True) + RMS_EPS)
            ym_s[r0:r0 + lc, h * DV:(h + 1) * DV] = (
                _sigmoid(mo[r0:r0 + lc, h * DV:(h + 1) * DV]) * hn * mng[:, h * DV:(h + 1) * DV])

            m_new = m_t[lc - 1:lc, :]
            bc_last = bc[lc - 1:lc, :]
            decay = jnp.exp(bc_last + m_prev - m_new)
            ks = kh * jnp.exp(bc_last - bc + li_c - m_new)
            c_ref[0, h] = decay * c_prev + _dot_tn(ks.astype(BF16), vh)
            n_ref[0, h:h + 1, :] = decay * n_prev + jnp.sum(ks, 0, keepdims=True)
            m_ref[0, h:h + 1, :] = jnp.broadcast_to(m_new, (1, LANES))

    aq = _dot(xb, waq_ref[...]) * (HD ** -0.5)
    akv = _dot(xb, wakv_ref[...])
    kk = akv[:, 0:KVH * HD]
    vv = akv[:, KVH * HD:2 * KVH * HD]
    lo = lax.broadcasted_iota(jnp.int32, (tm, LANES), 1) < HD
    k2 = [a.astype(BF16) for a in _dup_halves(kk, lo)]
    v2 = [a.astype(BF16) for a in _dup_halves(vv, lo)]
    lo_q = lax.broadcasted_iota(jnp.int32, (WIN, LANES), 1) < HD
    qi =lax.broadcasted_iota(jnp.int32, (WIN, 2 * WIN), 0)
    kj = lax.broadcasted_iota(jnp.int32, (WIN, 2 * WIN), 1)
    band = (kj >= qi) & (kj <= qi + WIN)
    first_floor = jnp.where(s == 0, WIN, 0)
    for j in range(tm // WIN):
        r0 = j * WIN
        mask = band & (kj >= first_floor) if j == 0 else band
        for g in range(KVH):
            if j == 0:
                kprev, vprev = kvc[g], kvc[KVH + g]
            else:
                kprev, vprev = k2[g][r0 - WIN:r0], v2[g][r0 - WIN:r0]
            kcat = jnp.concatenate([kprev, k2[g][r0:r0 + WIN]], axis=0)
            vcat = jnp.concatenate([vprev, v2[g][r0:r0 + WIN]], axis=0)
            for pj in range(g * GROUP // 2, (g + 1) * GROUP // 2):
                qpair = aq[r0:r0 + WIN, pj * LANES:(pj + 1) * LANES]
                outs = []
                for par in range(2):
                    sink = sink_ref[2 * pj + par]
                    qm = jnp.where(lo_q if par == 0 else ~lo_q, qpair, 0.0).astype(BF16)
                    sc = jnp.where(mask, _dot_nt(qm, kcat), NEG_INF)
                    mx = jnp.maximum(jnp.max(sc, -1, keepdims=True), sink)
                    e = jnp.exp(sc - mx)
                    dsum = jnp.sum(e, -1, keepdims=True) + jnp.exp(sink - mx)
                    outs.append(_dot(e.astype(BF16), vcat) / dsum)
                ya_s[r0:r0 + WIN, pj * LANES:(pj + 1) * LANES] = jnp.where(lo_q, outs[0], outs[1])
    for g in range(KVH):
        kvc[g] = k2[g][tm - WIN:tm]
        kvc[KVH + g] = v2[g][tm - WIN:tm]

    @pl.when(s == last)
    def _kv_out():
        kp_ref[0] = kk[tm - WIN:tm]
        vp_ref[0] = vv[tm - WIN:tm]

    gm = _dot(xb, wgm_ref[...])
    ga = _dot(xb, wga_ref[...])
    mixed = _sigmoid(gm) * ym_s[...] + _sigmoid(ga) * ya_s[...]
    r = _dot(mixed.astype(BF16), wout_ref[...])
    x1_ref[0] = _layer_norm(ALPHA * xn + r, l1g_ref[...], l1b_ref[...])


def _prompt_mixer(x, wts, *, tm=256, lc=128):
    B, S, _ = x.shape
    assert S % tm == 0 and tm % lc == 0 and tm % WIN == 0 and tm >= WIN
    grid = (B, S // tm)
    cs = functools.partial(_const_spec, grid_rank=2)
    in_specs = [
        pl.BlockSpec((1, tm, D), lambda b, s: (b, s, 0)),
        cs((1, D)), cs((1, D)),
        cs((D, QK_CH)), cs((D, D)), cs((D, LANES)), cs((SUBLANES, D)), cs((1, LANES)), cs((SUBLANES, 1)),
        cs((D, D)), cs((D, D)), cs((D, 2 * KVH * HD)), cs((D, D)), cs((D, D)),
        cs((CONV_W, QK_CH)), cs((1, QK_CH)), cs((1, D)),
        pl.BlockSpec(memory_space=pltpu.SMEM),
        cs((D, D)), cs((1, D)), cs((1, D)),
    ]
    out_shape = (
        jax.ShapeDtypeStruct((B, S, D), F32),
        jax.ShapeDtypeStruct((B, MH, DK, DV), F32),
        jax.ShapeDtypeStruct((B, MH, DK), F32),
        jax.ShapeDtypeStruct((B, SUBLANES, LANES), F32),
        jax.ShapeDtypeStruct((B, CONV_W - 1, QK_CH), F32),
        jax.ShapeDtypeStruct((B, WIN, KVH * HD), F32),
        jax.ShapeDtypeStruct((B, WIN, KVH * HD), F32),
    )
    out_specs = (
        pl.BlockSpec((1, tm, D), lambda b, s: (b, s, 0)),
        pl.BlockSpec((1, MH, DK, DV), lambda b, s: (b, 0, 0, 0)),
        pl.BlockSpec((1, MH, DK), lambda b, s: (b, 0, 0)),
        pl.BlockSpec((1, SUBLANES, LANES), lambda b, s: (b, 0, 0)),
        pl.BlockSpec((1, CONV_W - 1, QK_CH), lambda b, s: (b, 0, 0)),
        pl.BlockSpec((1, WIN, KVH * HD), lambda b, s: (b, 0, 0)),
        pl.BlockSpec((1, WIN, KVH * HD), lambda b, s: (b, 0, 0)),
    )
    scratch = [
        pltpu.VMEM((tm + SUBLANES, QK_CH), F32),
        pltpu.VMEM((2 * KVH, WIN, LANES), BF16),
        pltpu.VMEM((tm, D), F32),
        pltpu.VMEM((tm, D), F32),
    ]
    return pl.pallas_call(
        functools.partial(_mixer_kernel, tm=tm, lc=lc),
        grid=grid, in_specs=in_specs, out_specs=out_specs, out_shape=out_shape,
        scratch_shapes=scratch, name="prompt_mixer",
        compiler_params=pltpu.CompilerParams(
            dimension_semantics=("arbitrary", "arbitrary"), vmem_limit_bytes=56 * 1024 * 1024),
    )(x, wts["ln_in_g"], wts["ln_in_b"], wts["w_qk"], wts["w_mv"], wts["w_if"], wts["w_ift"],
      wts["b_if"], wts["b_ift"], wts["w_mo"], wts["w_aq"], wts["w_akv"], wts["w_gm"], wts["w_ga"],
      wts["conv_w"], wts["conv_b"], wts["m_norm_g"], wts["sinks"], wts["w_out"], wts["ln1_g"], wts["ln1_b"])


def _ffn_kernel(x_ref, p_ref, wg_ref, wu_ref, wd_ref, wpg_ref, wp_ref, g_ref, b_ref, o_ref, *, fc):
    x = x_ref[...]
    xb = x.astype(BF16)
    acc = ALPHA * x + _sigmoid(_dot(xb, wpg_ref[...])) * _dot(p_ref[...].astype(BF16), wp_ref[...])
    for c in range(DFF // fc):
        g = _dot(xb, wg_ref[:, c * fc:(c + 1) * fc])
        u = _dot(xb, wu_ref[:, c * fc:(c + 1) * fc])
        hcat = (g * _sigmoid(g) * u).astype(BF16)
        acc = acc + _dot(hcat, wd_ref[c * fc:(c + 1) * fc, :])
    o_ref[...] = _layer_norm(acc, g_ref[...], b_ref[...])


def _ffn(x, p, wts, *, tm, fc=1408):
    N = x.shape[0]
    assert N % tm == 0 and DFF % fc == 0
    cs = functools.partial(_const_spec, grid_rank=1)
    return pl.pallas_call(
        functools.partial(_ffn_kernel, fc=fc),
        grid=(N // tm,),
        in_specs=[pl.BlockSpec((tm, D), lambda i: (i, 0)), pl.BlockSpec((tm, PD), lambda i: (i, 0)),
                  cs((D, DFF)), cs((D, DFF)), cs((DFF, D)), cs((D, D)), cs((PD, D)), cs((1, D)), cs((1, D))],
        out_specs=pl.BlockSpec((tm, D), lambda i: (i, 0)),
        out_shape=jax.ShapeDtypeStruct((N, D), F32),
        name="ffn",
        compiler_params=pltpu.CompilerParams(
            dimension_semantics=("arbitrary",), vmem_limit_bytes=56 * 1024 * 1024),
    )(x, p, wts["w_g"], wts["w_u"], wts["w_d"], wts["w_pg"], wts["w_p"], wts["ln2_g"], wts["ln2_b"])


Z_V = 0
Z_MO = Z_V + D
Z_GM = Z_MO + D
Z_GA = Z_GM + D
Z_XN = Z_GA + D
Z_Q = Z_XN + D
Z_KW = Z_Q + MH * DK
Z_SC = Z_KW + MH * DK
Z_KN = Z_SC + LANES
Z_VN = Z_KN + LANES
Z_W = Z_VN + LANES


def _sample_pre_kernel(x_ref, lng_ref, lnb_ref, wqk_ref, wmv_ref, wif_ref, bif_ref, wmo_ref, waq_ref,
                       wakv_ref, wgm_ref, wga_ref, cw_ref, cb_ref, cs_ref, n_ref, m_ref,
                       z_ref, qs_ref, convo_ref, no_ref, mo_ref, *, nb, tb):
    xn = _layer_norm(x_ref[...], lng_ref[...], lnb_ref[...])
    xb = xn.astype(BF16)
    zqk = _dot(xb, wqk_ref[...])
    cw = cw_ref[...]
    y = (cw[3:4] * zqk + cw[2:3] * cs_ref[:, 2 * QK_CH:3 * QK_CH] + cw[1:2] * cs_ref[:, QK_CH:2 * QK_CH]
         + cw[0:1] * cs_ref[:, 0:QK_CH] + cb_ref[...])
    qk = y * _sigmoid(y)
    convo_ref[:, 0:2 * QK_CH] = cs_ref[:, QK_CH:3 * QK_CH]
    convo_ref[:, 2 * QK_CH:3 * QK_CH] = zqk

    gcol = _dot(xb, wif_ref[...]) + bif_ref[...]
    ig = gcol[:, 0:MH]
    m_inter = _log_sigmoid(gcol[:, MH:2 * MH]) + m_ref[...]
    m_t = jnp.maximum(m_inter, ig)
    inter = jnp.exp(m_inter - m_t)
    wi = jnp.exp(ig - m_t)
    enm = jnp.exp(-m_t)
    mo_ref[...] = m_t

    z_ref[:, Z_V:Z_V + D] = _dot(xb, wmv_ref[...])
    z_ref[:, Z_MO:Z_MO + D] = _dot(xb, wmo_ref[...])
    z_ref[:, Z_GM:Z_GM + D] = _dot(xb, wgm_ref[...])
    z_ref[:, Z_GA:Z_GA + D] = _dot(xb, wga_ref[...])
    z_ref[:, Z_XN:Z_XN + D] = xn
    z_ref[:, Z_Q:Z_Q + MH * DK] = qk[:, 0:MH * DK]
    sc = jnp.zeros((nb, LANES), F32)
    lane = lax.broadcasted_iota(jnp.int32, (nb, LANES), 1)
    for h in range(MH):
        qh = qk[:, h * DK:(h + 1) * DK]
        kh = qk[:, MH * DK + h * DK:MH * DK + (h + 1) * DK] * (DK ** -0.5)
        nh = n_ref[:, h * DK:(h + 1) * DK]
        wv = wi[:, h:h + 1] * jnp.sum(qh * kh, -1, keepdims=True)
        den = wv + inter[:, h:h + 1] * jnp.sum(qh * nh, -1, keepdims=True)
        denom = jnp.maximum(jnp.abs(den), enm[:, h:h + 1])
        kw = wi[:, h:h + 1] * kh
        z_ref[:, Z_KW + h * DK:Z_KW + (h + 1) * DK] = kw
        no_ref[:, h * DK:(h + 1) * DK] = inter[:, h:h + 1] * nh + kw
        sc = jnp.where(lane == h, inter[:, h:h + 1], sc)
        sc = jnp.where(lane == MH + h, wv, sc)
        sc = jnp.where(lane == 2 * MH + h, denom, sc)
    z_ref[:, Z_SC:Z_SC + LANES] = sc

    akv = _dot(xb, wakv_ref[...])
    z_ref[:, Z_KN:Z_KN + LANES] = akv[:, 0:LANES]
    z_ref[:, Z_VN:Z_VN + LANES] = akv[:, LANES:2 * LANES]
    aq = _dot(xb, waq_ref[...]) * (HD ** -0.5)
    lo = lax.broadcasted_iota(jnp.int32, (nb, LANES), 1) < HD
    for h in range(AH):
        pair = aq[:, (h // 2) * LANES:(h // 2 + 1) * LANES]
        qm = jnp.where(lo if h % 2 == 0 else ~lo, pair, 0.0)
        for t in range(nb // tb):
            qs_ref[(t * AH + h) * tb:(t * AH + h + 1) * tb, :] = qm[t * tb:(t + 1) * tb, :]


def _sample_state_kernel(z_ref, qs_ref, sink_ref, c_ref, ck_ref, cv_ref,
                         co_ref, qc_ref, os_ref, ko_ref, vo_ref, *, tb):
    lo = lax.broadcasted_iota(jnp.int32, (WIN, LANES), 1) < HD
    row = lax.broadcasted_iota(jnp.int32, (WIN, LANES), 0)
    rowt = lax.broadcasted_iota(jnp.int32, (tb, DK), 0)
    sinks = sink_ref[...]
    q_all = z_ref[:, Z_Q:Z_Q + MH * DK]
    kw_all = z_ref[:, Z_KW:Z_KW + MH * DK]
    v_all = z_ref[:, Z_V:Z_V + D]
    sc = z_ref[:, Z_SC:Z_SC + LANES]
    kn_all = z_ref[:, Z_KN:Z_KN + LANES]
    vn_all = z_ref[:, Z_VN:Z_VN + LANES]
    kn2_all = _dup_halves(kn_all, lo[0:tb])
    vn2_all = _dup_halves(vn_all, lo[0:tb])
    for i in range(tb):
        for h in range(MH):
            cb = c_ref[i, h]
            qc = _dot(q_all[:, h * DK:(h + 1) * DK].astype(BF16), cb.astype(BF16))
            qc_ref[i:i + 1, h * DV:(h + 1) * DV] = qc[i:i + 1]
            kw_i = jnp.where(rowt == i, kw_all[:, h * DK:(h + 1) * DK], 0.0)
            outer = _dot_tn(kw_i, v_all[:, h * DV:(h + 1) * DV])
            co_ref[i, h] = sc[i:i + 1, h:h + 1] * cb + outer
        kc = ck_ref[i]
        vc = cv_ref[i]
        kn = kn_all[i:i + 1]
        vn = vn_all[i:i + 1]
        k2 = _dup_halves(kc, lo)
        v2 = _dup_halves(vc, lo)
        kn2 = [a[i:i + 1] for a in kn2_all]
        vn2 = [a[i:i + 1] for a in vn2_all]
        qrow = qs_ref[pl.ds(i, AH, stride=tb), :]
        outs = []
        for g in range(KVH):
            qg = qrow[g * GROUP:(g + 1) * GROUP]
            s = _dot_nt(qg.astype(BF16), k2[g].astype(BF16))
            s_self = jnp.sum(qg * kn2[g], -1, keepdims=True)
            sg = sinks[g * GROUP:(g + 1) * GROUP]
            mx = jnp.maximum(jnp.maximum(jnp.max(s, -1, keepdims=True), s_self), sg)
            e = jnp.exp(s - mx)
            es = jnp.exp(s_self - mx)
            dsum = jnp.sum(e, -1, keepdims=True) + es + jnp.exp(sg - mx)
            outs.append((_dot(e.astype(BF16), v2[g].astype(BF16)) + es * vn2[g]) / dsum)
        os_ref[pl.ds(i, AH, stride=tb), :] = jnp.concatenate(outs, axis=0)
        ko_ref[i] = jnp.where(row == WIN - 1, kn, pltpu.roll(kc, WIN - 1, 0))
        vo_ref[i] = jnp.where(row == WIN - 1, vn, pltpu.roll(vc, WIN - 1, 0))


def _sample_post_kernel(z_ref, qc_ref, os_ref, mng_ref, wout_ref, l1g_ref, l1b_ref, x1_ref, *, nb, tb):
    sc = z_ref[:, Z_SC:Z_SC + LANES]
    mng = mng_ref[...]
    lo = lax.broadcasted_iota(jnp.int32, (nb, LANES), 1) < HD
    ym = []
    for h in range(MH):
        v = z_ref[:, Z_V + h * DV:Z_V + (h + 1) * DV]
        num = sc[:, MH + h:MH + h + 1] * v + sc[:, h:h + 1] * qc_ref[:, h * DV:(h + 1) * DV]
        hh = num / sc[:, 2 * MH + h:2 * MH + h + 1]
        hn = hh * lax.rsqrt(jnp.mean(hh * hh, -1, keepdims=True) + RMS_EPS)
        ym.append(_sigmoid(z_ref[:, Z_MO + h * DV:Z_MO + (h + 1) * DV]) * hn * mng[:, h * DV:(h + 1) * DV])
    ym = jnp.concatenate(ym, axis=1)
    ya = []
    for pj in range(AH // 2):
        halves = []
        for par in range(2):
            h = 2 * pj + par
            halves.append(jnp.concatenate(
                [os_ref[(t * AH + h) * tb:(t * AH + h + 1) * tb, :] for t in range(nb // tb)], axis=0))
        ya.append(jnp.where(lo, halves[0], halves[1]))
    ya = jnp.concatenate(ya, axis=1)
    mixed = _sigmoid(z_ref[:, Z_GM:Z_GM + D]) * ym + _sigmoid(z_ref[:, Z_GA:Z_GA + D]) * ya
    r = _dot(mixed.astype(BF16), wout_ref[...])
    x1_ref[...] = _layer_norm(ALPHA * z_ref[:, Z_XN:Z_XN + D] + r, l1g_ref[...], l1b_ref[...])


def _sample_mixer(x, c0, n0, m0, conv0, k0, v0, wts, *, tb=8):
    nb = x.shape[0]
    assert nb % tb == 0
    vmem = pltpu.CompilerParams(dimension_semantics=("arbitrary",), vmem_limit_bytes=56 * 1024 * 1024)
    full = lambda shape: pl.BlockSpec(shape, lambda i: (0,) * len(shape))
    pre_in = [x, wts["ln_in_g"], wts["ln_in_b"], wts["w_qk"], wts["w_mv"], wts["w_if"], wts["b_if"],
              wts["w_mo"], wts["w_aq"], wts["w_akv"], wts["w_gm"], wts["w_ga"], wts["conv_w"], wts["conv_b"],
              conv0, n0, m0]
    z, qs, conv_new, n_new, m_new = pl.pallas_call(
        functools.partial(_sample_pre_kernel, nb=nb, tb=tb),
        grid=(1,),
        in_specs=[full(a.shape) for a in pre_in],
        out_specs=(full((nb, Z_W)), full((nb * AH, LANES)), full((nb, (CONV_W - 1) * QK_CH)),
                   full((nb, MH * DK)), full((nb, MH))),
        out_shape=(jax.ShapeDtypeStruct((nb, Z_W), F32), jax.ShapeDtypeStruct((nb * AH, LANES), F32),
                   jax.ShapeDtypeStruct((nb, (CONV_W - 1) * QK_CH), F32),
                   jax.ShapeDtypeStruct((nb, MH * DK), F32), jax.ShapeDtypeStruct((nb, MH), F32)),
        name="sample_pre", compiler_params=vmem,
    )(*pre_in)

    c_new, qc, os_, k_new, v_new = pl.pallas_call(
        functools.partial(_sample_state_kernel, tb=tb),
        grid=(nb // tb,),
        in_specs=[pl.BlockSpec((tb, Z_W), lambda i: (i, 0)),
                  pl.BlockSpec((tb * AH, LANES), lambda i: (i, 0)),
                  pl.BlockSpec((AH, 1), lambda i: (0, 0)),
                  pl.BlockSpec((tb, MH, DK, DV), lambda i: (i, 0, 0, 0)),
                  pl.BlockSpec((tb, WIN, LANES), lambda i: (i, 0, 0)),
                  pl.BlockSpec((tb, WIN, LANES), lambda i: (i, 0, 0))],
        out_specs=(pl.BlockSpec((tb, MH, DK, DV), lambda i: (i, 0, 0, 0)),
                   pl.BlockSpec((tb, D), lambda i: (i, 0)),
                   pl.BlockSpec((tb * AH, LANES), lambda i: (i, 0)),
                   pl.BlockSpec((tb, WIN, LANES), lambda i: (i, 0, 0)),
                   pl.BlockSpec((tb, WIN, LANES), lambda i: (i, 0, 0))),
        out_shape=(jax.ShapeDtypeStruct((nb, MH, DK, DV), F32), jax.ShapeDtypeStruct((nb, D), F32),
                   jax.ShapeDtypeStruct((nb * AH, LANES), F32),
                   jax.ShapeDtypeStruct((nb, WIN, LANES), F32), jax.ShapeDtypeStruct((nb, WIN, LANES), F32)),
        name="sample_state", compiler_params=vmem,
    )(z, qs, wts["sinks_col"], c0, k0, v0)

    post_in = [z, qc, os_, wts["m_norm_g"], wts["w_out"], wts["ln1_g"], wts["ln1_b"]]
    x1 = pl.pallas_call(
        functools.partial(_sample_post_kernel, nb=nb, tb=tb),
        grid=(1,),
        in_specs=[full(a.shape) for a in post_in],
        out_specs=full((nb, D)),
        out_shape=jax.ShapeDtypeStruct((nb, D), F32),
        name="sample_post", compiler_params=vmem,
    )(*post_in)
    return x1, c_new, n_new, m_new, conv_new, k_new, v_new


def _prep_weights(ln_in_g, ln_in_b, w_in, b_igate, b_fgate, conv_w, conv_b, m_norm_g, attn_sinks, w_out,
                  ln1_g, ln1_b, w_gate_up, w_down, ln2_g, ln2_b, w_ple, w_ple_gate):
    w = w_in[0]
    row = lambda a: a.reshape(1, -1).astype(F32)
    w_if = jnp.pad(w[:, O_MI:O_MO], ((0, 0), (0, LANES - 2 * MH)))
    b_if = jnp.pad(jnp.concatenate([b_igate[0], b_fgate[0]]), (0, LANES - 2 * MH))
    return {
        "ln_in_g": row(ln_in_g), "ln_in_b": row(ln_in_b),
        "w_qk": w[:, O_QK:O_MV].astype(BF16), "w_mv": w[:, O_MV:O_MI].astype(BF16),
        "w_if": w_if.astype(BF16), "w_ift": w[:, O_MI:O_MO].T.astype(BF16),
        "b_if": b_if.reshape(1, LANES), "b_ift": b_if[:2 * MH].reshape(2 * MH, 1),
        "w_mo": w[:, O_MO:O_AQ].astype(BF16), "w_aq": w[:, O_AQ:O_AK].astype(BF16),
        "w_akv": w[:, O_AK:O_GM].astype(BF16), "w_gm": w[:, O_GM:O_GA].astype(BF16),
        "w_ga": w[:, O_GA:IN_WIDTH].astype(BF16),
        "conv_w": conv_w[0], "conv_b": row(conv_b[0]), "m_norm_g": row(m_norm_g[0]),
        "sinks": attn_sinks[0], "sinks_col": attn_sinks[0].reshape(AH, 1),
        "w_out": w_out[0].astype(BF16), "ln1_g": row(ln1_g[0]), "ln1_b": row(ln1_b[0]),
        "w_g": w_gate_up[0][:, :DFF].astype(BF16), "w_u": w_gate_up[0][:, DFF:].astype(BF16),
        "w_d": w_down[0].astype(BF16), "w_pg": w_ple_gate[0].astype(BF16), "w_p": w_ple[0].astype(BF16),
        "ln2_g": row(ln2_g[0]), "ln2_b": row(ln2_b[0]),
    }


def _prompt_path(x, p, wts, *, tm=256, lc=128, tmf=512):
    B, S, _ = x.shape
    x1, c, n, m, conv, k, v = _prompt_mixer(x, wts, tm=tm, lc=lc)
    y = _ffn(x1.reshape(B * S, D), p.reshape(B * S, PD), wts, tm=tmf).reshape(B, S, D)
    return (y, c[None], n[None], m[None, :, :MH, 0], conv[None],
            k.reshape(1, B, WIN, KVH, HD), v.reshape(1, B, WIN, KVH, HD))


def _sample_path(x, p, c0, n0, m0, conv0, k0, v0, wts, *, tb=8):
    nb = x.shape[0]
    x1, c, n, m, conv, k, v = _sample_mixer(
        x.reshape(nb, D), c0[0], n0[0].reshape(nb, MH * DK), m0[0],
        conv0[0].reshape(nb, (CONV_W - 1) * QK_CH), k0[0].reshape(nb, WIN, LANES),
        v0[0].reshape(nb, WIN, LANES), wts, tb=tb)
    y = _ffn(x1, p.reshape(nb, PD), wts, tm=nb).reshape(nb, 1, D)
    return (y, c[None], n.reshape(1, nb, MH, DK), m[None], conv.reshape(1, nb, CONV_W - 1, QK_CH),
            k.reshape(1, nb, WIN, KVH, HD), v.reshape(1, nb, WIN, KVH, HD))


def kernel(x_prompt, x_sample, state_mlstm_C, state_mlstm_n, state_mlstm_m, state_conv, cache_win_k, cache_win_v, p_prompt, p_sample, ln_in_g, ln_in_b, w_in, b_igate, b_fgate, conv_w, conv_b, m_norm_g, attn_sinks, w_out, ln1_g, ln1_b, w_gate_up, w_down, ln2_g, ln2_b, w_ple, w_ple_gate):
    wts = _prep_weights(ln_in_g, ln_in_b, w_in, b_igate, b_fgate, conv_w, conv_b, m_norm_g, attn_sinks,
                        w_out, ln1_g, ln1_b, w_gate_up, w_down, ln2_g, ln2_b, w_ple, w_ple_gate)
    yp, c_p, n_p, m_p, conv_p, k_p, v_p = _prompt_path(x_prompt, p_prompt[0], wts)
    ys, c_s, n_s, m_s, conv_s, k_s, v_s = _sample_path(
        x_sample, p_sample[0], state_mlstm_C, state_mlstm_n, state_mlstm_m, state_conv,
        cache_win_k, cache_win_v, wts)
    return (yp, ys, c_p, n_p, m_p, conv_p, k_p, v_p, c_s, n_s, m_s, conv_s, k_s, v_s)
```

```python
import functools

import jax
import jax.numpy as jnp
from jax import lax
from jax.experimental import pallas as pl
from jax.experimental.pallas import tpu as pltpu

F32 = jnp.float32
BF16 = jnp.bfloat16

D = 1024
MH = 4
DV = D // MH
DK = DV // 2
CONV_W = 4
QK_CH = 2 * MH * DK
AH = 16
KVH = 2
HD = D // AH
GROUP = AH // KVH
WIN = 128
DFF = 2816
PD = 256
LN_EPS = 1e-5
RMS_EPS = 1e-6
ALPHA = 2.0 ** 0.25
LANES = 128
SUBLANES = 8

O_QK = 0
O_MV = O_QK + QK_CH
O_MI = O_MV + D
O_MF = O_MI + MH
O_MO = O_MF + MH
O_AQ = O_MO + D
O_AK = O_AQ + D
O_AV = O_AK + KVH * HD
O_GM = O_AV + KVH * HD
O_GA = O_GM + D
IN_WIDTH = O_GA + D

NEG_INF = float("-inf")


def _dot(a, b):
    return jnp.dot(a, b, preferred_element_type=F32)


def _dot_nt(a, b):
    return lax.dot_general(a, b, (((1,), (1,)), ((), ())), preferred_element_type=F32)


def _dot_tn(a, b):
    return lax.dot_general(a, b, (((0,), (0,)), ((), ())), preferred_element_type=F32)


def _dot_exact(a, b):
    return jnp.dot(a, b, preferred_element_type=F32, precision=lax.Precision.HIGHEST)


def _layer_norm(x, g, b):
    mu = jnp.mean(x, -1, keepdims=True)
    xc = x - mu
    var = jnp.mean(xc * xc, -1, keepdims=True)
    return xc * lax.rsqrt(var + LN_EPS) * g + b


def _sigmoid(x):
    return 1.0 / (1.0 + jnp.exp(-x))


def _log_sigmoid(x):
    return jnp.minimum(x, 0.0) - jnp.log(1.0 + jnp.exp(-jnp.abs(x)))


def _const_spec(shape, grid_rank):
    zeros = (0,) * len(shape)
    if grid_rank == 1:
        imap = lambda i: zeros
    else:
        imap = lambda i, j: zeros
    return pl.BlockSpec(shape, imap, pipeline_mode=pl.Buffered(1))


def _dup_halves(x, lo):
    xr = pltpu.roll(x, HD, 1)
    return jnp.where(lo, x, xr), jnp.where(lo, xr, x)


def _mixer_kernel(x_ref, lng_ref, lnb_ref, wqk_ref, wmv_ref, wif_ref, wift_ref, bif_ref, bift_ref,
                  wmo_ref, waq_ref, wakv_ref, wgm_ref, wga_ref, cw_ref, cb_ref, mng_ref, sink_ref,
                  wout_ref, l1g_ref, l1b_ref,
                  x1_ref, c_ref, n_ref, m_ref, conv_ref, kp_ref, vp_ref,
                  convbuf, kvc, ym_s, ya_s, *, tm, lc):
    s = pl.program_id(1)
    last = pl.num_programs(1) - 1

    @pl.when(s == 0)
    def _init():
        c_ref[...] = jnp.zeros_like(c_ref)
        n_ref[...] = jnp.zeros_like(n_ref)
        m_ref[...] = jnp.zeros_like(m_ref)
        convbuf[0:SUBLANES, :] = jnp.zeros((SUBLANES, QK_CH), F32)
        kvc[...] = jnp.zeros_like(kvc)

    xn = _layer_norm(x_ref[0], lng_ref[...], lnb_ref[...])
    xb = xn.astype(BF16)

    zqk = _dot(xb, wqk_ref[...])
    convbuf[SUBLANES:SUBLANES + tm, :] = zqk
    cw = cw_ref[...]
    y = (cw[3:4] * zqk + cw[2:3] * convbuf[7:7 + tm, :] + cw[1:2] * convbuf[6:6 + tm, :]
         + cw[0:1] * convbuf[5:5 + tm, :] + cb_ref[...])
    qk = y * _sigmoid(y)
    convbuf[0:SUBLANES, :] = convbuf[tm:tm + SUBLANES, :]

    @pl.when(s == last)
    def _conv_out():
        conv_ref[0] = convbuf[SUBLANES - (CONV_W - 1):SUBLANES, :]

    gcol = _dot(xb, wif_ref[...]) + bif_ref[...]
    grow = _dot_nt(wift_ref[...], xb) + bift_ref[...]
    lscol = _log_sigmoid(gcol)
    lsrow = _log_sigmoid(grow)
    vm = _dot(xb, wmv_ref[...])
    mo = _dot(xb, wmo_ref[...])
    mng = mng_ref[...]
    ri = lax.broadcasted_iota(jnp.int32, (lc, lc), 0)
    ci = lax.broadcasted_iota(jnp.int32, (lc, lc), 1)
    causal = ci <= ri
    tril = causal.astype(F32)
    triu = (ri <= ci).astype(F32)
    for c in range(tm // lc):
        r0 = c * lc
        bcol = _dot_exact(tril, lscol[r0:r0 + lc, :])
        brow = _dot_exact(lsrow[:, r0:r0 + lc], triu)
        for h in range(MH):
            qh = qk[r0:r0 + lc, h * DK:(h + 1) * DK]
            kh = qk[r0:r0 + lc, MH * DK + h * DK:MH * DK + (h + 1) * DK] * (DK ** -0.5)
            vh = vm[r0:r0 + lc, h * DV:(h + 1) * DV].astype(BF16)
            qb = qh.astype(BF16)
            bc = bcol[:, MH + h:MH + h + 1]
            br = brow[MH + h:MH + h + 1, :]
            li_c = gcol[r0:r0 + lc, h:h + 1]
            li_r = grow[h:h + 1, r0:r0 + lc]
            m_prev = m_ref[0, h:h + 1, 0:1]
            c_prev = c_ref[0, h]
            n_prev = n_ref[0, h:h + 1, :]

            dmat = jnp.where(causal, bc - br + li_r, NEG_INF)
            m_inter = bc + m_prev
            m_t = jnp.maximum(m_inter, jnp.max(dmat, -1, keepdims=True))
            w = jnp.exp(dmat - m_t) * _dot_nt(qb, kh.astype(BF16))
            inter = jnp.exp(m_inter - m_t)
            num = _dot(w.astype(BF16), vh) + inter * _dot(qb, c_prev.astype(BF16))
            den = jnp.sum(w, -1, keepdims=True) + inter * jnp.sum(qh * n_prev, -1, keepdims=True)
            denom = jnp.maximum(jnp.abs(den), jnp.exp(-m_t))
            hh = num / denom
            hn = hh * lax.rsqrt(jnp.mean(hh * hh, -1, keepdims=True) + RMS_EPS)
            ym_s[r0:r0 + lc, h * DV:(h + 1) * DV] = (
                _sigmoid(mo[r0:r0 + lc, h * DV:(h + 1) * DV]) * hn * mng[:, h * DV:(h + 1) * DV])

            m_new = m_t[lc - 1:lc, :]
            bc_last = bc[lc - 1:lc, :]
            decay = jnp.exp(bc_last + m_prev - m_new)
            ks = kh * jnp.exp(bc_last - bc + li_c - m_new)
            c_ref[0, h] = decay * c_prev + _dot_tn(ks.astype(BF16), vh)
            n_ref[0, h:h + 1, :] = decay * n_prev + jnp.sum(ks, 0, keepdims=True)
            m_ref[0, h:h + 1, :] = jnp.broadcast_to(m_new, (1, LANES))

    aq = _dot(xb, waq_ref[...]) * (HD ** -0.5)
    akv = _dot(xb, wakv_ref[...])
    kk = akv[:, 0:KVH * HD]
    vv = akv[:, KVH * HD:2 * KVH * HD]
    lo = lax.broadcasted_iota(jnp.int32, (tm, LANES), 1) < HD
    k2 = [a.astype(BF16) for a in _dup_halves(kk, lo)]
    v2 = [a.astype(BF16) for a in _dup_halves(vv, lo)]
    lo_q = lax.broadcasted_iota(jnp.int32, (WIN, LANES), 1) < HD
    qi =lax.broadcasted_iota(jnp.int32, (WIN, 2 * WIN), 0)
    kj = lax.broadcasted_iota(jnp.int32, (WIN, 2 * WIN), 1)
    band = (kj >= qi) & (kj <= qi + WIN)
    first_floor = jnp.where(s == 0, WIN, 0)
    for j in range(tm // WIN):
        r0 = j * WIN
        mask = band & (kj >= first_floor) if j == 0 else band
        for g in range(KVH):
            if j == 0:
                kprev, vprev = kvc[g], kvc[KVH + g]
            else:
                kprev, vprev = k2[g][r0 - WIN:r0], v2[g][r0 - WIN:r0]
            kcat = jnp.concatenate([kprev, k2[g][r0:r0 + WIN]], axis=0)
            vcat = jnp.concatenate([vprev, v2[g][r0:r0 + WIN]], axis=0)
            for pj in range(g * GROUP // 2, (g + 1) * GROUP // 2):
                qpair = aq[r0:r0 + WIN, pj * LANES:(pj + 1) * LANES]
                outs = []
                for par in range(2):
                    sink = sink_ref[2 * pj + par]
                    qm = jnp.where(lo_q if par == 0 else ~lo_q, qpair, 0.0).astype(BF16)
                    sc = jnp.where(mask, _dot_nt(qm, kcat), NEG_INF)
                    mx = jnp.maximum(jnp.max(sc, -1, keepdims=True), sink)
                    e = jnp.exp(sc - mx)
                    dsum = jnp.sum(e, -1, keepdims=True) + jnp.exp(sink - mx)
                    outs.append(_dot(e.astype(BF16), vcat) / dsum)
                ya_s[r0:r0 + WIN, pj * LANES:(pj + 1) * LANES] = jnp.where(lo_q, outs[0], outs[1])
    for g in range(KVH):
        kvc[g] = k2[g][tm - WIN:tm]
        kvc[KVH + g] = v2[g][tm - WIN:tm]

    @pl.when(s == last)
    def _kv_out():
        kp_ref[0] = kk[tm - WIN:tm]
        vp_ref[0] = vv[tm - WIN:tm]

    gm = _dot(xb, wgm_ref[...])
    ga = _dot(xb, wga_ref[...])
    mixed = _sigmoid(gm) * ym_s[...] + _sigmoid(ga) * ya_s[...]
    r = _dot(mixed.astype(BF16), wout_ref[...])
    x1_ref[0] = _layer_norm(ALPHA * xn + r, l1g_ref[...], l1b_ref[...])


def _prompt_mixer(x, wts, *, tm=256, lc=128):
    B, S, _ = x.shape
    assert S % tm == 0 and tm % lc == 0 and tm % WIN == 0 and tm >= WIN
    grid = (B, S // tm)
    cs = functools.partial(_const_spec, grid_rank=2)
    in_specs = [
        pl.BlockSpec((1, tm, D), lambda b, s: (b, s, 0)),
        cs((1, D)), cs((1, D)),
        cs((D, QK_CH)), cs((D, D)), cs((D, LANES)), cs((SUBLANES, D)), cs((1, LANES)), cs((SUBLANES, 1)),
        cs((D, D)), cs((D, D)), cs((D, 2 * KVH * HD)), cs((D, D)), cs((D, D)),
        cs((CONV_W, QK_CH)), cs((1, QK_CH)), cs((1, D)),
        pl.BlockSpec(memory_space=pltpu.SMEM),
        cs((D, D)), cs((1, D)), cs((1, D)),
    ]
    out_shape = (
        jax.ShapeDtypeStruct((B, S, D), F32),
        jax.ShapeDtypeStruct((B, MH, DK, DV), F32),
        jax.ShapeDtypeStruct((B, MH, DK), F32),
        jax.ShapeDtypeStruct((B, SUBLANES, LANES), F32),
        jax.ShapeDtypeStruct((B, CONV_W - 1, QK_CH), F32),
        jax.ShapeDtypeStruct((B, WIN, KVH * HD), F32),
        jax.ShapeDtypeStruct((B, WIN, KVH * HD), F32),
    )
    out_specs = (
        pl.BlockSpec((1, tm, D), lambda b, s: (b, s, 0)),
        pl.BlockSpec((1, MH, DK, DV), lambda b, s: (b, 0, 0, 0)),
        pl.BlockSpec((1, MH, DK), lambda b, s: (b, 0, 0)),
        pl.BlockSpec((1, SUBLANES, LANES), lambda b, s: (b, 0, 0)),
        pl.BlockSpec((1, CONV_W - 1, QK_CH), lambda b, s: (b, 0, 0)),
        pl.BlockSpec((1, WIN, KVH * HD), lambda b, s: (b, 0, 0)),
        pl.BlockSpec((1, WIN, KVH * HD), lambda b, s: (b, 0, 0)),
    )
    scratch = [
        pltpu.VMEM((tm + SUBLANES, QK_CH), F32),
        pltpu.VMEM((2 * KVH, WIN, LANES), BF16),
        pltpu.VMEM((tm, D), F32),
        pltpu.VMEM((tm, D), F32),
    ]
    return pl.pallas_call(
        functools.partial(_mixer_kernel, tm=tm, lc=lc),
        grid=grid, in_specs=in_specs, out_specs=out_specs, out_shape=out_shape,
        scratch_shapes=scratch, name="prompt_mixer",
        compiler_params=pltpu.CompilerParams(
            dimension_semantics=("arbitrary", "arbitrary"), vmem_limit_bytes=56 * 1024 * 1024),
    )(x, wts["ln_in_g"], wts["ln_in_b"], wts["w_qk"], wts["w_mv"], wts["w_if"], wts["w_ift"],
      wts["b_if"], wts["b_ift"], wts["w_mo"], wts["w_aq"], wts["w_akv"], wts["w_gm"], wts["w_ga"],
      wts["conv_w"], wts["conv_b"], wts["m_norm_g"], wts["sinks"], wts["w_out"], wts["ln1_g"], wts["ln1_b"])


_PROJ_NAMES = ("xn", "zqk", "vm", "smo", "aq", "k2", "v2", "kvl", "sgm", "sga", "gcol", "grow", "bcol", "brow")


def _mixer_project(x, slot, w, buf, *, tm, lc):
    xn = _layer_norm(x, w["lng"][...], w["lnb"][...])
    xb = xn.astype(BF16)
    buf["xn"][slot] = xn
    buf["zqk"][slot] = _dot(xb, w["wqk"][...])
    buf["vm"][slot] = _dot(xb, w["wmv"][...]).astype(BF16)
    buf["smo"][slot] = _sigmoid(_dot(xb, w["wmo"][...])) * w["mng"][...]
    aq = _dot(xb, w["waq"][...]) * (HD ** -0.5)
    lo_d = (lax.broadcasted_iota(jnp.int32, (tm, D), 1) & (LANES - 1)) < HD
    buf["aq"][slot, 0] = jnp.where(lo_d, aq, 0.0).astype(BF16)
    buf["aq"][slot, 1] = jnp.where(lo_d, 0.0, aq).astype(BF16)
    akv = _dot(xb, w["wakv"][...])
    lo = lax.broadcasted_iota(jnp.int32, (tm, LANES), 1) < HD
    k2 = _dup_halves(akv[:, 0:LANES], lo)
    v2 = _dup_halves(akv[:, LANES:2 * LANES], lo)
    for g in range(KVH):
        buf["k2"][slot, g] = k2[g].astype(BF16)
        buf["v2"][slot, g] = v2[g].astype(BF16)
    buf["kvl"][slot] = akv[tm - WIN:tm, :]
    buf["sgm"][slot] = _sigmoid(_dot(xb, w["wgm"][...]))
    buf["sga"][slot] = _sigmoid(_dot(xb, w["wga"][...]))
    gcol = _dot(xb, w["wif"][...]) + w["bif"][...]
    grow = _dot_nt(w["wift"][...], xb) + w["bift"][...]
    buf["gcol"][slot] = gcol
    buf["grow"][slot] = grow
    lscol = _log_sigmoid(gcol)
    lsrow = _log_sigmoid(grow)
    ri = lax.broadcasted_iota(jnp.int32, (lc, lc), 0)
    ci = lax.broadcasted_iota(jnp.int32, (lc, lc), 1)
    tril = (ci <= ri).astype(F32)
    triu = (ri <= ci).astype(F32)
    for c in range(tm // lc):
        r0 = c * lc
        buf["bcol"][slot, r0:r0 + lc, :] = _dot_exact(tril, lscol[r0:r0 + lc, :])
        buf["brow"][slot, :, r0:r0 + lc] = _dot_exact(lsrow[:, r0:r0 + lc], triu)


def _mixer_consume(slot, first, w, buf, st, x1_ref, row0, *, tm, lc):
    c_ref, n_ref, m_ref, ctail, kvc, ym_s, ya_s = st
    z = buf["zqk"][slot]
    tail = ctail[...]
    ctail[...] = z[tm - SUBLANES:tm, :]
    cw = w["cw"][...]
    rows8 = lax.broadcasted_iota(jnp.int32, (SUBLANES, QK_CH), 0)
    y = cw[CONV_W - 1:CONV_W] * z + w["cb"][...]
    for sh in range(1, CONV_W):
        zr = pltpu.roll(z, sh, 0)
        head = jnp.where(rows8 < sh, pltpu.roll(tail, sh, 0), zr[0:SUBLANES])
        zr = jnp.concatenate([head, zr[SUBLANES:]], axis=0)
        y = y + cw[CONV_W - 1 - sh:CONV_W - sh] * zr
    qk = y * _sigmoid(y)

    ri = lax.broadcasted_iota(jnp.int32, (lc, lc), 0)
    ci = lax.broadcasted_iota(jnp.int32, (lc, lc), 1)
    causal = ci <= ri
    for c in range(tm // lc):
        r0 = c * lc
        for h in range(MH):
            qh = qk[r0:r0 + lc, h * DK:(h + 1) * DK]
            kh = qk[r0:r0 + lc, MH * DK + h * DK:MH * DK + (h + 1) * DK] * (DK ** -0.5)
            vh = buf["vm"][slot, r0:r0 + lc, h * DV:(h + 1) * DV]
            qb = qh.astype(BF16)
            bc = buf["bcol"][slot, r0:r0 + lc, MH + h:MH + h + 1]
            br = buf["brow"][slot, MH + h:MH + h + 1, r0:r0 + lc]
            li_c = buf["gcol"][slot, r0:r0 + lc, h:h + 1]
            li_r = buf["grow"][slot, h:h + 1, r0:r0 + lc]
            m_prev = m_ref[0, h:h + 1, 0:1]
            c_prev = c_ref[0, h]
            n_prev = n_ref[0, h:h + 1, :]

            dmat = jnp.where(causal, bc - br + li_r, NEG_INF)
            m_inter = bc + m_prev
            m_t = jnp.maximum(m_inter, jnp.max(dmat, -1, keepdims=True))
            wgt = jnp.exp(dmat - m_t) * _dot_nt(qb, kh.astype(BF16))
            inter = jnp.exp(m_inter - m_t)
            num = _dot(wgt.astype(BF16), vh) + inter * _dot(qb, c_prev.astype(BF16))
            den = jnp.sum(wgt, -1, keepdims=True) + inter * jnp.sum(qh * n_prev, -1, keepdims=True)
            denom = jnp.maximum(jnp.abs(den), jnp.exp(-m_t))
            hh = num / denom
            hn = hh * lax.rsqrt(jnp.mean(hh * hh, -1, keepdims=True) + RMS_EPS)
            ym_s[r0:r0 + lc, h * DV:(h + 1) * DV] = buf["smo"][slot, r0:r0 + lc, h * DV:(h + 1) * DV] * hn

            m_new = m_t[lc - 1:lc, :]
            bc_last = bc[lc - 1:lc, :]
            decay = jnp.exp(bc_last + m_prev - m_new)
            ks = kh * jnp.exp(bc_last - bc + li_c - m_new)
            c_ref[0, h] = decay * c_prev + _dot_tn(ks.astype(BF16), vh)
            n_ref[0, h:h + 1, :] = decay * n_prev + jnp.sum(ks, 0, keepdims=True)
            m_ref[0, h:h + 1, :] = jnp.broadcast_to(m_new, (1, LANES))

    lo_q = lax.broadcasted_iota(jnp.int32, (WIN, LANES), 1) < HD
    qi = lax.broadcasted_iota(jnp.int32, (WIN, 2 * WIN), 0)
    kj = lax.broadcasted_iota(jnp.int32, (WIN, 2 * WIN), 1)
    band = (kj >= qi) & (kj <= qi + WIN)
    for j in range(tm // WIN):
        r0 = j * WIN
        if j == 0 and first is not None:
            mask = band & (kj >= jnp.where(first, WIN, 0))
        else:
            mask = band
        for g in range(KVH):
            if j == 0:
                kprev, vprev = kvc[g], kvc[KVH + g]
            else:
                kprev, vprev = buf["k2"][slot, g, r0 - WIN:r0, :], buf["v2"][slot, g, r0 - WIN:r0, :]
            kcat = jnp.concatenate([kprev, buf["k2"][slot, g, r0:r0 + WIN, :]], axis=0)
            vcat = jnp.concatenate([vprev, buf["v2"][slot, g, r0:r0 + WIN, :]], axis=0)
            for pj in range(g * GROUP // 2, (g + 1) * GROUP // 2):
                outs = []
                for par in range(2):
                    sink = w["sink"][2 * pj + par]
                    qm = buf["aq"][slot, par, r0:r0 + WIN, pj * LANES:(pj + 1) * LANES]
                    sc = jnp.where(mask, _dot_nt(qm, kcat), NEG_INF)
                    mx = jnp.maximum(jnp.max(sc, -1, keepdims=True), sink)
                    e = jnp.exp(sc - mx)
                    dsum = jnp.sum(e, -1, keepdims=True) + jnp.exp(sink - mx)
                    outs.append(_dot(e.astype(BF16), vcat) / dsum)
                ya_s[r0:r0 + WIN, pj * LANES:(pj + 1) * LANES] = jnp.where(lo_q, outs[0], outs[1])
    for g in range(KVH):
        kvc[g] = buf["k2"][slot, g, tm - WIN:tm, :]
        kvc[KVH + g] = buf["v2"][slot, g, tm - WIN:tm, :]

    mixed = buf["sgm"][slot] * ym_s[...] + buf["sga"][slot] * ya_s[...]
    r = _dot(mixed.astype(BF16), w["wout"][...])
    x1_ref[0, row0:row0 + tm, :] = _layer_norm(ALPHA * buf["xn"][slot] + r, w["l1g"][...], w["l1b"][...])


_MIXER_W_NAMES = ("lng", "lnb", "wqk", "wmv", "wif", "wift", "bif", "bift", "wmo", "waq", "wakv", "wgm", "wga",
                  "cw", "cb", "mng", "sink", "wout", "l1g", "l1b")


def _mixer2_kernel(*refs, tm, lc, npb):
    nw = len(_MIXER_W_NAMES)
    x0_ref, xa_ref, xb_ref = refs[0:3]
    w = dict(zip(_MIXER_W_NAMES, refs[3:3 + nw]))
    x1_ref, c_ref, n_ref, m_ref, conv_ref, kp_ref, vp_ref = refs[3 + nw:10 + nw]
    scratch = refs[10 + nw:]
    buf = dict(zip(_PROJ_NAMES, scratch[:len(_PROJ_NAMES)]))
    ctail, kvc, ym_s, ya_s = scratch[len(_PROJ_NAMES):]
    st = (c_ref, n_ref, m_ref, ctail, kvc, ym_s, ya_s)
    k = pl.program_id(0)
    first = (2 * k) % npb == 0

    @pl.when(k == 0)
    def _prologue():
        _mixer_project(x0_ref[0], 0, w, buf, tm=tm, lc=lc)

    @pl.when(first)
    def _init():
        c_ref[...] = jnp.zeros_like(c_ref)
        n_ref[...] = jnp.zeros_like(n_ref)
        m_ref[...] = jnp.zeros_like(m_ref)
        ctail[...] = jnp.zeros_like(ctail)
        kvc[...] = jnp.zeros_like(kvc)

    _mixer_project(xa_ref[0], 1, w, buf, tm=tm, lc=lc)
    _mixer_consume(0, first, w, buf, st, x1_ref, 0, tm=tm, lc=lc)
    _mixer_project(xb_ref[0], 0, w, buf, tm=tm, lc=lc)
    _mixer_consume(1, None, w, buf, st, x1_ref, tm, tm=tm, lc=lc)

    @pl.when((2 * k + 1) % npb == npb - 1)
    def _state_out():
        conv_ref[0] = ctail[SUBLANES - (CONV_W - 1):SUBLANES, :]
        kp_ref[0] = buf["kvl"][1, :, 0:LANES]
        vp_ref[0] = buf["kvl"][1, :, LANES:2 * LANES]


def _prompt_mixer2(x, wts, *, tm=256, lc=128):
    B, S, _ = x.shape
    npb = S // tm
    assert S % tm == 0 and npb % 2 == 0 and tm % lc == 0 and tm % WIN == 0
    nt = B * npb
    hpb = npb // 2
    cs = functools.partial(_const_spec, grid_rank=1)

    def tile_map(off):
        def imap(k):
            t = jnp.minimum(2 * k + off, nt - 1)
            return (t // npb, t % npb, 0)
        return imap

    in_specs = [
        pl.BlockSpec((1, tm, D), lambda k: (0, 0, 0), pipeline_mode=pl.Buffered(1)),
        pl.BlockSpec((1, tm, D), tile_map(1)),
        pl.BlockSpec((1, tm, D), tile_map(2)),
        cs((1, D)), cs((1, D)),
        cs((D, QK_CH)), cs((D, D)), cs((D, LANES)), cs((SUBLANES, D)), cs((1, LANES)), cs((SUBLANES, 1)),
        cs((D, D)), cs((D, D)), cs((D, 2 * KVH * HD)), cs((D, D)), cs((D, D)),
        cs((CONV_W, QK_CH)), cs((1, QK_CH)), cs((1, D)),
        pl.BlockSpec(memory_space=pltpu.SMEM),
        cs((D, D)), cs((1, D)), cs((1, D)),
    ]
    out_shape = (
        jax.ShapeDtypeStruct((B, S, D), F32),
        jax.ShapeDtypeStruct((B, MH, DK, DV), F32),
        jax.ShapeDtypeStruct((B, MH, DK), F32),
        jax.ShapeDtypeStruct((B, SUBLANES, LANES), F32),
        jax.ShapeDtypeStruct((B, CONV_W - 1, QK_CH), F32),
        jax.ShapeDtypeStruct((B, WIN, KVH * HD), F32),
        jax.ShapeDtypeStruct((B, WIN, KVH * HD), F32),
    )
    out_specs = (
        pl.BlockSpec((1, 2 * tm, D), lambda k: (k // hpb, k % hpb, 0)),
        pl.BlockSpec((1, MH, DK, DV), lambda k: (k // hpb, 0, 0, 0)),
        pl.BlockSpec((1, MH, DK), lambda k: (k // hpb, 0, 0)),
        pl.BlockSpec((1, SUBLANES, LANES), lambda k: (k // hpb, 0, 0)),
        pl.BlockSpec((1, CONV_W - 1, QK_CH), lambda k: (k // hpb, 0, 0)),
        pl.BlockSpec((1, WIN, KVH * HD), lambda k: (k // hpb, 0, 0)),
        pl.BlockSpec((1, WIN, KVH * HD), lambda k: (k // hpb, 0, 0)),
    )
    proj_scratch = {
        "xn": pltpu.VMEM((2, tm, D), F32), "zqk": pltpu.VMEM((2, tm, QK_CH), F32),
        "vm": pltpu.VMEM((2, tm, D), BF16), "smo": pltpu.VMEM((2, tm, D), F32),
        "aq": pltpu.VMEM((2, 2, tm, D), BF16),
        "k2": pltpu.VMEM((2, KVH, tm, LANES), BF16), "v2": pltpu.VMEM((2, KVH, tm, LANES), BF16),
        "kvl": pltpu.VMEM((2, WIN, 2 * LANES), F32),
        "sgm": pltpu.VMEM((2, tm, D), F32), "sga": pltpu.VMEM((2, tm, D), F32),
        "gcol": pltpu.VMEM((2, tm, LANES), F32), "grow": pltpu.VMEM((2, SUBLANES, tm), F32),
        "bcol": pltpu.VMEM((2, tm, LANES), F32), "brow": pltpu.VMEM((2, SUBLANES, tm), F32),
    }
    scratch = [proj_scratch[n] for n in _PROJ_NAMES] + [
        pltpu.VMEM((SUBLANES, QK_CH), F32),
        pltpu.VMEM((2 * KVH, WIN, LANES), BF16),
        pltpu.VMEM((tm, D), F32),
        pltpu.VMEM((tm, D), F32),
    ]
    return pl.pallas_call(
        functools.partial(_mixer2_kernel, tm=tm, lc=lc, npb=npb),
        grid=(nt // 2,), in_specs=in_specs, out_specs=out_specs, out_shape=out_shape,
        scratch_shapes=scratch, name="prompt_mixer",
        compiler_params=pltpu.CompilerParams(
            dimension_semantics=("arbitrary",), vmem_limit_bytes=56 * 1024 * 1024),
    )(x, x, x, wts["ln_in_g"], wts["ln_in_b"], wts["w_qk"], wts["w_mv"], wts["w_if"], wts["w_ift"],
      wts["b_if"], wts["b_ift"], wts["w_mo"], wts["w_aq"], wts["w_akv"], wts["w_gm"], wts["w_ga"],
      wts["conv_w"], wts["conv_b"], wts["m_norm_g"], wts["sinks"], wts["w_out"], wts["ln1_g"], wts["ln1_b"])


def _ffn_kernel(x_ref, p_ref, wg_ref, wu_ref, wd_ref, wpg_ref, wp_ref, g_ref, b_ref, o_ref, *, fc):
    x = x_ref[...]
    xb = x.astype(BF16)
    acc = ALPHA * x + _sigmoid(_dot(xb, wpg_ref[...])) * _dot(p_ref[...].astype(BF16), wp_ref[...])
    for c in range(DFF // fc):
        g = _dot(xb, wg_ref[:, c * fc:(c + 1) * fc])
        u = _dot(xb, wu_ref[:, c * fc:(c + 1) * fc])
        hcat = (g * _sigmoid(g) * u).astype(BF16)
        acc = acc + _dot(hcat, wd_ref[c * fc:(c + 1) * fc, :])
    o_ref[...] = _layer_norm(acc, g_ref[...], b_ref[...])


def _ffn(x, p, wts, *, tm, fc=1408):
    N = x.shape[0]
    assert N % tm == 0 and DFF % fc == 0
    cs = functools.partial(_const_spec, grid_rank=1)
    return pl.pallas_call(
        functools.partial(_ffn_kernel, fc=fc),
        grid=(N // tm,),
        in_specs=[pl.BlockSpec((tm, D), lambda i: (i, 0)), pl.BlockSpec((tm, PD), lambda i: (i, 0)),
                  cs((D, DFF)), cs((D, DFF)), cs((DFF, D)), cs((D, D)), cs((PD, D)), cs((1, D)), cs((1, D))],
        out_specs=pl.BlockSpec((tm, D), lambda i: (i, 0)),
        out_shape=jax.ShapeDtypeStruct((N, D), F32),
        name="ffn",
        compiler_params=pltpu.CompilerParams(
            dimension_semantics=("arbitrary",), vmem_limit_bytes=56 * 1024 * 1024),
    )(x, p, wts["w_g"], wts["w_u"], wts["w_d"], wts["w_pg"], wts["w_p"], wts["ln2_g"], wts["ln2_b"])


Z_V = 0
Z_MO = Z_V + D
Z_GM = Z_MO + D
Z_GA = Z_GM + D
Z_XN = Z_GA + D
Z_Q = Z_XN + D
Z_KW = Z_Q + MH * DK
Z_SC = Z_KW + MH * DK
Z_KN = Z_SC + LANES
Z_VN = Z_KN + LANES
Z_W = Z_VN + LANES


def _sample_pre_kernel(x_ref, lng_ref, lnb_ref, wqk_ref, wmv_ref, wif_ref, bif_ref, wmo_ref, waq_ref,
                       wakv_ref, wgm_ref, wga_ref, cw_ref, cb_ref, cs_ref, n_ref, m_ref,
                       z_ref, qs_ref, convo_ref, no_ref, mo_ref, *, nb, tb):
    xn = _layer_norm(x_ref[...], lng_ref[...], lnb_ref[...])
    xb = xn.astype(BF16)
    zqk = _dot(xb, wqk_ref[...])
    cw = cw_ref[...]
    y = (cw[3:4] * zqk + cw[2:3] * cs_ref[:, 2 * QK_CH:3 * QK_CH] + cw[1:2] * cs_ref[:, QK_CH:2 * QK_CH]
         + cw[0:1] * cs_ref[:, 0:QK_CH] + cb_ref[...])
    qk = y * _sigmoid(y)
    convo_ref[:, 0:2 * QK_CH] = cs_ref[:, QK_CH:3 * QK_CH]
    convo_ref[:, 2 * QK_CH:3 * QK_CH] = zqk

    gcol = _dot(xb, wif_ref[...]) + bif_ref[...]
    ig = gcol[:, 0:MH]
    m_inter = _log_sigmoid(gcol[:, MH:2 * MH]) + m_ref[...]
    m_t = jnp.maximum(m_inter, ig)
    inter = jnp.exp(m_inter - m_t)
    wi = jnp.exp(ig - m_t)
    enm = jnp.exp(-m_t)
    mo_ref[...] = m_t

    z_ref[:, Z_V:Z_V + D] = _dot(xb, wmv_ref[...])
    z_ref[:, Z_MO:Z_MO + D] = _dot(xb, wmo_ref[...])
    z_ref[:, Z_GM:Z_GM + D] = _dot(xb, wgm_ref[...])
    z_ref[:, Z_GA:Z_GA + D] = _dot(xb, wga_ref[...])
    z_ref[:, Z_XN:Z_XN + D] = xn
    z_ref[:, Z_Q:Z_Q + MH * DK] = qk[:, 0:MH * DK]
    sc = jnp.zeros((nb, LANES), F32)
    lane = lax.broadcasted_iota(jnp.int32, (nb, LANES), 1)
    for h in range(MH):
        qh = qk[:, h * DK:(h + 1) * DK]
        kh = qk[:, MH * DK + h * DK:MH * DK + (h + 1) * DK] * (DK ** -0.5)
        nh = n_ref[:, h * DK:(h + 1) * DK]
        wv = wi[:, h:h + 1] * jnp.sum(qh * kh, -1, keepdims=True)
        den = wv + inter[:, h:h + 1] * jnp.sum(qh * nh, -1, keepdims=True)
        denom = jnp.maximum(jnp.abs(den), enm[:, h:h + 1])
        kw = wi[:, h:h + 1] * kh
        z_ref[:, Z_KW + h * DK:Z_KW + (h + 1) * DK] = kw
        no_ref[:, h * DK:(h + 1) * DK] = inter[:, h:h + 1] * nh + kw
        sc = jnp.where(lane == h, inter[:, h:h + 1], sc)
        sc = jnp.where(lane == MH + h, wv, sc)
        sc = jnp.where(lane == 2 * MH + h, denom, sc)
    z_ref[:, Z_SC:Z_SC + LANES] = sc

    akv = _dot(xb, wakv_ref[...])
    z_ref[:, Z_KN:Z_KN + LANES] = akv[:, 0:LANES]
    z_ref[:, Z_VN:Z_VN + LANES] = akv[:, LANES:2 * LANES]
    aq = _dot(xb, waq_ref[...]) * (HD ** -0.5)
    lo = lax.broadcasted_iota(jnp.int32, (nb, LANES), 1) < HD
    for h in range(AH):
        pair = aq[:, (h // 2) * LANES:(h // 2 + 1) * LANES]
        qm = jnp.where(lo if h % 2 == 0 else ~lo, pair, 0.0)
        for t in range(nb // tb):
            qs_ref[(t * AH + h) * tb:(t * AH + h + 1) * tb, :] = qm[t * tb:(t + 1) * tb, :]


def _sample_state_kernel(z_ref, qs_ref, sink_ref, c_ref, ck_ref, cv_ref,
                         co_ref, qc_ref, os_ref, ko_ref, vo_ref, *, tb):
    lo = lax.broadcasted_iota(jnp.int32, (WIN, LANES), 1) < HD
    row = lax.broadcasted_iota(jnp.int32, (WIN, LANES), 0)
    rowt = lax.broadcasted_iota(jnp.int32, (tb, DK), 0)
    sinks = sink_ref[...]
    q_all = z_ref[:, Z_Q:Z_Q + MH * DK]
    kw_all = z_ref[:, Z_KW:Z_KW + MH * DK]
    v_all = z_ref[:, Z_V:Z_V + D]
    sc = z_ref[:, Z_SC:Z_SC + LANES]
    kn_all = z_ref[:, Z_KN:Z_KN + LANES]
    vn_all = z_ref[:, Z_VN:Z_VN + LANES]
    kn2_all = _dup_halves(kn_all, lo[0:tb])
    vn2_all = _dup_halves(vn_all, lo[0:tb])
    for i in range(tb):
        for h in range(MH):
            cb = c_ref[i, h]
            qc = _dot(q_all[:, h * DK:(h + 1) * DK].astype(BF16), cb.astype(BF16))
            qc_ref[i:i + 1, h * DV:(h + 1) * DV] = qc[i:i + 1]
            kw_i = jnp.where(rowt == i, kw_all[:, h * DK:(h + 1) * DK], 0.0)
            outer = _dot_tn(kw_i, v_all[:, h * DV:(h + 1) * DV])
            co_ref[i, h] = sc[i:i + 1, h:h + 1] * cb + outer
        kc = ck_ref[i]
        vc = cv_ref[i]
        kn = kn_all[i:i + 1]
        vn = vn_all[i:i + 1]
        k2 = _dup_halves(kc, lo)
        v2 = _dup_halves(vc, lo)
        kn2 = [a[i:i + 1] for a in kn2_all]
        vn2 = [a[i:i + 1] for a in vn2_all]
        qrow = qs_ref[pl.ds(i, AH, stride=tb), :]
        outs = []
        for g in range(KVH):
            qg = qrow[g * GROUP:(g + 1) * GROUP]
            s = _dot_nt(qg.astype(BF16), k2[g].astype(BF16))
            s_self = jnp.sum(qg * kn2[g], -1, keepdims=True)
            sg = sinks[g * GROUP:(g + 1) * GROUP]
            mx = jnp.maximum(jnp.maximum(jnp.max(s, -1, keepdims=True), s_self), sg)
            e = jnp.exp(s - mx)
            es = jnp.exp(s_self - mx)
            dsum = jnp.sum(e, -1, keepdims=True) + es + jnp.exp(sg - mx)
            outs.append((_dot(e.astype(BF16), v2[g].astype(BF16)) + es * vn2[g]) / dsum)
        os_ref[pl.ds(i, AH, stride=tb), :] = jnp.concatenate(outs, axis=0)
        ko_ref[i] = jnp.where(row == WIN - 1, kn, pltpu.roll(kc, WIN - 1, 0))
        vo_ref[i] = jnp.where(row == WIN - 1, vn, pltpu.roll(vc, WIN - 1, 0))


def _sample_post_kernel(z_ref, qc_ref, os_ref, mng_ref, wout_ref, l1g_ref, l1b_ref, x1_ref, *, nb, tb):
    sc = z_ref[:, Z_SC:Z_SC + LANES]
    mng = mng_ref[...]
    lo = lax.broadcasted_iota(jnp.int32, (nb, LANES), 1) < HD
    ym = []
    for h in range(MH):
        v = z_ref[:, Z_V + h * DV:Z_V + (h + 1) * DV]
        num = sc[:, MH + h:MH + h + 1] * v + sc[:, h:h + 1] * qc_ref[:, h * DV:(h + 1) * DV]
        hh = num / sc[:, 2 * MH + h:2 * MH + h + 1]
        hn = hh * lax.rsqrt(jnp.mean(hh * hh, -1, keepdims=True) + RMS_EPS)
        ym.append(_sigmoid(z_ref[:, Z_MO + h * DV:Z_MO + (h + 1) * DV]) * hn * mng[:, h * DV:(h + 1) * DV])
    ym = jnp.concatenate(ym, axis=1)
    ya = []
    for pj in range(AH // 2):
        halves = []
        for par in range(2):
            h = 2 * pj + par
            halves.append(jnp.concatenate(
                [os_ref[(t * AH + h) * tb:(t * AH + h + 1) * tb, :] for t in range(nb // tb)], axis=0))
        ya.append(jnp.where(lo, halves[0], halves[1]))
    ya = jnp.concatenate(ya, axis=1)
    mixed = _sigmoid(z_ref[:, Z_GM:Z_GM + D]) * ym + _sigmoid(z_ref[:, Z_GA:Z_GA + D]) * ya
    r = _dot(mixed.astype(BF16), wout_ref[...])
    x1_ref[...] = _layer_norm(ALPHA * z_ref[:, Z_XN:Z_XN + D] + r, l1g_ref[...], l1b_ref[...])


def _sample_mixer(x, c0, n0, m0, conv0, k0, v0, wts, *, tb=8):
    nb = x.shape[0]
    assert nb % tb == 0
    vmem = pltpu.CompilerParams(dimension_semantics=("arbitrary",), vmem_limit_bytes=56 * 1024 * 1024)
    full = lambda shape: pl.BlockSpec(shape, lambda i: (0,) * len(shape))
    pre_in = [x, wts["ln_in_g"], wts["ln_in_b"], wts["w_qk"], wts["w_mv"], wts["w_if"], wts["b_if"],
              wts["w_mo"], wts["w_aq"], wts["w_akv"], wts["w_gm"], wts["w_ga"], wts["conv_w"], wts["conv_b"],
              conv0, n0, m0]
    z, qs, conv_new, n_new, m_new = pl.pallas_call(
        functools.partial(_sample_pre_kernel, nb=nb, tb=tb),
        grid=(1,),
        in_specs=[full(a.shape) for a in pre_in],
        out_specs=(full((nb, Z_W)), full((nb * AH, LANES)), full((nb, (CONV_W - 1) * QK_CH)),
                   full((nb, MH * DK)), full((nb, MH))),
        out_shape=(jax.ShapeDtypeStruct((nb, Z_W), F32), jax.ShapeDtypeStruct((nb * AH, LANES), F32),
                   jax.ShapeDtypeStruct((nb, (CONV_W - 1) * QK_CH), F32),
                   jax.ShapeDtypeStruct((nb, MH * DK), F32), jax.ShapeDtypeStruct((nb, MH), F32)),
        name="sample_pre", compiler_params=vmem,
    )(*pre_in)

    c_new, qc, os_, k_new, v_new = pl.pallas_call(
        functools.partial(_sample_state_kernel, tb=tb),
        grid=(nb // tb,),
        in_specs=[pl.BlockSpec((tb, Z_W), lambda i: (i, 0)),
                  pl.BlockSpec((tb * AH, LANES), lambda i: (i, 0)),
                  pl.BlockSpec((AH, 1), lambda i: (0, 0)),
                  pl.BlockSpec((tb, MH, DK, DV), lambda i: (i, 0, 0, 0)),
                  pl.BlockSpec((tb, WIN, LANES), lambda i: (i, 0, 0)),
                  pl.BlockSpec((tb, WIN, LANES), lambda i: (i, 0, 0))],
        out_specs=(pl.BlockSpec((tb, MH, DK, DV), lambda i: (i, 0, 0, 0)),
                   pl.BlockSpec((tb, D), lambda i: (i, 0)),
                   pl.BlockSpec((tb * AH, LANES), lambda i: (i, 0)),
                   pl.BlockSpec((tb, WIN, LANES), lambda i: (i, 0, 0)),
                   pl.BlockSpec((tb, WIN, LANES), lambda i: (i, 0, 0))),
        out_shape=(jax.ShapeDtypeStruct((nb, MH, DK, DV), F32), jax.ShapeDtypeStruct((nb, D), F32),
                   jax.ShapeDtypeStruct((nb * AH, LANES), F32),
                   jax.ShapeDtypeStruct((nb, WIN, LANES), F32), jax.ShapeDtypeStruct((nb, WIN, LANES), F32)),
        name="sample_state", compiler_params=vmem,
    )(z, qs, wts["sinks_col"], c0, k0, v0)

    post_in = [z, qc, os_, wts["m_norm_g"], wts["w_out"], wts["ln1_g"], wts["ln1_b"]]
    x1 = pl.pallas_call(
        functools.partial(_sample_post_kernel, nb=nb, tb=tb),
        grid=(1,),
        in_specs=[full(a.shape) for a in post_in],
        out_specs=full((nb, D)),
        out_shape=jax.ShapeDtypeStruct((nb, D), F32),
        name="sample_post", compiler_params=vmem,
    )(*post_in)
    return x1, c_new, n_new, m_new, conv_new, k_new, v_new


def _prep_weights(ln_in_g, ln_in_b, w_in, b_igate, b_fgate, conv_w, conv_b, m_norm_g, attn_sinks, w_out,
                  ln1_g, ln1_b, w_gate_up, w_down, ln2_g, ln2_b, w_ple, w_ple_gate):
    w = w_in[0]
    row = lambda a: a.reshape(1, -1).astype(F32)
    w_if = jnp.pad(w[:, O_MI:O_MO], ((0, 0), (0, LANES - 2 * MH)))
    b_if = jnp.pad(jnp.concatenate([b_igate[0], b_fgate[0]]), (0, LANES - 2 * MH))
    return {
        "ln_in_g": row(ln_in_g), "ln_in_b": row(ln_in_b),
        "w_qk": w[:, O_QK:O_MV].astype(BF16), "w_mv": w[:, O_MV:O_MI].astype(BF16),
        "w_if": w_if.astype(BF16), "w_ift": w[:, O_MI:O_MO].T.astype(BF16),
        "b_if": b_if.reshape(1, LANES), "b_ift": b_if[:2 * MH].reshape(2 * MH, 1),
        "w_mo": w[:, O_MO:O_AQ].astype(BF16), "w_aq": w[:, O_AQ:O_AK].astype(BF16),
        "w_akv": w[:, O_AK:O_GM].astype(BF16), "w_gm": w[:, O_GM:O_GA].astype(BF16),
        "w_ga": w[:, O_GA:IN_WIDTH].astype(BF16),
        "conv_w": conv_w[0], "conv_b": row(conv_b[0]), "m_norm_g": row(m_norm_g[0]),
        "sinks": attn_sinks[0], "sinks_col": attn_sinks[0].reshape(AH, 1),
        "w_out": w_out[0].astype(BF16), "ln1_g": row(ln1_g[0]), "ln1_b": row(ln1_b[0]),
        "w_g": w_gate_up[0][:, :DFF].astype(BF16), "w_u": w_gate_up[0][:, DFF:].astype(BF16),
        "w_d": w_down[0].astype(BF16), "w_pg": w_ple_gate[0].astype(BF16), "w_p": w_ple[0].astype(BF16),
        "ln2_g": row(ln2_g[0]), "ln2_b": row(ln2_b[0]),
    }


def _prompt_path(x, p, wts, *, tm=256, lc=128, tmf=512):
    B, S, _ = x.shape
    x1, c, n, m, conv, k, v = _prompt_mixer2(x, wts, tm=tm, lc=lc)
    y = _ffn(x1.reshape(B * S, D), p.reshape(B * S, PD), wts, tm=tmf).reshape(B, S, D)
    return (y, c[None], n[None], m[None, :, :MH, 0], conv[None],
            k.reshape(1, B, WIN, KVH, HD), v.reshape(1, B, WIN, KVH, HD))


def _sample_path(x, p, c0, n0, m0, conv0, k0, v0, wts, *, tb=8):
    nb = x.shape[0]
    x1, c, n, m, conv, k, v = _sample_mixer(
        x.reshape(nb, D), c0[0], n0[0].reshape(nb, MH * DK), m0[0],
        conv0[0].reshape(nb, (CONV_W - 1) * QK_CH), k0[0].reshape(nb, WIN, LANES),
        v0[0].reshape(nb, WIN, LANES), wts, tb=tb)
    y = _ffn(x1, p.reshape(nb, PD), wts, tm=nb).reshape(nb, 1, D)
    return (y, c[None], n.reshape(1, nb, MH, DK), m[None], conv.reshape(1, nb, CONV_W - 1, QK_CH),
            k.reshape(1, nb, WIN, KVH, HD), v.reshape(1, nb, WIN, KVH, HD))


def kernel(x_prompt, x_sample, state_mlstm_C, state_mlstm_n, state_mlstm_m, state_conv, cache_win_k, cache_win_v, p_prompt, p_sample, ln_in_g, ln_in_b, w_in, b_igate, b_fgate, conv_w, conv_b, m_norm_g, attn_sinks, w_out, ln1_g, ln1_b, w_gate_up, w_down, ln2_g, ln2_b, w_ple, w_ple_gate):
    wts = _prep_weights(ln_in_g, ln_in_b, w_in, b_igate, b_fgate, conv_w, conv_b, m_norm_g, attn_sinks,
                        w_out, ln1_g, ln1_b, w_gate_up, w_down, ln2_g, ln2_b, w_ple, w_ple_gate)
    yp, c_p, n_p, m_p, conv_p, k_p, v_p = _prompt_path(x_prompt, p_prompt[0], wts)
    ys, c_s, n_s, m_s, conv_s, k_s, v_s = _sample_path(
        x_sample, p_sample[0], state_mlstm_C, state_mlstm_n, state_mlstm_m, state_conv,
        cache_win_k, cache_win_v, wts)
    return (yp, ys, c_p, n_p, m_p, conv_p, k_p, v_p, c_s, n_s, m_s, conv_s, k_s, v_s)
```

```python
import functools

import jax
import jax.numpy as jnp
from jax import lax
from jax.experimental import pallas as pl
from jax.experimental.pallas import tpu as pltpu

F32 = jnp.float32
BF16 = jnp.bfloat16

D = 1024
MH = 4
DV = D // MH
DK = DV // 2
CONV_W = 4
QK_CH = 2 * MH * DK
AH = 16
KVH = 2
HD = D // AH
GROUP = AH // KVH
WIN = 128
DFF = 2816
PD = 256
LN_EPS = 1e-5
RMS_EPS = 1e-6
ALPHA = 2.0 ** 0.25
LANES = 128
SUBLANES = 8

O_QK = 0
O_MV = O_QK + QK_CH
O_MI = O_MV + D
O_MF = O_MI + MH
O_MO = O_MF + MH
O_AQ = O_MO + D
O_AK = O_AQ + D
O_AV = O_AK + KVH * HD
O_GM = O_AV + KVH * HD
O_GA = O_GM + D
IN_WIDTH = O_GA + D

NEG_INF = float("-inf")
LOG2E = 1.4426950408889634


def _dot(a, b):
    return jnp.dot(a, b, preferred_element_type=F32)


def _dot_nt(a, b):
    return lax.dot_general(a, b, (((1,), (1,)), ((), ())), preferred_element_type=F32)


def _dot_tn(a, b):
    return lax.dot_general(a, b, (((0,), (0,)), ((), ())), preferred_element_type=F32)


def _dot_exact(a, b):
    return jnp.dot(a, b, preferred_element_type=F32, precision=lax.Precision.HIGHEST)


def _layer_norm(x, g, b):
    mu = jnp.mean(x, -1, keepdims=True)
    xc = x - mu
    var = jnp.mean(xc * xc, -1, keepdims=True)
    return xc * lax.rsqrt(var + LN_EPS) * g + b


def _sigmoid(x):
    return 1.0 / (1.0 + jnp.exp(-x))


def _log_sigmoid(x):
    return jnp.minimum(x, 0.0) - jnp.log(1.0 + jnp.exp(-jnp.abs(x)))


def _const_spec(shape, grid_rank):
    zeros = (0,) * len(shape)
    if grid_rank == 1:
        imap = lambda i: zeros
    else:
        imap = lambda i, j: zeros
    return pl.BlockSpec(shape, imap, pipeline_mode=pl.Buffered(1))


def _dup_halves(x, lo):
    xr = pltpu.roll(x, HD, 1)
    return jnp.where(lo, x, xr), jnp.where(lo, xr, x)


def _mixer_kernel(x_ref, lng_ref, lnb_ref, wqk_ref, wmv_ref, wif_ref, wift_ref, bif_ref, bift_ref,
                  wmo_ref, waq_ref, wakv_ref, wgm_ref, wga_ref, cw_ref, cb_ref, mng_ref, sink_ref,
                  wout_ref, l1g_ref, l1b_ref,
                  x1_ref, c_ref, n_ref, m_ref, conv_ref, kp_ref, vp_ref,
                  convbuf, kvc, ym_s, ya_s, *, tm, lc):
    s = pl.program_id(1)
    last = pl.num_programs(1) - 1

    @pl.when(s == 0)
    def _init():
        c_ref[...] = jnp.zeros_like(c_ref)
        n_ref[...] = jnp.zeros_like(n_ref)
        m_ref[...] = jnp.zeros_like(m_ref)
        convbuf[0:SUBLANES, :] = jnp.zeros((SUBLANES, QK_CH), F32)
        kvc[...] = jnp.zeros_like(kvc)

    xn = _layer_norm(x_ref[0], lng_ref[...], lnb_ref[...])
    xb = xn.astype(BF16)

    zqk = _dot(xb, wqk_ref[...])
    convbuf[SUBLANES:SUBLANES + tm, :] = zqk
    cw = cw_ref[...]
    y = (cw[3:4] * zqk + cw[2:3] * convbuf[7:7 + tm, :] + cw[1:2] * convbuf[6:6 + tm, :]
         + cw[0:1] * convbuf[5:5 + tm, :] + cb_ref[...])
    qk = y * _sigmoid(y)
    convbuf[0:SUBLANES, :] = convbuf[tm:tm + SUBLANES, :]

    @pl.when(s == last)
    def _conv_out():
        conv_ref[0] = convbuf[SUBLANES - (CONV_W - 1):SUBLANES, :]

    gcol = _dot(xb, wif_ref[...]) + bif_ref[...]
    grow = _dot_nt(wift_ref[...], xb) + bift_ref[...]
    lscol = _log_sigmoid(gcol)
    lsrow = _log_sigmoid(grow)
    vm = _dot(xb, wmv_ref[...])
    mo = _dot(xb, wmo_ref[...])
    mng = mng_ref[...]
    ri = lax.broadcasted_iota(jnp.int32, (lc, lc), 0)
    ci = lax.broadcasted_iota(jnp.int32, (lc, lc), 1)
    causal = ci <= ri
    tril = causal.astype(F32)
    triu = (ri <= ci).astype(F32)
    for c in range(tm // lc):
        r0 = c * lc
        bcol = _dot_exact(tril, lscol[r0:r0 + lc, :])
        brow = _dot_exact(lsrow[:, r0:r0 + lc], triu)
        for h in range(MH):
            qh = qk[r0:r0 + lc, h * DK:(h + 1) * DK]
            kh = qk[r0:r0 + lc, MH * DK + h * DK:MH * DK + (h + 1) * DK] * (DK ** -0.5)
            vh = vm[r0:r0 + lc, h * DV:(h + 1) * DV].astype(BF16)
            qb = qh.astype(BF16)
            bc = bcol[:, MH + h:MH + h + 1]
            br = brow[MH + h:MH + h + 1, :]
            li_c = gcol[r0:r0 + lc, h:h + 1]
            li_r = grow[h:h + 1, r0:r0 + lc]
            m_prev = m_ref[0, h:h + 1, 0:1]
            c_prev = c_ref[0, h]
            n_prev = n_ref[0, h:h + 1, :]

            dmat = jnp.where(causal, bc - br + li_r, NEG_INF)
            m_inter = bc + m_prev
            m_t = jnp.maximum(m_inter, jnp.max(dmat, -1, keepdims=True))
            w = jnp.exp(dmat - m_t) * _dot_nt(qb, kh.astype(BF16))
            inter = jnp.exp(m_inter - m_t)
            num = _dot(w.astype(BF16), vh) + inter * _dot(qb, c_prev.astype(BF16))
            den = jnp.sum(w, -1, keepdims=True) + inter * jnp.sum(qh * n_prev, -1, keepdims=True)
            denom = jnp.maximum(jnp.abs(den), jnp.exp(-m_t))
            hh = num / denom
            hn = hh * lax.rsqrt(jnp.mean(hh * hh, -1, keepdims=True) + RMS_EPS)
            ym_s[r0:r0 + lc, h * DV:(h + 1) * DV] = (
                _sigmoid(mo[r0:r0 + lc, h * DV:(h + 1) * DV]) * hn * mng[:, h * DV:(h + 1) * DV])

            m_new = m_t[lc - 1:lc, :]
            bc_last = bc[lc - 1:lc, :]
            decay = jnp.exp(bc_last + m_prev - m_new)
            ks = kh * jnp.exp(bc_last - bc + li_c - m_new)
            c_ref[0, h] = decay * c_prev + _dot_tn(ks.astype(BF16), vh)
            n_ref[0, h:h + 1, :] = decay * n_prev + jnp.sum(ks, 0, keepdims=True)
            m_ref[0, h:h + 1, :] = jnp.broadcast_to(m_new, (1, LANES))

    aq = _dot(xb, waq_ref[...]) * (HD ** -0.5)
    akv = _dot(xb, wakv_ref[...])
    kk = akv[:, 0:KVH * HD]
    vv = akv[:, KVH * HD:2 * KVH * HD]
    lo = lax.broadcasted_iota(jnp.int32, (tm, LANES), 1) < HD
    k2 = [a.astype(BF16) for a in _dup_halves(kk, lo)]
    v2 = [a.astype(BF16) for a in _dup_halves(vv, lo)]
    lo_q = lax.broadcasted_iota(jnp.int32, (WIN, LANES), 1) < HD
    qi =lax.broadcasted_iota(jnp.int32, (WIN, 2 * WIN), 0)
    kj = lax.broadcasted_iota(jnp.int32, (WIN, 2 * WIN), 1)
    band = (kj >= qi) & (kj <= qi + WIN)
    first_floor = jnp.where(s == 0, WIN, 0)
    for j in range(tm // WIN):
        r0 = j * WIN
        mask = band & (kj >= first_floor) if j == 0 else band
        for g in range(KVH):
            if j == 0:
                kprev, vprev = kvc[g], kvc[KVH + g]
            else:
                kprev, vprev = k2[g][r0 - WIN:r0], v2[g][r0 - WIN:r0]
            kcat = jnp.concatenate([kprev, k2[g][r0:r0 + WIN]], axis=0)
            vcat = jnp.concatenate([vprev, v2[g][r0:r0 + WIN]], axis=0)
            for pj in range(g * GROUP // 2, (g + 1) * GROUP // 2):
                qpair = aq[r0:r0 + WIN, pj * LANES:(pj + 1) * LANES]
                outs = []
                for par in range(2):
                    sink = sink_ref[2 * pj + par]
                    qm = jnp.where(lo_q if par == 0 else ~lo_q, qpair, 0.0).astype(BF16)
                    sc = jnp.where(mask, _dot_nt(qm, kcat), NEG_INF)
                    mx = jnp.maximum(jnp.max(sc, -1, keepdims=True), sink)
                    e = jnp.exp(sc - mx)
                    dsum = jnp.sum(e, -1, keepdims=True) + jnp.exp(sink - mx)
                    outs.append(_dot(e.astype(BF16), vcat) / dsum)
                ya_s[r0:r0 + WIN, pj * LANES:(pj + 1) * LANES] = jnp.where(lo_q, outs[0], outs[1])
    for g in range(KVH):
        kvc[g] = k2[g][tm - WIN:tm]
        kvc[KVH + g] = v2[g][tm - WIN:tm]

    @pl.when(s == last)
    def _kv_out():
        kp_ref[0] = kk[tm - WIN:tm]
        vp_ref[0] = vv[tm - WIN:tm]

    gm = _dot(xb, wgm_ref[...])
    ga = _dot(xb, wga_ref[...])
    mixed = _sigmoid(gm) * ym_s[...] + _sigmoid(ga) * ya_s[...]
    r = _dot(mixed.astype(BF16), wout_ref[...])
    x1_ref[0] = _layer_norm(ALPHA * xn + r, l1g_ref[...], l1b_ref[...])


def _prompt_mixer(x, wts, *, tm=256, lc=128):
    B, S, _ = x.shape
    assert S % tm == 0 and tm % lc == 0 and tm % WIN == 0 and tm >= WIN
    grid = (B, S // tm)
    cs = functools.partial(_const_spec, grid_rank=2)
    in_specs = [
        pl.BlockSpec((1, tm, D), lambda b, s: (b, s, 0)),
        cs((1, D)), cs((1, D)),
        cs((D, QK_CH)), cs((D, D)), cs((D, LANES)), cs((SUBLANES, D)), cs((1, LANES)), cs((SUBLANES, 1)),
        cs((D, D)), cs((D, D)), cs((D, 2 * KVH * HD)), cs((D, D)), cs((D, D)),
        cs((CONV_W, QK_CH)), cs((1, QK_CH)), cs((1, D)),
        pl.BlockSpec(memory_space=pltpu.SMEM),
        cs((D, D)), cs((1, D)), cs((1, D)),
    ]
    out_shape = (
        jax.ShapeDtypeStruct((B, S, D), F32),
        jax.ShapeDtypeStruct((B, MH, DK, DV), F32),
        jax.ShapeDtypeStruct((B, MH, DK), F32),
        jax.ShapeDtypeStruct((B, SUBLANES, LANES), F32),
        jax.ShapeDtypeStruct((B, CONV_W - 1, QK_CH), F32),
        jax.ShapeDtypeStruct((B, WIN, KVH * HD), F32),
        jax.ShapeDtypeStruct((B, WIN, KVH * HD), F32),
    )
    out_specs = (
        pl.BlockSpec((1, tm, D), lambda b, s: (b, s, 0)),
        pl.BlockSpec((1, MH, DK, DV), lambda b, s: (b, 0, 0, 0)),
        pl.BlockSpec((1, MH, DK), lambda b, s: (b, 0, 0)),
        pl.BlockSpec((1, SUBLANES, LANES), lambda b, s: (b, 0, 0)),
        pl.BlockSpec((1, CONV_W - 1, QK_CH), lambda b, s: (b, 0, 0)),
        pl.BlockSpec((1, WIN, KVH * HD), lambda b, s: (b, 0, 0)),
        pl.BlockSpec((1, WIN, KVH * HD), lambda b, s: (b, 0, 0)),
    )
    scratch = [
        pltpu.VMEM((tm + SUBLANES, QK_CH), F32),
        pltpu.VMEM((2 * KVH, WIN, LANES), BF16),
        pltpu.VMEM((tm, D), F32),
        pltpu.VMEM((tm, D), F32),
    ]
    return pl.pallas_call(
        functools.partial(_mixer_kernel, tm=tm, lc=lc),
        grid=grid, in_specs=in_specs, out_specs=out_specs, out_shape=out_shape,
        scratch_shapes=scratch, name="prompt_mixer",
        compiler_params=pltpu.CompilerParams(
            dimension_semantics=("arbitrary", "arbitrary"), vmem_limit_bytes=56 * 1024 * 1024),
    )(x, wts["ln_in_g"], wts["ln_in_b"], wts["w_qk"], wts["w_mv"], wts["w_if"], wts["w_ift"],
      wts["b_if"], wts["b_ift"], wts["w_mo"], wts["w_aq"], wts["w_akv"], wts["w_gm"], wts["w_ga"],
      wts["conv_w"], wts["conv_b"], wts["m_norm_g"], wts["sinks"], wts["w_out"], wts["ln1_g"], wts["ln1_b"])


_PROJ_NAMES = ("xn", "zqk", "vm", "smo", "aq", "k2", "v2", "kvl", "sgm", "sga", "gcol", "grow", "brow")


def _mixer_project(x, slot, w, buf, *, tm, lc):
    xn = _layer_norm(x, w["lng"][...], w["lnb"][...])
    xb = xn.astype(BF16)
    buf["xn"][slot] = xn
    buf["zqk"][slot] = _dot(xb, w["wqk"][...])
    buf["vm"][slot] = _dot(xb, w["wmv"][...]).astype(BF16)
    buf["smo"][slot] = _sigmoid(_dot(xb, w["wmo"][...])) * w["mng"][...]
    aq = _dot(xb, w["waq"][...]) * (HD ** -0.5 * LOG2E)
    lo_d =(lax.broadcasted_iota(jnp.int32, (tm, D), 1) & (LANES - 1)) < HD
    aqm = (jnp.where(lo_d, aq, 0.0).astype(BF16), jnp.where(lo_d, 0.0, aq).astype(BF16))
    for j in range(tm // WIN):
        for h in range(AH):
            buf["aq"][slot, j, h // GROUP, (h % GROUP) * WIN:(h % GROUP + 1) * WIN, :] = (
                aqm[h % 2][j * WIN:(j + 1) * WIN, (h // 2) * LANES:(h // 2 + 1) * LANES])
    akv = _dot(xb, w["wakv"][...])
    lo = lax.broadcasted_iota(jnp.int32, (tm, LANES), 1) < HD
    k2 = _dup_halves(akv[:, 0:LANES], lo)
    v2 = _dup_halves(akv[:, LANES:2 * LANES], lo)
    for g in range(KVH):
        buf["k2"][slot, g] = k2[g].astype(BF16)
        buf["v2"][slot, g] = v2[g].astype(BF16)
    buf["kvl"][slot] = akv[tm - WIN:tm, :]
    buf["sgm"][slot] = _sigmoid(_dot(xb, w["wgm"][...]))
    buf["sga"][slot] = _sigmoid(_dot(xb, w["wga"][...]))
    grow = _dot_nt(w["wift"][...], xb) + w["bift"][...]
    buf["grow"][slot] = grow
    lsrow = _log_sigmoid(grow)
    lane = lax.broadcasted_iota(jnp.int32, (SUBLANES, lc), 1)
    pad = jnp.zeros((lc - 2 * SUBLANES, lc), F32)
    for c in range(tm // lc):
        r0 = c * lc
        seg = lsrow[:, r0:r0 + lc]
        sh = 1
        while sh < lc:
            seg = seg + jnp.where(lane >= sh, pltpu.roll(seg, sh, 1), 0.0)
            sh *= 2
        buf["brow"][slot, :, r0:r0 + lc] = seg
        buf["gcol"][slot, r0:r0 + lc, :] = jnp.concatenate([grow[:, r0:r0 + lc], seg, pad], axis=0).T


def _mixer_consume(slot, first, w, buf, st, x1_ref, row0, *, tm, lc):
    c_ref, n_ref, m_ref, ctail, kvc, ym_s, ya_s = st
    z = buf["zqk"][slot]
    tail = ctail[...]
    ctail[...] = z[tm - SUBLANES:tm, :]
    cw = w["cw"][...]
    rows8 = lax.broadcasted_iota(jnp.int32, (SUBLANES, QK_CH), 0)
    y = cw[CONV_W - 1:CONV_W] * z + w["cb"][...]
    for sh in range(1, CONV_W):
        zr = pltpu.roll(z, sh, 0)
        head = jnp.where(rows8 < sh, pltpu.roll(tail, sh, 0), zr[0:SUBLANES])
        zr = jnp.concatenate([head, zr[SUBLANES:]], axis=0)
        y = y + cw[CONV_W - 1 - sh:CONV_W - sh] * zr
    qk = y * _sigmoid(y)

    ri = lax.broadcasted_iota(jnp.int32, (lc, lc), 0)
    ci = lax.broadcasted_iota(jnp.int32, (lc, lc), 1)
    causal = ci <= ri
    for c in range(tm // lc):
        r0 = c * lc
        for h in range(MH):
            qh = qk[r0:r0 + lc, h * DK:(h + 1) * DK]
            kh = qk[r0:r0 + lc, MH * DK + h * DK:MH * DK + (h + 1) * DK] * (DK ** -0.5)
            vh = buf["vm"][slot, r0:r0 + lc, h * DV:(h + 1) * DV]
            qb = qh.astype(BF16)
            bc = buf["gcol"][slot, r0:r0 + lc, SUBLANES + MH + h:SUBLANES + MH + h + 1]
            br = buf["brow"][slot, MH + h:MH + h + 1, r0:r0 + lc]
            li_c = buf["gcol"][slot, r0:r0 + lc, h:h + 1]
            li_r = buf["grow"][slot, h:h + 1, r0:r0 + lc]
            m_prev = m_ref[0, h:h + 1, 0:1]
            c_prev = c_ref[0, h]
            n_prev = n_ref[0, h:h + 1, :]

            dmat = jnp.where(causal, bc - br + li_r, NEG_INF)
            m_inter = bc + m_prev
            m_t = jnp.maximum(m_inter, jnp.max(dmat, -1, keepdims=True))
            wgt = jnp.exp(dmat - m_t) * _dot_nt(qb, kh.astype(BF16))
            inter = jnp.exp(m_inter - m_t)
            num = _dot(wgt.astype(BF16), vh) + inter * _dot(qb, c_prev.astype(BF16))
            den = jnp.sum(wgt, -1, keepdims=True) + inter * jnp.sum(qh * n_prev, -1, keepdims=True)
            denom = jnp.maximum(jnp.abs(den), jnp.exp(-m_t))
            hh = num / denom
            hn = hh * lax.rsqrt(jnp.mean(hh * hh, -1, keepdims=True) + RMS_EPS)
            ym_s[r0:r0 + lc, h * DV:(h + 1) * DV] = buf["smo"][slot, r0:r0 + lc, h * DV:(h + 1) * DV] * hn

            m_new = m_t[lc - 1:lc, :]
            bc_last = bc[lc - 1:lc, :]
            decay = jnp.exp(bc_last + m_prev - m_new)
            ks = kh * jnp.exp(bc_last - bc + li_c - m_new)
            c_ref[0, h] = decay * c_prev + _dot_tn(ks.astype(BF16), vh)
            n_ref[0, h:h + 1, :] = decay * n_prev + jnp.sum(ks, 0, keepdims=True)
            m_ref[0, h:h + 1, :] = jnp.broadcast_to(m_new, (1, LANES))

    lo_q = lax.broadcasted_iota(jnp.int32, (WIN, LANES), 1) < HD
    qi = lax.broadcasted_iota(jnp.int32, (WIN, 2 * WIN), 0)
    kj = lax.broadcasted_iota(jnp.int32, (WIN, 2 * WIN), 1)
    band = (kj >= qi) & (kj <= qi + WIN)
    ones_v = jnp.ones((2 * WIN, LANES), BF16)
    for j in range(tm // WIN):
        r0 = j * WIN
        if j == 0 and first is not None:
            mask = band & (kj >= jnp.where(first, WIN, 0))
        else:
            mask = band
        for g in range(KVH):
            if j == 0:
                kprev, vprev = kvc[g], kvc[KVH + g]
            else:
                kprev, vprev = buf["k2"][slot, g, r0 - WIN:r0, :], buf["v2"][slot, g, r0 - WIN:r0, :]
            kcat = jnp.concatenate([kprev, buf["k2"][slot, g, r0:r0 + WIN, :]], axis=0)
            vcat = jnp.concatenate([vprev, buf["v2"][slot, g, r0:r0 + WIN, :]], axis=0)
            sc_all = _dot_nt(buf["aq"][slot, j, g], kcat)
            es, ds = [], []
            for hh in range(GROUP):
                sink = w["sink"][g * GROUP + hh] * LOG2E
                sc = jnp.where(mask, sc_all[hh * WIN:(hh + 1) * WIN, :], NEG_INF)
                mx = jnp.maximum(jnp.max(sc, -1, keepdims=True), sink)
                es.append(jnp.exp2(sc - mx).astype(BF16))
                ds.append(jnp.exp2(sink - mx))
            o_all = _dot(jnp.concatenate(es, axis=0), jnp.concatenate([vcat, ones_v], axis=1))
            for pp in range(GROUP // 2):
                pj = g * GROUP // 2 + pp
                halves = []
                for hh in (2 * pp, 2 * pp + 1):
                    oh = o_all[hh * WIN:(hh + 1) * WIN, :]
                    halves.append(oh[:, 0:LANES] / (oh[:, LANES:2 * LANES] + ds[hh]))
                ya_s[r0:r0 + WIN, pj * LANES:(pj + 1) * LANES] = jnp.where(lo_q, halves[0], halves[1])
    for g in range(KVH):
        kvc[g] = buf["k2"][slot, g, tm - WIN:tm, :]
        kvc[KVH + g] = buf["v2"][slot, g, tm - WIN:tm, :]

    mixed = buf["sgm"][slot] * ym_s[...] + buf["sga"][slot] * ya_s[...]
    r = _dot(mixed.astype(BF16), w["wout"][...])
    x1_ref[0, row0:row0 + tm, :] = _layer_norm(ALPHA * buf["xn"][slot] + r, w["l1g"][...], w["l1b"][...])


_MIXER_W_NAMES = ("lng", "lnb", "wqk", "wmv", "wift", "bift", "wmo", "waq", "wakv", "wgm", "wga",
                  "cw", "cb", "mng", "sink", "wout", "l1g", "l1b")


def _mixer2_kernel(*refs, tm, lc, npb):
    nw = len(_MIXER_W_NAMES)
    x0_ref, xa_ref, xb_ref = refs[0:3]
    w = dict(zip(_MIXER_W_NAMES, refs[3:3 + nw]))
    x1_ref, c_ref, n_ref, m_ref, conv_ref, kp_ref, vp_ref = refs[3 + nw:10 + nw]
    scratch = refs[10 + nw:]
    buf = dict(zip(_PROJ_NAMES, scratch[:len(_PROJ_NAMES)]))
    ctail, kvc, ym_s, ya_s = scratch[len(_PROJ_NAMES):]
    st = (c_ref, n_ref, m_ref, ctail, kvc, ym_s, ya_s)
    k = pl.program_id(0)
    first = (2 * k) % npb == 0

    @pl.when(k == 0)
    def _prologue():
        _mixer_project(x0_ref[0], 0, w, buf, tm=tm, lc=lc)

    @pl.when(first)
    def _init():
        c_ref[...] = jnp.zeros_like(c_ref)
        n_ref[...] = jnp.zeros_like(n_ref)
        m_ref[...] = jnp.zeros_like(m_ref)
        ctail[...] = jnp.zeros_like(ctail)
        kvc[...] = jnp.zeros_like(kvc)

    _mixer_project(xa_ref[0], 1, w, buf, tm=tm, lc=lc)
    _mixer_consume(0, first, w, buf, st, x1_ref, 0, tm=tm, lc=lc)
    _mixer_project(xb_ref[0], 0, w, buf, tm=tm, lc=lc)
    _mixer_consume(1, None, w, buf, st, x1_ref, tm, tm=tm, lc=lc)

    @pl.when((2 * k + 1) % npb == npb - 1)
    def _state_out():
        conv_ref[0] = ctail[SUBLANES - (CONV_W - 1):SUBLANES, :]
        kp_ref[0] = buf["kvl"][1, :, 0:LANES]
        vp_ref[0] = buf["kvl"][1, :, LANES:2 * LANES]


def _prompt_mixer2(x, wts, *, tm=256, lc=128):
    B, S, _ = x.shape
    npb = S // tm
    assert S % tm == 0 and npb % 2 == 0 and tm % lc == 0 and tm % WIN == 0
    nt = B * npb
    hpb = npb // 2
    cs = functools.partial(_const_spec, grid_rank=1)

    def tile_map(off):
        def imap(k):
            t = jnp.minimum(2 * k + off, nt - 1)
            return (t // npb, t % npb, 0)
        return imap

    in_specs = [
        pl.BlockSpec((1, tm, D), lambda k: (0, 0, 0), pipeline_mode=pl.Buffered(1)),
        pl.BlockSpec((1, tm, D), tile_map(1)),
        pl.BlockSpec((1, tm, D), tile_map(2)),
        cs((1, D)), cs((1, D)),
        cs((D, QK_CH)), cs((D, D)), cs((SUBLANES, D)), cs((SUBLANES, 1)),
        cs((D, D)), cs((D, D)), cs((D, 2 * KVH * HD)), cs((D, D)), cs((D, D)),
        cs((CONV_W, QK_CH)), cs((1, QK_CH)), cs((1, D)),
        pl.BlockSpec(memory_space=pltpu.SMEM),
        cs((D, D)), cs((1, D)), cs((1, D)),
    ]
    out_shape = (
        jax.ShapeDtypeStruct((B, S, D), F32),
        jax.ShapeDtypeStruct((B, MH, DK, DV), F32),
        jax.ShapeDtypeStruct((B, MH, DK), F32),
        jax.ShapeDtypeStruct((B, SUBLANES, LANES), F32),
        jax.ShapeDtypeStruct((B, CONV_W - 1, QK_CH), F32),
        jax.ShapeDtypeStruct((B, WIN, KVH * HD), F32),
        jax.ShapeDtypeStruct((B, WIN, KVH * HD), F32),
    )
    out_specs = (
        pl.BlockSpec((1, 2 * tm, D), lambda k: (k // hpb, k % hpb, 0)),
        pl.BlockSpec((1, MH, DK, DV), lambda k: (k // hpb, 0, 0, 0)),
        pl.BlockSpec((1, MH, DK), lambda k: (k // hpb, 0, 0)),
        pl.BlockSpec((1, SUBLANES, LANES), lambda k: (k // hpb, 0, 0)),
        pl.BlockSpec((1, CONV_W - 1, QK_CH), lambda k: (k // hpb, 0, 0)),
        pl.BlockSpec((1, WIN, KVH * HD), lambda k: (k // hpb, 0, 0)),
        pl.BlockSpec((1, WIN, KVH * HD), lambda k: (k // hpb, 0, 0)),
    )
    proj_scratch = {
        "xn": pltpu.VMEM((2, tm, D), F32), "zqk": pltpu.VMEM((2, tm, QK_CH), F32),
        "vm": pltpu.VMEM((2, tm, D), BF16), "smo": pltpu.VMEM((2, tm, D), F32),
        "aq": pltpu.VMEM((2, tm // WIN, KVH, GROUP * WIN, LANES), BF16),
        "k2": pltpu.VMEM((2, KVH, tm, LANES), BF16), "v2": pltpu.VMEM((2, KVH, tm, LANES), BF16),
        "kvl": pltpu.VMEM((2, WIN, 2 * LANES), F32),
        "sgm": pltpu.VMEM((2, tm, D), F32), "sga": pltpu.VMEM((2, tm, D), F32),
        "gcol": pltpu.VMEM((2, tm, LANES), F32), "grow": pltpu.VMEM((2, SUBLANES, tm), F32),
        "brow": pltpu.VMEM((2, SUBLANES, tm), F32),
    }
    scratch = [proj_scratch[n] for n in _PROJ_NAMES] + [
        pltpu.VMEM((SUBLANES, QK_CH), F32),
        pltpu.VMEM((2 * KVH, WIN, LANES), BF16),
        pltpu.VMEM((tm, D), F32),
        pltpu.VMEM((tm, D), F32),
    ]
    return pl.pallas_call(
        functools.partial(_mixer2_kernel, tm=tm, lc=lc, npb=npb),
        grid=(nt // 2,), in_specs=in_specs, out_specs=out_specs, out_shape=out_shape,
        scratch_shapes=scratch, name="prompt_mixer",
        compiler_params=pltpu.CompilerParams(
            dimension_semantics=("arbitrary",), vmem_limit_bytes=56 * 1024 * 1024),
    )(x, x, x, wts["ln_in_g"], wts["ln_in_b"], wts["w_qk"], wts["w_mv"], wts["w_ift"],
      wts["b_ift"], wts["w_mo"], wts["w_aq"], wts["w_akv"], wts["w_gm"], wts["w_ga"],
      wts["conv_w"], wts["conv_b"], wts["m_norm_g"], wts["sinks"], wts["w_out"], wts["ln1_g"], wts["ln1_b"])


MXU_DIM = 256
FFN_CHUNKS = ((0, 6 * MXU_DIM), (6 * MXU_DIM, DFF))


def _ffn_kernel(x_ref, p_ref, wgu_ref, wd_ref, wpg_ref, wp_ref, g_ref, b_ref, o_ref, *, sub):
    for r0 in range(0, x_ref.shape[0], sub):
        x = x_ref[r0:r0 + sub, :]
        xb = x.astype(BF16)
        acc = ALPHA * x + (_sigmoid(_dot(xb, wpg_ref[...]))
                           * _dot(p_ref[r0:r0 + sub, :].astype(BF16), wp_ref[...]))
        for c0, c1 in FFN_CHUNKS:
            g = _dot(xb, wgu_ref[:, c0:c1])
            u = _dot(xb, wgu_ref[:, DFF + c0:DFF + c1])
            hcat = (g * _sigmoid(g) * u).astype(BF16)
            acc = acc + _dot(hcat, wd_ref[c0:c1, :])
        o_ref[r0:r0 + sub, :] = _layer_norm(acc, g_ref[...], b_ref[...])


def _ffn(x, p, wts, *, tm, sub):
    N = x.shape[0]
    assert N % tm == 0 and tm % sub == 0
    cs = functools.partial(_const_spec, grid_rank=1)
    return pl.pallas_call(
        functools.partial(_ffn_kernel, sub=sub),
        grid=(N // tm,),
        in_specs=[pl.BlockSpec((tm, D), lambda i: (i, 0)), pl.BlockSpec((tm, PD), lambda i: (i, 0)),
                  cs((D, 2 * DFF)), cs((DFF, D)), cs((D, D)), cs((PD, D)), cs((1, D)), cs((1, D))],
        out_specs=pl.BlockSpec((tm, D), lambda i: (i, 0)),
        out_shape=jax.ShapeDtypeStruct((N, D), F32),
        name="ffn",
        compiler_params=pltpu.CompilerParams(
            dimension_semantics=("arbitrary",), vmem_limit_bytes=56 * 1024 * 1024),
    )(x, p, wts["w_gu"], wts["w_d"], wts["w_pg"], wts["w_p"], wts["ln2_g"], wts["ln2_b"])


Z_V = 0
Z_MO = Z_V + D
Z_GM = Z_MO + D
Z_GA = Z_GM + D
Z_XN = Z_GA + D
Z_Q = Z_XN + D
Z_KW = Z_Q + MH * DK
Z_SC = Z_KW + MH * DK
Z_KN = Z_SC + LANES
Z_VN = Z_KN + LANES
Z_W = Z_VN + LANES


def _sample_pre_kernel(x_ref, lng_ref, lnb_ref, wqk_ref, wmv_ref, wif_ref, bif_ref, wmo_ref, waq_ref,
                       wakv_ref, wgm_ref, wga_ref, cw_ref, cb_ref, cs_ref, n_ref, m_ref,
                       z_ref, qs_ref, convo_ref, no_ref, mo_ref, *, nb, tb):
    xn = _layer_norm(x_ref[...], lng_ref[...], lnb_ref[...])
    xb = xn.astype(BF16)
    zqk = _dot(xb, wqk_ref[...])
    cw = cw_ref[...]
    y = (cw[3:4] * zqk + cw[2:3] * cs_ref[:, 2 * QK_CH:3 * QK_CH] + cw[1:2] * cs_ref[:, QK_CH:2 * QK_CH]
         + cw[0:1] * cs_ref[:, 0:QK_CH] + cb_ref[...])
    qk = y * _sigmoid(y)
    convo_ref[:, 0:2 * QK_CH] = cs_ref[:, QK_CH:3 * QK_CH]
    convo_ref[:, 2 * QK_CH:3 * QK_CH] = zqk

    gcol = _dot(xb, wif_ref[...]) + bif_ref[...]
    ig = gcol[:, 0:MH]
    m_inter = _log_sigmoid(gcol[:, MH:2 * MH]) + m_ref[...]
    m_t = jnp.maximum(m_inter, ig)
    inter = jnp.exp(m_inter - m_t)
    wi = jnp.exp(ig - m_t)
    enm = jnp.exp(-m_t)
    mo_ref[...] = m_t

    z_ref[:, Z_V:Z_V + D] = _dot(xb, wmv_ref[...])
    z_ref[:, Z_MO:Z_MO + D] = _dot(xb, wmo_ref[...])
    z_ref[:, Z_GM:Z_GM + D] = _dot(xb, wgm_ref[...])
    z_ref[:, Z_GA:Z_GA + D] = _dot(xb, wga_ref[...])
    z_ref[:, Z_XN:Z_XN + D] = xn
    z_ref[:, Z_Q:Z_Q + MH * DK] = qk[:, 0:MH * DK]
    sc = jnp.zeros((nb, LANES), F32)
    lane = lax.broadcasted_iota(jnp.int32, (nb, LANES), 1)
    for h in range(MH):
        qh = qk[:, h * DK:(h + 1) * DK]
        kh = qk[:, MH * DK + h * DK:MH * DK + (h + 1) * DK] * (DK ** -0.5)
        nh = n_ref[:, h * DK:(h + 1) * DK]
        wv = wi[:, h:h + 1] * jnp.sum(qh * kh, -1, keepdims=True)
        den = wv + inter[:, h:h + 1] * jnp.sum(qh * nh, -1, keepdims=True)
        denom = jnp.maximum(jnp.abs(den), enm[:, h:h + 1])
        kw = wi[:, h:h + 1] * kh
        z_ref[:, Z_KW + h * DK:Z_KW + (h + 1) * DK] = kw
        no_ref[:, h * DK:(h + 1) * DK] = inter[:, h:h + 1] * nh + kw
        sc = jnp.where(lane == h, inter[:, h:h + 1], sc)
        sc = jnp.where(lane == MH + h, wv, sc)
        sc = jnp.where(lane == 2 * MH + h, denom, sc)
    z_ref[:, Z_SC:Z_SC + LANES] = sc

    akv = _dot(xb, wakv_ref[...])
    z_ref[:, Z_KN:Z_KN + LANES] = akv[:, 0:LANES]
    z_ref[:, Z_VN:Z_VN + LANES] = akv[:, LANES:2 * LANES]
    aq = _dot(xb, waq_ref[...]) * (HD ** -0.5)
    lo = lax.broadcasted_iota(jnp.int32, (nb, LANES), 1) < HD
    for h in range(AH):
        pair = aq[:, (h // 2) * LANES:(h // 2 + 1) * LANES]
        qm = jnp.where(lo if h % 2 == 0 else ~lo, pair, 0.0)
        for t in range(nb // tb):
            qs_ref[(t * AH + h) * tb:(t * AH + h + 1) * tb, :] = qm[t * tb:(t + 1) * tb, :]


def _sample_state_kernel(z_ref, qs_ref, sink_ref, c_ref, ck_ref, cv_ref,
                         co_ref, qc_ref, os_ref, ko_ref, vo_ref, *, tb):
    lo = lax.broadcasted_iota(jnp.int32, (WIN, LANES), 1) < HD
    row = lax.broadcasted_iota(jnp.int32, (WIN, LANES), 0)
    rowt = lax.broadcasted_iota(jnp.int32, (tb, DK), 0)
    sinks = sink_ref[...]
    q_all = z_ref[:, Z_Q:Z_Q + MH * DK]
    kw_all = z_ref[:, Z_KW:Z_KW + MH * DK]
    v_all = z_ref[:, Z_V:Z_V + D]
    sc = z_ref[:, Z_SC:Z_SC + LANES]
    kn_all = z_ref[:, Z_KN:Z_KN + LANES]
    vn_all = z_ref[:, Z_VN:Z_VN + LANES]
    kn2_all = _dup_halves(kn_all, lo[0:tb])
    vn2_all = _dup_halves(vn_all, lo[0:tb])
    for i in range(tb):
        for h in range(MH):
            cb = c_ref[i, h]
            qc = _dot(q_all[:, h * DK:(h + 1) * DK].astype(BF16), cb.astype(BF16))
            qc_ref[i:i + 1, h * DV:(h + 1) * DV] = qc[i:i + 1]
            kw_i = jnp.where(rowt == i, kw_all[:, h * DK:(h + 1) * DK], 0.0)
            outer = _dot_tn(kw_i, v_all[:, h * DV:(h + 1) * DV])
            co_ref[i, h] = sc[i:i + 1, h:h + 1] * cb + outer
        kc = ck_ref[i]
        vc = cv_ref[i]
        kn = kn_all[i:i + 1]
        vn = vn_all[i:i + 1]
        k2 = _dup_halves(kc, lo)
        v2 = _dup_halves(vc, lo)
        kn2 = [a[i:i + 1] for a in kn2_all]
        vn2 = [a[i:i + 1] for a in vn2_all]
        qrow = qs_ref[pl.ds(i, AH, stride=tb), :]
        outs = []
        for g in range(KVH):
            qg = qrow[g * GROUP:(g + 1) * GROUP]
            s = _dot_nt(qg.astype(BF16), k2[g].astype(BF16))
            s_self = jnp.sum(qg * kn2[g], -1, keepdims=True)
            sg = sinks[g * GROUP:(g + 1) * GROUP]
            mx = jnp.maximum(jnp.maximum(jnp.max(s, -1, keepdims=True), s_self), sg)
            e = jnp.exp(s - mx)
            es = jnp.exp(s_self - mx)
            dsum = jnp.sum(e, -1, keepdims=True) + es + jnp.exp(sg - mx)
            outs.append((_dot(e.astype(BF16), v2[g].astype(BF16)) + es * vn2[g]) / dsum)
        os_ref[pl.ds(i, AH, stride=tb), :] = jnp.concatenate(outs, axis=0)
        ko_ref[i] = jnp.where(row == WIN - 1, kn, pltpu.roll(kc, WIN - 1, 0))
        vo_ref[i] = jnp.where(row == WIN - 1, vn, pltpu.roll(vc, WIN - 1, 0))


def _sample_post_kernel(z_ref, qc_ref, os_ref, mng_ref, wout_ref, l1g_ref, l1b_ref, x1_ref, *, nb, tb):
    sc = z_ref[:, Z_SC:Z_SC + LANES]
    mng = mng_ref[...]
    lo = lax.broadcasted_iota(jnp.int32, (nb, LANES), 1) < HD
    ym = []
    for h in range(MH):
        v = z_ref[:, Z_V + h * DV:Z_V + (h + 1) * DV]
        num = sc[:, MH + h:MH + h + 1] * v + sc[:, h:h + 1] * qc_ref[:, h * DV:(h + 1) * DV]
        hh = num / sc[:, 2 * MH + h:2 * MH + h + 1]
        hn = hh * lax.rsqrt(jnp.mean(hh * hh, -1, keepdims=True) + RMS_EPS)
        ym.append(_sigmoid(z_ref[:, Z_MO + h * DV:Z_MO + (h + 1) * DV]) * hn * mng[:, h * DV:(h + 1) * DV])
    ym = jnp.concatenate(ym, axis=1)
    ya = []
    for pj in range(AH // 2):
        halves = []
        for par in range(2):
            h = 2 * pj + par
            halves.append(jnp.concatenate(
                [os_ref[(t * AH + h) * tb:(t * AH + h + 1) * tb, :] for t in range(nb // tb)], axis=0))
        ya.append(jnp.where(lo, halves[0], halves[1]))
    ya = jnp.concatenate(ya, axis=1)
    mixed = _sigmoid(z_ref[:, Z_GM:Z_GM + D]) * ym + _sigmoid(z_ref[:, Z_GA:Z_GA + D]) * ya
    r = _dot(mixed.astype(BF16), wout_ref[...])
    x1_ref[...] = _layer_norm(ALPHA * z_ref[:, Z_XN:Z_XN + D] + r, l1g_ref[...], l1b_ref[...])


def _sample_mixer(x, c0, n0, m0, conv0, k0, v0, wts, *, tb=8):
    nb = x.shape[0]
    assert nb % tb == 0
    vmem = pltpu.CompilerParams(dimension_semantics=("arbitrary",), vmem_limit_bytes=56 * 1024 * 1024)
    full = lambda shape: pl.BlockSpec(shape, lambda i: (0,) * len(shape))
    pre_in = [x, wts["ln_in_g"], wts["ln_in_b"], wts["w_qk"], wts["w_mv"], wts["w_if"], wts["b_if"],
              wts["w_mo"], wts["w_aq"], wts["w_akv"], wts["w_gm"], wts["w_ga"], wts["conv_w"], wts["conv_b"],
              conv0, n0, m0]
    z, qs, conv_new, n_new, m_new = pl.pallas_call(
        functools.partial(_sample_pre_kernel, nb=nb, tb=tb),
        grid=(1,),
        in_specs=[full(a.shape) for a in pre_in],
        out_specs=(full((nb, Z_W)), full((nb * AH, LANES)), full((nb, (CONV_W - 1) * QK_CH)),
                   full((nb, MH * DK)), full((nb, MH))),
        out_shape=(jax.ShapeDtypeStruct((nb, Z_W), F32), jax.ShapeDtypeStruct((nb * AH, LANES), F32),
                   jax.ShapeDtypeStruct((nb, (CONV_W - 1) * QK_CH), F32),
                   jax.ShapeDtypeStruct((nb, MH * DK), F32), jax.ShapeDtypeStruct((nb, MH), F32)),
        name="sample_pre", compiler_params=vmem,
    )(*pre_in)

    c_new, qc, os_, k_new, v_new = pl.pallas_call(
        functools.partial(_sample_state_kernel, tb=tb),
        grid=(nb // tb,),
        in_specs=[pl.BlockSpec((tb, Z_W), lambda i: (i, 0)),
                  pl.BlockSpec((tb * AH, LANES), lambda i: (i, 0)),
                  pl.BlockSpec((AH, 1), lambda i: (0, 0)),
                  pl.BlockSpec((tb, MH, DK, DV), lambda i: (i, 0, 0, 0)),
                  pl.BlockSpec((tb, WIN, LANES), lambda i: (i, 0, 0)),
                  pl.BlockSpec((tb, WIN, LANES), lambda i: (i, 0, 0))],
        out_specs=(pl.BlockSpec((tb, MH, DK, DV), lambda i: (i, 0, 0, 0)),
                   pl.BlockSpec((tb, D), lambda i: (i, 0)),
                   pl.BlockSpec((tb * AH, LANES), lambda i: (i, 0)),
                   pl.BlockSpec((tb, WIN, LANES), lambda i: (i, 0, 0)),
                   pl.BlockSpec((tb, WIN, LANES), lambda i: (i, 0, 0))),
        out_shape=(jax.ShapeDtypeStruct((nb, MH, DK, DV), F32), jax.ShapeDtypeStruct((nb, D), F32),
                   jax.ShapeDtypeStruct((nb * AH, LANES), F32),
                   jax.ShapeDtypeStruct((nb, WIN, LANES), F32), jax.ShapeDtypeStruct((nb, WIN, LANES), F32)),
        name="sample_state", compiler_params=vmem,
    )(z, qs, wts["sinks_col"], c0, k0, v0)

    post_in = [z, qc, os_, wts["m_norm_g"], wts["w_out"], wts["ln1_g"], wts["ln1_b"]]
    x1 = pl.pallas_call(
        functools.partial(_sample_post_kernel, nb=nb, tb=tb),
        grid=(1,),
        in_specs=[full(a.shape) for a in post_in],
        out_specs=full((nb, D)),
        out_shape=jax.ShapeDtypeStruct((nb, D), F32),
        name="sample_post", compiler_params=vmem,
    )(*post_in)
    return x1, c_new, n_new, m_new, conv_new, k_new, v_new


def _prep_weights(ln_in_g, ln_in_b, w_in, b_igate, b_fgate, conv_w, conv_b, m_norm_g, attn_sinks, w_out,
                  ln1_g, ln1_b, w_gate_up, w_down, ln2_g, ln2_b, w_ple, w_ple_gate):
    w = w_in[0]
    row = lambda a: a.reshape(1, -1).astype(F32)
    w_if = jnp.pad(w[:, O_MI:O_MO], ((0, 0), (0, LANES - 2 * MH)))
    b_if = jnp.pad(jnp.concatenate([b_igate[0], b_fgate[0]]), (0, LANES - 2 * MH))
    return {
        "ln_in_g": row(ln_in_g), "ln_in_b": row(ln_in_b),
        "w_qk": w[:, O_QK:O_MV].astype(BF16), "w_mv": w[:, O_MV:O_MI].astype(BF16),
        "w_if": w_if.astype(BF16), "w_ift": w[:, O_MI:O_MO].T.astype(BF16),
        "b_if": b_if.reshape(1, LANES), "b_ift": b_if[:2 * MH].reshape(2 * MH, 1),
        "w_mo": w[:, O_MO:O_AQ].astype(BF16), "w_aq": w[:, O_AQ:O_AK].astype(BF16),
        "w_akv": w[:, O_AK:O_GM].astype(BF16), "w_gm": w[:, O_GM:O_GA].astype(BF16),
        "w_ga": w[:, O_GA:IN_WIDTH].astype(BF16),
        "conv_w": conv_w[0], "conv_b": row(conv_b[0]), "m_norm_g": row(m_norm_g[0]),
        "sinks": attn_sinks[0], "sinks_col": attn_sinks[0].reshape(AH, 1),
        "w_out": w_out[0].astype(BF16), "ln1_g": row(ln1_g[0]), "ln1_b": row(ln1_b[0]),
        "w_gu": w_gate_up[0].astype(BF16),
        "w_d": w_down[0].astype(BF16), "w_pg": w_ple_gate[0].astype(BF16), "w_p": w_ple[0].astype(BF16),
        "ln2_g": row(ln2_g[0]), "ln2_b": row(ln2_b[0]),
    }


def _prompt_path(x, p, wts, *, tm=256, lc=128, tmf=1024):
    B, S, _ = x.shape
    x1, c, n, m, conv, k, v = _prompt_mixer2(x, wts, tm=tm, lc=lc)
    y = _ffn(x1.reshape(B * S, D), p.reshape(B * S, PD), wts, tm=tmf, sub=tmf // 2).reshape(B, S, D)
    return (y, c[None], n[None], m[None, :, :MH, 0], conv[None],
            k.reshape(1, B, WIN, KVH, HD), v.reshape(1, B, WIN, KVH, HD))


def _sample_path(x, p, c0, n0, m0, conv0, k0, v0, wts, *, tb=8):
    nb = x.shape[0]
    x1, c, n, m, conv, k, v = _sample_mixer(
        x.reshape(nb, D), c0[0], n0[0].reshape(nb, MH * DK), m0[0],
        conv0[0].reshape(nb, (CONV_W - 1) * QK_CH), k0[0].reshape(nb, WIN, LANES),
        v0[0].reshape(nb, WIN, LANES), wts, tb=tb)
    y = _ffn(x1, p.reshape(nb, PD), wts, tm=nb, sub=nb).reshape(nb, 1, D)
    return (y, c[None], n.reshape(1, nb, MH, DK), m[None], conv.reshape(1, nb, CONV_W - 1, QK_CH),
            k.reshape(1, nb, WIN, KVH, HD), v.reshape(1, nb, WIN, KVH, HD))


def kernel(x_prompt, x_sample, state_mlstm_C, state_mlstm_n, state_mlstm_m, state_conv, cache_win_k, cache_win_v, p_prompt, p_sample, ln_in_g, ln_in_b, w_in, b_igate, b_fgate, conv_w, conv_b, m_norm_g, attn_sinks, w_out, ln1_g, ln1_b, w_gate_up, w_down, ln2_g, ln2_b, w_ple, w_ple_gate):
    wts = _prep_weights(ln_in_g, ln_in_b, w_in, b_igate, b_fgate, conv_w, conv_b, m_norm_g, attn_sinks,
                        w_out, ln1_g, ln1_b, w_gate_up, w_down, ln2_g, ln2_b, w_ple, w_ple_gate)
    yp, c_p, n_p, m_p, conv_p, k_p, v_p = _prompt_path(x_prompt, p_prompt[0], wts)
    ys, c_s, n_s, m_s, conv_s, k_s, v_s = _sample_path(
        x_sample, p_sample[0], state_mlstm_C, state_mlstm_n, state_mlstm_m, state_conv,
        cache_win_k, cache_win_v, wts)
    return (yp, ys, c_p, n_p, m_p, conv_p, k_p, v_p, c_s, n_s, m_s, conv_s, k_s, v_s)
```

```python
import functools

import jax
import jax.numpy as jnp
from jax import lax
from jax.experimental import pallas as pl
from jax.experimental.pallas import tpu as pltpu

F32 = jnp.float32
BF16 = jnp.bfloat16

D = 1024
MH = 4
DV = D // MH
DK = DV // 2
CONV_W = 4
QK_CH = 2 * MH * DK
AH = 16
KVH = 2
HD = D // AH
GROUP = AH // KVH
WIN = 128
DFF = 2816
PD = 256
LN_EPS = 1e-5
RMS_EPS = 1e-6
ALPHA = 2.0 ** 0.25
LOG2E = 1.4426950408889634
NEG_INF = float("-inf")

LANES = 128
SUBLANES = 8
MXU_DIM = 256
VMEM_LIMIT = 56 * 1024 * 1024

O_QK = 0
O_MV = O_QK + QK_CH
O_MI = O_MV + D
O_MF = O_MI + MH
O_MO = O_MF + MH
O_AQ = O_MO + D
O_AK = O_AQ + D
O_AV = O_AK + KVH * HD
O_GM = O_AV + KVH * HD
O_GA = O_GM + D
IN_WIDTH = O_GA + D

W_HALF = D // 2
W1_W = 2 * D + W_HALF
W2_W = 2 * D + W_HALF
W3_W = D + 2 * KVH * HD
_WCOL = {"qk": ("w1", 0, D), "mv": ("w1", D, 2 * D), "mo_lo": ("w1", 2 * D, W1_W),
         "mo_hi": ("w2", 0, W_HALF), "aq": ("w2", W_HALF, W_HALF + D), "gm": ("w2", W_HALF + D, W2_W),
         "ga": ("w3", 0, D), "akv": ("w3", D, W3_W)}


def _dot(a, b):
    return jnp.dot(a, b, preferred_element_type=F32)


def _dot_nt(a, b):
    return lax.dot_general(a, b, (((1,), (1,)), ((), ())), preferred_element_type=F32)


def _dot_tn(a, b):
    return lax.dot_general(a, b, (((0,), (0,)), ((), ())), preferred_element_type=F32)


def _proj(xb, w, name):
    blk, a, b = _WCOL[name]
    return _dot(xb, w[blk][:, a:b])


def _layer_norm(x, g, b):
    mu = jnp.mean(x, -1, keepdims=True)
    xc = x - mu
    var = jnp.mean(xc * xc, -1, keepdims=True)
    return xc * lax.rsqrt(var + LN_EPS) * g + b


def _sigmoid(x):
    return 1.0 / (1.0 + jnp.exp(-x))


def _log_sigmoid(x):
    return jnp.minimum(x, 0.0) - jnp.log(1.0 + jnp.exp(-jnp.abs(x)))


def _const_spec(shape):
    zeros = (0,) * len(shape)
    return pl.BlockSpec(shape, lambda i: zeros, pipeline_mode=pl.Buffered(1))


def _dup_halves(x, lo):
    xr = pltpu.roll(x, HD, 1)
    return jnp.where(lo, x, xr), jnp.where(lo, xr, x)


_PROJ_NAMES = ("xn", "zqk", "vm", "smo", "aq", "k2", "v2", "kvl", "sgm", "sga", "gcol", "grow", "brow")
_MIXER_W_NAMES = ("lng", "lnb", "w1", "w2", "w3", "wift", "bift", "cw", "cb", "mng", "sink", "wout_a", "wout_b")


def _mixer_project(x, slot, w, buf, *, tm, lc):
    xn = _layer_norm(x, w["lng"][...], w["lnb"][...])
    xb = xn.astype(BF16)
    buf["xn"][slot] = xn
    buf["zqk"][slot] = _proj(xb, w, "qk")
    buf["vm"][slot] = _proj(xb, w, "mv").astype(BF16)
    mo = jnp.concatenate([_proj(xb, w, "mo_lo"), _proj(xb, w, "mo_hi")], axis=1)
    buf["smo"][slot] = _sigmoid(mo) * w["mng"][...]
    aq = _proj(xb, w, "aq") * (HD ** -0.5 * LOG2E)
    lo_d = (lax.broadcasted_iota(jnp.int32, (tm, D), 1) & (LANES - 1)) < HD
    aqm = (jnp.where(lo_d, aq, 0.0).astype(BF16), jnp.where(lo_d, 0.0, aq).astype(BF16))
    for j in range(tm // WIN):
        for h in range(AH):
            buf["aq"][slot, j, h // GROUP, (h % GROUP) * WIN:(h % GROUP + 1) * WIN, :] = (
                aqm[h % 2][j * WIN:(j + 1) * WIN, (h // 2) * LANES:(h // 2 + 1) * LANES])
    akv = _proj(xb, w, "akv")
    lo = lax.broadcasted_iota(jnp.int32, (tm, LANES), 1) < HD
    k2 = _dup_halves(akv[:, 0:LANES], lo)
    v2 = _dup_halves(akv[:, LANES:2 * LANES], lo)
    for g in range(KVH):
        buf["k2"][slot, g] = k2[g].astype(BF16)
        buf["v2"][slot, g] = v2[g].astype(BF16)
    buf["kvl"][slot] = akv[tm - WIN:tm, :]
    buf["sgm"][slot] = _sigmoid(_proj(xb, w, "gm"))
    buf["sga"][slot] = _sigmoid(_proj(xb, w, "ga"))
    grow = _dot_nt(w["wift"][...], xb) + w["bift"][...]
    buf["grow"][slot] = grow
    lsrow = _log_sigmoid(grow)
    lane = lax.broadcasted_iota(jnp.int32, (SUBLANES, lc), 1)
    pad = jnp.zeros((lc - 2 * SUBLANES, lc), F32)
    for c in range(tm // lc):
        r0 = c * lc
        seg = lsrow[:, r0:r0 + lc]
        sh = 1
        while sh < lc:
            seg = seg + jnp.where(lane >= sh, pltpu.roll(seg, sh, 1), 0.0)
            sh *= 2
        buf["brow"][slot, :, r0:r0 + lc] = seg
        buf["gcol"][slot, r0:r0 + lc, :] = jnp.concatenate([grow[:, r0:r0 + lc], seg, pad], axis=0).T


def _mixer_consume(slot, first, w, buf, st, x1_ref, row0, *, tm, lc):
    c_ref, n_ref, m_ref, ctail, kvc, ym_s, ya_s = st
    x1_ref[0, row0:row0 + tm, :] = ALPHA * buf["xn"][slot]
    z = buf["zqk"][slot]
    tail = ctail[...]
    ctail[...] = z[tm - SUBLANES:tm, :]
    cw = w["cw"][...]
    rows8 = lax.broadcasted_iota(jnp.int32, (SUBLANES, QK_CH), 0)
    y = cw[CONV_W - 1:CONV_W] * z + w["cb"][...]
    for sh in range(1, CONV_W):
        zr = pltpu.roll(z, sh, 0)
        head = jnp.where(rows8 < sh, pltpu.roll(tail, sh, 0), zr[0:SUBLANES])
        zr = jnp.concatenate([head, zr[SUBLANES:]], axis=0)
        y = y + cw[CONV_W - 1 - sh:CONV_W - sh] * zr
    qk = y * _sigmoid(y)

    ri = lax.broadcasted_iota(jnp.int32, (lc, lc), 0)
    ci = lax.broadcasted_iota(jnp.int32, (lc, lc), 1)
    causal = ci <= ri
    for c in range(tm // lc):
        r0 = c * lc
        for h in range(MH):
            qh = qk[r0:r0 + lc, h * DK:(h + 1) * DK]
            kh = qk[r0:r0 + lc, MH * DK + h * DK:MH * DK + (h + 1) * DK] * (DK ** -0.5)
            vh = buf["vm"][slot, r0:r0 + lc, h * DV:(h + 1) * DV]
            qb = qh.astype(BF16)
            bc = buf["gcol"][slot, r0:r0 + lc, SUBLANES + MH + h:SUBLANES + MH + h + 1]
            br = buf["brow"][slot, MH + h:MH + h + 1, r0:r0 + lc]
            li_c = buf["gcol"][slot, r0:r0 + lc, h:h + 1]
            li_r = buf["grow"][slot, h:h + 1, r0:r0 + lc]
            m_prev = m_ref[0, h:h + 1, 0:1]
            c_prev = c_ref[0, h]
            n_prev = n_ref[0, h:h + 1, :]

            dmat = jnp.where(causal, bc - br + li_r, NEG_INF)
            m_inter = bc + m_prev
            m_t = jnp.maximum(m_inter, jnp.max(dmat, -1, keepdims=True))
            wgt = jnp.exp(dmat - m_t) * _dot_nt(qb, kh.astype(BF16))
            inter = jnp.exp(m_inter - m_t)
            num = _dot(wgt.astype(BF16), vh) + inter * _dot(qb, c_prev.astype(BF16))
            den = jnp.sum(wgt, -1, keepdims=True) + inter * jnp.sum(qh * n_prev, -1, keepdims=True)
            denom = jnp.maximum(jnp.abs(den), jnp.exp(-m_t))
            hh = num / denom
            hn = hh * lax.rsqrt(jnp.mean(hh * hh, -1, keepdims=True) + RMS_EPS)
            ym_s[r0:r0 + lc, h * DV:(h + 1) * DV] = buf["smo"][slot, r0:r0 + lc, h * DV:(h + 1) * DV] * hn

            m_new = m_t[lc - 1:lc, :]
            bc_last = bc[lc - 1:lc, :]
            decay = jnp.exp(bc_last + m_prev - m_new)
            ks = kh * jnp.exp(bc_last - bc + li_c - m_new)
            c_ref[0, h] = decay * c_prev + _dot_tn(ks.astype(BF16), vh)
            n_ref[0, h:h + 1, :] = decay * n_prev + jnp.sum(ks, 0, keepdims=True)
            m_ref[0, h:h + 1, :] = jnp.broadcast_to(m_new, (1, LANES))

    lo_q = lax.broadcasted_iota(jnp.int32, (WIN, LANES), 1) < HD
    qi = lax.broadcasted_iota(jnp.int32, (WIN, 2 * WIN), 0)
    kj = lax.broadcasted_iota(jnp.int32, (WIN, 2 * WIN), 1)
    band = (kj >= qi) & (kj <= qi + WIN)
    ones_v = jnp.ones((2 * WIN, LANES), BF16)
    for j in range(tm // WIN):
        r0 = j * WIN
        if j == 0 and first is not None:
            mask = band & (kj >= jnp.where(first, WIN, 0))
        else:
            mask = band
        for g in range(KVH):
            if j == 0:
                kprev, vprev = kvc[g], kvc[KVH + g]
            else:
                kprev, vprev = buf["k2"][slot, g, r0 - WIN:r0, :], buf["v2"][slot, g, r0 - WIN:r0, :]
            kcat = jnp.concatenate([kprev, buf["k2"][slot, g, r0:r0 + WIN, :]], axis=0)
            vcat = jnp.concatenate([vprev, buf["v2"][slot, g, r0:r0 + WIN, :]], axis=0)
            sc_all = _dot_nt(buf["aq"][slot, j, g], kcat)
            es, ds = [], []
            for hh in range(GROUP):
                sink = w["sink"][g * GROUP + hh] * LOG2E
                sc = jnp.where(mask, sc_all[hh * WIN:(hh + 1) * WIN, :], NEG_INF)
                mx = jnp.maximum(jnp.max(sc, -1, keepdims=True), sink)
                es.append(jnp.exp2(sc - mx).astype(BF16))
                ds.append(jnp.exp2(sink - mx))
            o_all = _dot(jnp.concatenate(es, axis=0), jnp.concatenate([vcat, ones_v], axis=1))
            for pp in range(GROUP // 2):
                pj = g * GROUP // 2 + pp
                halves = []
                for hh in (2 * pp, 2 * pp + 1):
                    oh = o_all[hh * WIN:(hh + 1) * WIN, :]
                    halves.append(oh[:, 0:LANES] / (oh[:, LANES:2 * LANES] + ds[hh]))
                ya_s[r0:r0 + WIN, pj * LANES:(pj + 1) * LANES] = jnp.where(lo_q, halves[0], halves[1])
    for g in range(KVH):
        kvc[g] = buf["k2"][slot, g, tm - WIN:tm, :]
        kvc[KVH + g] = buf["v2"][slot, g, tm - WIN:tm, :]

    mixed = (buf["sgm"][slot] * ym_s[...] + buf["sga"][slot] * ya_s[...]).astype(BF16)
    r = jnp.concatenate([_dot(mixed, w["wout_a"][...]), _dot(mixed, w["wout_b"][...])], axis=1)
    x1_ref[0, row0:row0 + tm, :] = x1_ref[0, row0:row0 + tm, :] + r


def _mixer_kernel(*refs, tm, lc, npb):
    nw = len(_MIXER_W_NAMES)
    x0_ref, xa_ref, xb_ref = refs[0:3]
    w = dict(zip(_MIXER_W_NAMES, refs[3:3 + nw]))
    x1_ref, c_ref, n_ref, m_ref, conv_ref, kp_ref, vp_ref = refs[3 + nw:10 + nw]
    scratch = refs[10 + nw:]
    buf = dict(zip(_PROJ_NAMES, scratch[:len(_PROJ_NAMES)]))
    ctail, kvc, ym_s, ya_s = scratch[len(_PROJ_NAMES):]
    st = (c_ref, n_ref, m_ref, ctail, kvc, ym_s, ya_s)
    k = pl.program_id(0)
    first = (2 * k) % npb == 0

    @pl.when(k == 0)
    def _prologue():
        _mixer_project(x0_ref[0], 0, w, buf, tm=tm, lc=lc)

    @pl.when(first)
    def _init():
        c_ref[...] = jnp.zeros_like(c_ref)
        n_ref[...] = jnp.zeros_like(n_ref)
        m_ref[...] = jnp.zeros_like(m_ref)
        ctail[...] = jnp.zeros_like(ctail)
        kvc[...] = jnp.zeros_like(kvc)

    _mixer_project(xa_ref[0], 1, w, buf, tm=tm, lc=lc)
    _mixer_consume(0, first, w, buf, st, x1_ref, 0, tm=tm, lc=lc)
    _mixer_project(xb_ref[0], 0, w, buf, tm=tm, lc=lc)
    _mixer_consume(1, None, w, buf, st, x1_ref, tm, tm=tm, lc=lc)

    @pl.when((2 * k + 1) % npb == npb - 1)
    def _state_out():
        conv_ref[0] = ctail[SUBLANES - (CONV_W - 1):SUBLANES, :]
        kp_ref[0] = buf["kvl"][1, :, 0:LANES]
        vp_ref[0] = buf["kvl"][1, :, LANES:2 * LANES]


def _prompt_mixer(x, wts, *, tm=256, lc=128):
    B, S, _ = x.shape
    npb = S // tm
    assert S % tm == 0 and npb % 2 == 0 and tm % lc == 0 and tm % WIN == 0
    nt = B * npb
    hpb = npb // 2
    cs = _const_spec

    def tile_map(off):
        def imap(k):
            t = jnp.minimum(2 * k + off, nt - 1)
            return (t // npb, t % npb, 0)
        return imap

    in_specs = [
        pl.BlockSpec((1, tm, D), lambda k: (0, 0, 0), pipeline_mode=pl.Buffered(1)),
        pl.BlockSpec((1, tm, D), tile_map(1)),
        pl.BlockSpec((1, tm, D), tile_map(2)),
        cs((1, D)), cs((1, D)),
        cs((D, W1_W)), cs((D, W2_W)), cs((D, W3_W)), cs((SUBLANES, D)), cs((SUBLANES, 1)),
        cs((CONV_W, QK_CH)), cs((1, QK_CH)), cs((1, D)),
        pl.BlockSpec(memory_space=pltpu.SMEM),
        cs((D, W_HALF)), cs((D, W_HALF)),
    ]
    out_shape = (
        jax.ShapeDtypeStruct((B, S, D), F32),
        jax.ShapeDtypeStruct((B, MH, DK, DV), F32),
        jax.ShapeDtypeStruct((B, MH, DK), F32),
        jax.ShapeDtypeStruct((B, SUBLANES, LANES), F32),
        jax.ShapeDtypeStruct((B, CONV_W - 1, QK_CH), F32),
        jax.ShapeDtypeStruct((B, WIN, KVH * HD), F32),
        jax.ShapeDtypeStruct((B, WIN, KVH * HD), F32),
    )
    out_specs = (
        pl.BlockSpec((1, 2 * tm, D), lambda k: (k // hpb, k % hpb, 0)),
        pl.BlockSpec((1, MH, DK, DV), lambda k: (k // hpb, 0, 0, 0)),
        pl.BlockSpec((1, MH, DK), lambda k: (k // hpb, 0, 0)),
        pl.BlockSpec((1, SUBLANES, LANES), lambda k: (k // hpb, 0, 0)),
        pl.BlockSpec((1, CONV_W - 1, QK_CH), lambda k: (k // hpb, 0, 0)),
        pl.BlockSpec((1, WIN, KVH * HD), lambda k: (k // hpb, 0, 0)),
        pl.BlockSpec((1, WIN, KVH * HD), lambda k: (k // hpb, 0, 0)),
    )
    proj_scratch = {
        "xn": pltpu.VMEM((2, tm, D), F32), "zqk": pltpu.VMEM((2, tm, QK_CH), F32),
        "vm": pltpu.VMEM((2, tm, D), BF16), "smo": pltpu.VMEM((2, tm, D), F32),
        "aq": pltpu.VMEM((2, tm // WIN, KVH, GROUP * WIN, LANES), BF16),
        "k2": pltpu.VMEM((2, KVH, tm, LANES), BF16), "v2": pltpu.VMEM((2, KVH, tm, LANES), BF16),
        "kvl": pltpu.VMEM((2, WIN, 2 * LANES), F32),
        "sgm": pltpu.VMEM((2, tm, D), F32), "sga": pltpu.VMEM((2, tm, D), F32),
        "gcol": pltpu.VMEM((2, tm, LANES), F32), "grow": pltpu.VMEM((2, SUBLANES, tm), F32),
        "brow": pltpu.VMEM((2, SUBLANES, tm), F32),
    }
    scratch = [proj_scratch[n] for n in _PROJ_NAMES] + [
        pltpu.VMEM((SUBLANES, QK_CH), F32),
        pltpu.VMEM((2 * KVH, WIN, LANES), BF16),
        pltpu.VMEM((tm, D), F32),
        pltpu.VMEM((tm, D), F32),
    ]
    return pl.pallas_call(
        functools.partial(_mixer_kernel, tm=tm, lc=lc, npb=npb),
        grid=(nt // 2,), in_specs=in_specs, out_specs=out_specs, out_shape=out_shape,
        scratch_shapes=scratch, name="prompt_mixer",
        compiler_params=pltpu.CompilerParams(dimension_semantics=("arbitrary",), vmem_limit_bytes=VMEM_LIMIT),
    )(x, x, x, wts["ln_in_g"], wts["ln_in_b"], wts["w1"], wts["w2"], wts["w3"], wts["w_ift"], wts["b_ift"],
      wts["conv_w"], wts["conv_b"], wts["m_norm_g"], wts["sinks"], wts["w_out_a"], wts["w_out_b"])


FFN_CHUNKS = ((0, 6 * MXU_DIM), (6 * MXU_DIM, DFF))


def _ffn_kernel(x_ref, p_ref, l1g_ref, l1b_ref, wgu_ref, wd_ref, wpg_ref, wp_ref, g_ref, b_ref, o_ref, *, sub):
    for r0 in range(0, x_ref.shape[0], sub):
        x = _layer_norm(x_ref[r0:r0 + sub, :], l1g_ref[...], l1b_ref[...])
        xb = x.astype(BF16)
        acc = ALPHA * x + (_sigmoid(_dot(xb, wpg_ref[...]))
                           * _dot(p_ref[r0:r0 + sub, :].astype(BF16), wp_ref[...]))
        for c0, c1 in FFN_CHUNKS:
            g = _dot(xb, wgu_ref[:, c0:c1])
            u = _dot(xb, wgu_ref[:, DFF + c0:DFF + c1])
            hcat = (g * _sigmoid(g) * u).astype(BF16)
            acc = acc + _dot(hcat, wd_ref[c0:c1, :])
        o_ref[r0:r0 + sub, :] = _layer_norm(acc, g_ref[...], b_ref[...])


def _ffn(x, p, wts, *, tm, sub):
    N = x.shape[0]
    assert N % tm == 0 and tm % sub == 0
    cs = _const_spec
    return pl.pallas_call(
        functools.partial(_ffn_kernel, sub=sub),
        grid=(N // tm,),
        in_specs=[pl.BlockSpec((tm, D), lambda i: (i, 0)), pl.BlockSpec((tm, PD), lambda i: (i, 0)),
                  cs((1, D)), cs((1, D)),
                  cs((D, 2 * DFF)), cs((DFF, D)), cs((D, D)), cs((PD, D)), cs((1, D)), cs((1, D))],
        out_specs=pl.BlockSpec((tm, D), lambda i: (i, 0)),
        out_shape=jax.ShapeDtypeStruct((N, D), F32),
        name="ffn",
        compiler_params=pltpu.CompilerParams(dimension_semantics=("arbitrary",), vmem_limit_bytes=VMEM_LIMIT),
    )(x, p, wts["ln1_g"], wts["ln1_b"], wts["w_gu"], wts["w_d"], wts["w_pg"], wts["w_p"],
      wts["ln2_g"], wts["ln2_b"])


Z_V = 0
Z_MO = Z_V + D
Z_GM = Z_MO + D
Z_GA = Z_GM + D
Z_XN = Z_GA + D
Z_Q = Z_XN + D
Z_KW = Z_Q + MH * DK
Z_SC = Z_KW + MH * DK
Z_KN = Z_SC + LANES
Z_VN = Z_KN + LANES
Z_W = Z_VN + LANES


def _sample_pre_kernel(x_ref, lng_ref, lnb_ref, w1_ref, w2_ref, w3_ref, wif_ref, bif_ref,
                       cw_ref, cb_ref, cs_ref, n_ref, m_ref,
                       z_ref, qs_ref, convo_ref, no_ref, mo_ref, *, nb, tb):
    w = {"w1": w1_ref, "w2": w2_ref, "w3": w3_ref}
    xn = _layer_norm(x_ref[...], lng_ref[...], lnb_ref[...])
    xb = xn.astype(BF16)
    zqk = _proj(xb, w, "qk")
    cw = cw_ref[...]
    y = (cw[3:4] * zqk + cw[2:3] * cs_ref[:, 2 * QK_CH:3 * QK_CH] + cw[1:2] * cs_ref[:, QK_CH:2 * QK_CH]
         + cw[0:1] * cs_ref[:, 0:QK_CH] + cb_ref[...])
    qk = y * _sigmoid(y)
    convo_ref[:, 0:2 * QK_CH] = cs_ref[:, QK_CH:3 * QK_CH]
    convo_ref[:, 2 * QK_CH:3 * QK_CH] = zqk

    gcol = _dot(xb, wif_ref[...]) + bif_ref[...]
    ig = gcol[:, 0:MH]
    m_inter = _log_sigmoid(gcol[:, MH:2 * MH]) + m_ref[...]
    m_t = jnp.maximum(m_inter, ig)
    inter = jnp.exp(m_inter - m_t)
    wi = jnp.exp(ig - m_t)
    enm = jnp.exp(-m_t)
    mo_ref[...] = m_t

    z_ref[:, Z_V:Z_V + D] = _proj(xb, w, "mv")
    z_ref[:, Z_MO:Z_MO + W_HALF] = _proj(xb, w, "mo_lo")
    z_ref[:, Z_MO + W_HALF:Z_MO + D] = _proj(xb, w, "mo_hi")
    z_ref[:, Z_GM:Z_GM + D] = _proj(xb, w, "gm")
    z_ref[:, Z_GA:Z_GA + D] = _proj(xb, w, "ga")
    z_ref[:, Z_XN:Z_XN + D] = xn
    z_ref[:, Z_Q:Z_Q + MH * DK] = qk[:, 0:MH * DK]
    sc = jnp.zeros((nb, LANES), F32)
    lane = lax.broadcasted_iota(jnp.int32, (nb, LANES), 1)
    for h in range(MH):
        qh = qk[:, h * DK:(h + 1) * DK]
        kh = qk[:, MH * DK + h * DK:MH * DK + (h + 1) * DK] * (DK ** -0.5)
        nh = n_ref[:, h * DK:(h + 1) * DK]
        wv = wi[:, h:h + 1] * jnp.sum(qh * kh, -1, keepdims=True)
        den = wv + inter[:, h:h + 1] * jnp.sum(qh * nh, -1, keepdims=True)
        denom = jnp.maximum(jnp.abs(den), enm[:, h:h + 1])
        kw = wi[:, h:h + 1] * kh
        z_ref[:, Z_KW + h * DK:Z_KW + (h + 1) * DK] = kw
        no_ref[:, h * DK:(h + 1) * DK] = inter[:, h:h + 1] * nh + kw
        sc = jnp.where(lane == h, inter[:, h:h + 1], sc)
        sc = jnp.where(lane == MH + h, wv, sc)
        sc = jnp.where(lane == 2 * MH + h, denom, sc)
    z_ref[:, Z_SC:Z_SC + LANES] = sc

    akv = _proj(xb, w, "akv")
    z_ref[:, Z_KN:Z_KN + LANES] = akv[:, 0:LANES]
    z_ref[:, Z_VN:Z_VN + LANES] = akv[:, LANES:2 * LANES]
    aq = _proj(xb, w, "aq") * (HD ** -0.5)
    lo = lax.broadcasted_iota(jnp.int32, (nb, LANES), 1) < HD
    for h in range(AH):
        pair = aq[:, (h // 2) * LANES:(h // 2 + 1) * LANES]
        qm = jnp.where(lo if h % 2 == 0 else ~lo, pair, 0.0)
        for t in range(nb // tb):
            qs_ref[(t * AH + h) * tb:(t * AH + h + 1) * tb, :] = qm[t * tb:(t + 1) * tb, :]


def _sample_state_kernel(z_ref, qs_ref, sink_ref, c_ref, ck_ref, cv_ref,
                         co_ref, qc_ref, os_ref, ko_ref, vo_ref, *, tb):
    lo = lax.broadcasted_iota(jnp.int32, (WIN, LANES), 1) < HD
    row = lax.broadcasted_iota(jnp.int32, (WIN, LANES), 0)
    rowt = lax.broadcasted_iota(jnp.int32, (tb, DK), 0)
    sinks = sink_ref[...]
    q_all = z_ref[:, Z_Q:Z_Q + MH * DK]
    kw_all = z_ref[:, Z_KW:Z_KW + MH * DK]
    v_all = z_ref[:, Z_V:Z_V + D]
    sc = z_ref[:, Z_SC:Z_SC + LANES]
    kn_all = z_ref[:, Z_KN:Z_KN + LANES]
    vn_all = z_ref[:, Z_VN:Z_VN + LANES]
    kn2_all = _dup_halves(kn_all, lo[0:tb])
    vn2_all = _dup_halves(vn_all, lo[0:tb])
    for i in range(tb):
        for h in range(MH):
            cb = c_ref[i, h]
            qc = _dot(q_all[:, h * DK:(h + 1) * DK].astype(BF16), cb.astype(BF16))
            qc_ref[i:i + 1, h * DV:(h + 1) * DV] = qc[i:i + 1]
            kw_i = jnp.where(rowt == i, kw_all[:, h * DK:(h + 1) * DK], 0.0)
            outer = _dot_tn(kw_i, v_all[:, h * DV:(h + 1) * DV])
            co_ref[i, h] = sc[i:i + 1, h:h + 1] * cb + outer
        kc = ck_ref[i]
        vc = cv_ref[i]
        kn = kn_all[i:i + 1]
        vn = vn_all[i:i + 1]
        k2 = _dup_halves(kc, lo)
        v2 = _dup_halves(vc, lo)
        kn2 = [a[i:i + 1] for a in kn2_all]
        vn2 = [a[i:i + 1] for a in vn2_all]
        qrow = qs_ref[pl.ds(i, AH, stride=tb), :]
        outs = []
        for g in range(KVH):
            qg = qrow[g * GROUP:(g + 1) * GROUP]
            s = _dot_nt(qg.astype(BF16), k2[g].astype(BF16))
            s_self = jnp.sum(qg * kn2[g], -1, keepdims=True)
            sg = sinks[g * GROUP:(g + 1) * GROUP]
            mx = jnp.maximum(jnp.maximum(jnp.max(s, -1, keepdims=True), s_self), sg)
            e = jnp.exp(s - mx)
            es = jnp.exp(s_self - mx)
            dsum = jnp.sum(e, -1, keepdims=True) + es + jnp.exp(sg - mx)
            outs.append((_dot(e.astype(BF16), v2[g].astype(BF16)) + es * vn2[g]) / dsum)
        os_ref[pl.ds(i, AH, stride=tb), :] = jnp.concatenate(outs, axis=0)
        ko_ref[i] = jnp.where(row == WIN - 1, kn, pltpu.roll(kc, WIN - 1, 0))
        vo_ref[i] = jnp.where(row == WIN - 1, vn, pltpu.roll(vc, WIN - 1, 0))


def _sample_post_kernel(z_ref, qc_ref, os_ref, mng_ref, wouta_ref, woutb_ref, x1_ref, *, nb, tb):
    sc = z_ref[:, Z_SC:Z_SC + LANES]
    mng = mng_ref[...]
    lo = lax.broadcasted_iota(jnp.int32, (nb, LANES), 1) < HD
    ym = []
    for h in range(MH):
        v = z_ref[:, Z_V + h * DV:Z_V + (h + 1) * DV]
        num = sc[:, MH + h:MH + h + 1] * v + sc[:, h:h + 1] * qc_ref[:, h * DV:(h + 1) * DV]
        hh = num / sc[:, 2 * MH + h:2 * MH + h + 1]
        hn = hh * lax.rsqrt(jnp.mean(hh * hh, -1, keepdims=True) + RMS_EPS)
        ym.append(_sigmoid(z_ref[:, Z_MO + h * DV:Z_MO + (h + 1) * DV]) * hn * mng[:, h * DV:(h + 1) * DV])
    ym = jnp.concatenate(ym, axis=1)
    ya = []
    for pj in range(AH // 2):
        halves = []
        for par in range(2):
            h = 2 * pj + par
            halves.append(jnp.concatenate(
                [os_ref[(t * AH + h) * tb:(t * AH + h + 1) * tb, :] for t in range(nb // tb)], axis=0))
        ya.append(jnp.where(lo, halves[0], halves[1]))
    ya = jnp.concatenate(ya, axis=1)
    mixed = _sigmoid(z_ref[:, Z_GM:Z_GM + D]) * ym + _sigmoid(z_ref[:, Z_GA:Z_GA + D]) * ya
    mixed = mixed.astype(BF16)
    r = jnp.concatenate([_dot(mixed, wouta_ref[...]), _dot(mixed, woutb_ref[...])], axis=1)
    x1_ref[...] = ALPHA * z_ref[:, Z_XN:Z_XN + D] + r


def _sample_mixer(x, c0, n0, m0, conv0, k0, v0, wts, *, tb=8):
    nb = x.shape[0]
    assert nb % tb == 0
    vmem = pltpu.CompilerParams(dimension_semantics=("arbitrary",), vmem_limit_bytes=VMEM_LIMIT)
    full = lambda shape: pl.BlockSpec(shape, lambda i: (0,) * len(shape))
    pre_in = [x, wts["ln_in_g"], wts["ln_in_b"], wts["w1"], wts["w2"], wts["w3"], wts["w_if"], wts["b_if"],
              wts["conv_w"], wts["conv_b"], conv0, n0, m0]
    z, qs, conv_new, n_new, m_new = pl.pallas_call(
        functools.partial(_sample_pre_kernel, nb=nb, tb=tb),
        grid=(1,),
        in_specs=[full(a.shape) for a in pre_in],
        out_specs=(full((nb, Z_W)), full((nb * AH, LANES)), full((nb, (CONV_W - 1) * QK_CH)),
                   full((nb, MH * DK)), full((nb, MH))),
        out_shape=(jax.ShapeDtypeStruct((nb, Z_W), F32), jax.ShapeDtypeStruct((nb * AH, LANES), F32),
                   jax.ShapeDtypeStruct((nb, (CONV_W - 1) * QK_CH), F32),
                   jax.ShapeDtypeStruct((nb, MH * DK), F32), jax.ShapeDtypeStruct((nb, MH), F32)),
        name="sample_pre", compiler_params=vmem,
    )(*pre_in)

    c_new, qc, os_, k_new, v_new = pl.pallas_call(
        functools.partial(_sample_state_kernel, tb=tb),
        grid=(nb // tb,),
        in_specs=[pl.BlockSpec((tb, Z_W), lambda i: (i, 0)),
                  pl.BlockSpec((tb * AH, LANES), lambda i: (i, 0)),
                  pl.BlockSpec((AH, 1), lambda i: (0, 0)),
                  pl.BlockSpec((tb, MH, DK, DV), lambda i: (i, 0, 0, 0)),
                  pl.BlockSpec((tb, WIN, LANES), lambda i: (i, 0, 0)),
                  pl.BlockSpec((tb, WIN, LANES), lambda i: (i, 0, 0))],
        out_specs=(pl.BlockSpec((tb, MH, DK, DV), lambda i: (i, 0, 0, 0)),
                   pl.BlockSpec((tb, D), lambda i: (i, 0)),
                   pl.BlockSpec((tb * AH, LANES), lambda i: (i, 0)),
                   pl.BlockSpec((tb, WIN, LANES), lambda i: (i, 0, 0)),
                   pl.BlockSpec((tb, WIN, LANES), lambda i: (i, 0, 0))),
        out_shape=(jax.ShapeDtypeStruct((nb, MH, DK, DV), F32), jax.ShapeDtypeStruct((nb, D), F32),
                   jax.ShapeDtypeStruct((nb * AH, LANES), F32),
                   jax.ShapeDtypeStruct((nb, WIN, LANES), F32), jax.ShapeDtypeStruct((nb, WIN, LANES), F32)),
        name="sample_state", compiler_params=vmem,
    )(z, qs, wts["sinks_col"], c0, k0, v0)

    post_in = [z, qc, os_, wts["m_norm_g"], wts["w_out_a"], wts["w_out_b"]]
    x1 = pl.pallas_call(
        functools.partial(_sample_post_kernel, nb=nb, tb=tb),
        grid=(1,),
        in_specs=[full(a.shape) for a in post_in],
        out_specs=full((nb, D)),
        out_shape=jax.ShapeDtypeStruct((nb, D), F32),
        name="sample_post", compiler_params=vmem,
    )(*post_in)
    return x1, c_new, n_new, m_new, conv_new, k_new, v_new


def _prep_weights(ln_in_g, ln_in_b, w_in, b_igate, b_fgate, conv_w, conv_b, m_norm_g, attn_sinks, w_out,
                  ln1_g, ln1_b, w_gate_up, w_down, ln2_g, ln2_b, w_ple, w_ple_gate):
    w = w_in[0]
    row = lambda a: a.reshape(1, -1).astype(F32)
    w_if = jnp.pad(w[:, O_MI:O_MO], ((0, 0), (0, LANES - 2 * MH)))
    b_if = jnp.pad(jnp.concatenate([b_igate[0], b_fgate[0]]), (0, LANES - 2 * MH))
    cols = lambda a, b: w[:, a:b].astype(BF16)
    w1 = jnp.concatenate([cols(O_QK, O_MI), cols(O_MO, O_MO + W_HALF)], axis=1)
    w2 = jnp.concatenate([cols(O_MO + W_HALF, O_AK), cols(O_GM, O_GA)], axis=1)
    w3 = jnp.concatenate([cols(O_GA, IN_WIDTH), cols(O_AK, O_GM)], axis=1)
    wo = w_out[0].astype(BF16)
    return {
        "ln_in_g": row(ln_in_g), "ln_in_b": row(ln_in_b),
        "w1": w1, "w2": w2, "w3": w3,
        "w_if": w_if.astype(BF16), "w_ift": w[:, O_MI:O_MO].T.astype(BF16),
        "b_if": b_if.reshape(1, LANES), "b_ift": b_if[:2 * MH].reshape(2 * MH, 1),
        "conv_w": conv_w[0], "conv_b": row(conv_b[0]), "m_norm_g": row(m_norm_g[0]),
        "sinks": attn_sinks[0], "sinks_col": attn_sinks[0].reshape(AH, 1),
        "w_out_a": wo[:, :W_HALF], "w_out_b": wo[:, W_HALF:], "ln1_g": row(ln1_g[0]), "ln1_b": row(ln1_b[0]),
        "w_gu": w_gate_up[0].astype(BF16),
        "w_d": w_down[0].astype(BF16), "w_pg": w_ple_gate[0].astype(BF16), "w_p": w_ple[0].astype(BF16),
        "ln2_g": row(ln2_g[0]), "ln2_b": row(ln2_b[0]),
    }


def _prompt_path(x, p, wts, *, tm=256, lc=128, tmf=1024):
    B, S, _ = x.shape
    x1, c, n, m, conv, k, v = _prompt_mixer(x, wts, tm=tm, lc=lc)
    y = _ffn(x1.reshape(B * S, D), p.reshape(B * S, PD), wts, tm=tmf, sub=tmf // 2).reshape(B, S, D)
    return (y, c[None], n[None], m[None, :, :MH, 0], conv[None],
            k.reshape(1, B, WIN, KVH, HD), v.reshape(1, B, WIN, KVH, HD))


def _sample_path(x, p, c0, n0, m0, conv0, k0, v0, wts, *, tb=8):
    nb = x.shape[0]
    x1, c, n, m, conv, k, v = _sample_mixer(
        x.reshape(nb, D), c0[0], n0[0].reshape(nb, MH * DK), m0[0],
        conv0[0].reshape(nb, (CONV_W - 1) * QK_CH), k0[0].reshape(nb, WIN, LANES),
        v0[0].reshape(nb, WIN, LANES), wts, tb=tb)
    y = _ffn(x1, p.reshape(nb, PD), wts, tm=nb, sub=nb).reshape(nb, 1, D)
    return (y, c[None], n.reshape(1, nb, MH, DK), m[None], conv.reshape(1, nb, CONV_W - 1, QK_CH),
            k.reshape(1, nb, WIN, KVH, HD), v.reshape(1, nb, WIN, KVH, HD))


def kernel(x_prompt, x_sample, state_mlstm_C, state_mlstm_n, state_mlstm_m, state_conv, cache_win_k, cache_win_v, p_prompt, p_sample, ln_in_g, ln_in_b, w_in, b_igate, b_fgate, conv_w, conv_b, m_norm_g, attn_sinks, w_out, ln1_g, ln1_b, w_gate_up, w_down, ln2_g, ln2_b, w_ple, w_ple_gate):
    wts = _prep_weights(ln_in_g, ln_in_b, w_in, b_igate, b_fgate, conv_w, conv_b, m_norm_g, attn_sinks,
                        w_out, ln1_g, ln1_b, w_gate_up, w_down, ln2_g, ln2_b, w_ple, w_ple_gate)
    yp, c_p, n_p, m_p, conv_p, k_p, v_p = _prompt_path(x_prompt, p_prompt[0], wts)
    ys, c_s, n_s, m_s, conv_s, k_s, v_s = _sample_path(
        x_sample, p_sample[0], state_mlstm_C, state_mlstm_n, state_mlstm_m, state_conv,
        cache_win_k, cache_win_v, wts)
    return (yp, ys, c_p, n_p, m_p, conv_p, k_p, v_p, c_s, n_s, m_s, conv_s, k_s, v_s)
```

```python
import functools

import jax
import jax.numpy as jnp
from jax import lax
from jax.experimental import pallas as pl
from jax.experimental.pallas import tpu as pltpu

F32 = jnp.float32
BF16 = jnp.bfloat16

D = 1024
MH = 4
DV = D // MH
DK = DV // 2
CONV_W = 4
QK_CH = 2 * MH * DK
AH = 16
KVH = 2
HD = D // AH
GROUP = AH // KVH
WIN = 128
DFF = 2816
PD = 256
LN_EPS = 1e-5
RMS_EPS = 1e-6
ALPHA = 2.0 ** 0.25
LOG2E = 1.4426950408889634
NEG_INF = float("-inf")

LANES = 128
SUBLANES = 8
MXU_DIM = 256
VMEM_LIMIT = 56 * 1024 * 1024

O_QK = 0
O_MV = O_QK + QK_CH
O_MI = O_MV + D
O_MF = O_MI + MH
O_MO = O_MF + MH
O_AQ = O_MO + D
O_AK = O_AQ + D
O_AV = O_AK + KVH * HD
O_GM = O_AV + KVH * HD
O_GA = O_GM + D
IN_WIDTH = O_GA + D

W_HALF = D // 2
W1_W = 2 * D + W_HALF
W2_W = 2 * D + W_HALF
W3_W = D + 2 * KVH * HD
_WCOL = {"qk": ("w1", 0, D), "mv": ("w1", D, 2 * D), "mo_lo": ("w1", 2 * D, W1_W),
         "mo_hi": ("w2", 0, W_HALF), "aq": ("w2", W_HALF, W_HALF + D), "gm": ("w2", W_HALF + D, W2_W),
         "ga": ("w3", 0, D), "akv": ("w3", D, W3_W)}


def _dot(a, b):
    return jnp.dot(a, b, preferred_element_type=F32)


def _dot_nt(a, b):
    return lax.dot_general(a, b, (((1,), (1,)), ((), ())), preferred_element_type=F32)


def _dot_tn(a, b):
    return lax.dot_general(a, b, (((0,), (0,)), ((), ())), preferred_element_type=F32)


def _proj(xb, w, name):
    blk, a, b = _WCOL[name]
    return _dot(xb, w[blk][:, a:b])


def _layer_norm(x, g, b):
    mu = jnp.mean(x, -1, keepdims=True)
    xc = x - mu
    var = jnp.mean(xc * xc, -1, keepdims=True)
    return xc * lax.rsqrt(var + LN_EPS) * g + b


def _sigmoid(x):
    return 1.0 / (1.0 + jnp.exp(-x))


def _log_sigmoid(x):
    return jnp.minimum(x, 0.0) - jnp.log(1.0 + jnp.exp(-jnp.abs(x)))


def _const_spec(shape):
    zeros = (0,) * len(shape)
    return pl.BlockSpec(shape, lambda i: zeros, pipeline_mode=pl.Buffered(1))


def _dup_halves(x, lo):
    xr = pltpu.roll(x, HD, 1)
    return jnp.where(lo, x, xr), jnp.where(lo, xr, x)


_PROJ_NAMES = ("xn", "zqk", "vm", "smo", "aq", "k2", "v2", "kvl", "sgm", "sga", "gcol", "grow", "brow")
_MIXER_W_NAMES = ("lng", "lnb", "w1", "w2", "w3", "wift", "bift", "cw", "cb", "mng", "sink", "wout_a", "wout_b")


def _mixer_project(x, slot, w, buf, *, tm, lc):
    xn = _layer_norm(x, w["lng"][...], w["lnb"][...])
    xb = xn.astype(BF16)
    buf["xn"][slot] = xn
    buf["zqk"][slot] = _proj(xb, w, "qk")
    buf["vm"][slot] = _proj(xb, w, "mv").astype(BF16)
    mo = jnp.concatenate([_proj(xb, w, "mo_lo"), _proj(xb, w, "mo_hi")], axis=1)
    buf["smo"][slot] = _sigmoid(mo) * w["mng"][...]
    aq = _proj(xb, w, "aq") * (HD ** -0.5 * LOG2E)
    lo_d = (lax.broadcasted_iota(jnp.int32, (tm, D), 1) & (LANES - 1)) < HD
    aqm = (jnp.where(lo_d, aq, 0.0).astype(BF16), jnp.where(lo_d, 0.0, aq).astype(BF16))
    for j in range(tm // WIN):
        for h in range(AH):
            buf["aq"][slot, j, h // GROUP, (h % GROUP) * WIN:(h % GROUP + 1) * WIN, :] = (
                aqm[h % 2][j * WIN:(j + 1) * WIN, (h // 2) * LANES:(h // 2 + 1) * LANES])
    akv = _proj(xb, w, "akv")
    lo = lax.broadcasted_iota(jnp.int32, (tm, LANES), 1) < HD
    k2 = _dup_halves(akv[:, 0:LANES], lo)
    v2 = _dup_halves(akv[:, LANES:2 * LANES], lo)
    for g in range(KVH):
        buf["k2"][slot, g] = k2[g].astype(BF16)
        buf["v2"][slot, g] = v2[g].astype(BF16)
    buf["kvl"][slot] = akv[tm - WIN:tm, :]
    buf["sgm"][slot] = _sigmoid(_proj(xb, w, "gm"))
    buf["sga"][slot] = _sigmoid(_proj(xb, w, "ga"))
    grow = _dot_nt(w["wift"][...], xb) + w["bift"][...]
    buf["grow"][slot] = grow
    lsrow = _log_sigmoid(grow)
    lane = lax.broadcasted_iota(jnp.int32, (SUBLANES, lc), 1)
    pad = jnp.zeros((lc - 2 * SUBLANES, lc), F32)
    for c in range(tm // lc):
        r0 = c * lc
        seg = lsrow[:, r0:r0 + lc]
        sh = 1
        while sh < lc:
            seg = seg + jnp.where(lane >= sh, pltpu.roll(seg, sh, 1), 0.0)
            sh *= 2
        buf["brow"][slot, :, r0:r0 + lc] = seg
        buf["gcol"][slot, r0:r0 + lc, :] = jnp.concatenate([grow[:, r0:r0 + lc], seg, pad], axis=0).T


def _mixer_consume(slot, first, w, buf, st, x1_ref, row0, *, tm, lc):
    c_ref, n_ref, m_ref, ctail, kvc, ym_s, ya_s = st
    x1_ref[0, row0:row0 + tm, :] = ALPHA * buf["xn"][slot]
    z = buf["zqk"][slot]
    tail = ctail[...]
    ctail[...] = z[tm - SUBLANES:tm, :]
    cw = w["cw"][...]
    rows8 = lax.broadcasted_iota(jnp.int32, (SUBLANES, QK_CH), 0)
    y = cw[CONV_W - 1:CONV_W] * z + w["cb"][...]
    for sh in range(1, CONV_W):
        zr = pltpu.roll(z, sh, 0)
        head = jnp.where(rows8 < sh, pltpu.roll(tail, sh, 0), zr[0:SUBLANES])
        zr = jnp.concatenate([head, zr[SUBLANES:]], axis=0)
        y = y + cw[CONV_W - 1 - sh:CONV_W - sh] * zr
    qk = y * _sigmoid(y)

    ri = lax.broadcasted_iota(jnp.int32, (lc, lc), 0)
    ci = lax.broadcasted_iota(jnp.int32, (lc, lc), 1)
    causal = ci <= ri
    for c in range(tm // lc):
        r0 = c * lc
        for h in range(MH):
            qh = qk[r0:r0 + lc, h * DK:(h + 1) * DK]
            kh = qk[r0:r0 + lc, MH * DK + h * DK:MH * DK + (h + 1) * DK] * (DK ** -0.5)
            vh = buf["vm"][slot, r0:r0 + lc, h * DV:(h + 1) * DV]
            qb = qh.astype(BF16)
            bc = buf["gcol"][slot, r0:r0 + lc, SUBLANES + MH + h:SUBLANES + MH + h + 1]
            br = buf["brow"][slot, MH + h:MH + h + 1, r0:r0 + lc]
            li_c = buf["gcol"][slot, r0:r0 + lc, h:h + 1]
            li_r = buf["grow"][slot, h:h + 1, r0:r0 + lc]
            m_prev = m_ref[0, h:h + 1, 0:1]
            c_prev = c_ref[0, h]
            n_prev = n_ref[0, h:h + 1, :]

            dmat = jnp.where(causal, bc - br + li_r, NEG_INF)
            m_inter = bc + m_prev
            m_t = jnp.maximum(m_inter, jnp.max(dmat, -1, keepdims=True))
            wgt = jnp.exp(dmat - m_t) * _dot_nt(qb, kh.astype(BF16))
            inter = jnp.exp(m_inter - m_t)
            num = _dot(wgt.astype(BF16), vh) + inter * _dot(qb, c_prev.astype(BF16))
            den = jnp.sum(wgt, -1, keepdims=True) + inter * jnp.sum(qh * n_prev, -1, keepdims=True)
            denom = jnp.maximum(jnp.abs(den), jnp.exp(-m_t))
            hh = num / denom
            hn = hh * lax.rsqrt(jnp.mean(hh * hh, -1, keepdims=True) + RMS_EPS)
            ym_s[r0:r0 + lc, h * DV:(h + 1) * DV] = buf["smo"][slot, r0:r0 + lc, h * DV:(h + 1) * DV] * hn

            m_new = m_t[lc - 1:lc, :]
            bc_last = bc[lc - 1:lc, :]
            decay = jnp.exp(bc_last + m_prev - m_new)
            ks = kh * jnp.exp(bc_last - bc + li_c - m_new)
            c_ref[0, h] = decay * c_prev + _dot_tn(ks.astype(BF16), vh)
            n_ref[0, h:h + 1, :] = decay * n_prev + jnp.sum(ks, 0, keepdims=True)
            m_ref[0, h:h + 1, :] = jnp.broadcast_to(m_new, (1, LANES))

    lo_q = lax.broadcasted_iota(jnp.int32, (WIN, LANES), 1) < HD
    qi = lax.broadcasted_iota(jnp.int32, (WIN, 2 * WIN), 0)
    kj = lax.broadcasted_iota(jnp.int32, (WIN, 2 * WIN), 1)
    band = (kj >= qi) & (kj <= qi + WIN)
    ones_v = jnp.ones((2 * WIN, LANES), BF16)
    for j in range(tm // WIN):
        r0 = j * WIN
        if j == 0 and first is not None:
            mask = band & (kj >= jnp.where(first, WIN, 0))
        else:
            mask = band
        for g in range(KVH):
            if j == 0:
                kprev, vprev = kvc[g], kvc[KVH + g]
            else:
                kprev, vprev = buf["k2"][slot, g, r0 - WIN:r0, :], buf["v2"][slot, g, r0 - WIN:r0, :]
            kcat = jnp.concatenate([kprev, buf["k2"][slot, g, r0:r0 + WIN, :]], axis=0)
            vcat = jnp.concatenate([vprev, buf["v2"][slot, g, r0:r0 + WIN, :]], axis=0)
            sc_all = _dot_nt(buf["aq"][slot, j, g], kcat)
            es, ds = [], []
            for hh in range(GROUP):
                sink = w["sink"][g * GROUP + hh] * LOG2E
                sc = jnp.where(mask, sc_all[hh * WIN:(hh + 1) * WIN, :], NEG_INF)
                mx = jnp.maximum(jnp.max(sc, -1, keepdims=True), sink)
                es.append(jnp.exp2(sc - mx).astype(BF16))
                ds.append(jnp.exp2(sink - mx))
            o_all = _dot(jnp.concatenate(es, axis=0), jnp.concatenate([vcat, ones_v], axis=1))
            for pp in range(GROUP // 2):
                pj = g * GROUP // 2 + pp
                halves = []
                for hh in (2 * pp, 2 * pp + 1):
                    oh = o_all[hh * WIN:(hh + 1) * WIN, :]
                    halves.append(oh[:, 0:LANES] / (oh[:, LANES:2 * LANES] + ds[hh]))
                ya_s[r0:r0 + WIN, pj * LANES:(pj + 1) * LANES] = jnp.where(lo_q, halves[0], halves[1])
    for g in range(KVH):
        kvc[g] = buf["k2"][slot, g, tm - WIN:tm, :]
        kvc[KVH + g] = buf["v2"][slot, g, tm - WIN:tm, :]

    mixed = (buf["sgm"][slot] * ym_s[...] + buf["sga"][slot] * ya_s[...]).astype(BF16)
    r = jnp.concatenate([_dot(mixed, w["wout_a"][...]), _dot(mixed, w["wout_b"][...])], axis=1)
    x1_ref[0, row0:row0 + tm, :] = x1_ref[0, row0:row0 + tm, :] + r


def _mixer_kernel(*refs, tm, lc, npb, tps):
    nw = len(_MIXER_W_NAMES)
    x0_ref = refs[0]
    xt_refs = refs[1:1 + tps]
    w = dict(zip(_MIXER_W_NAMES, refs[1 + tps:1 + tps + nw]))
    x1_ref, c_ref, n_ref, m_ref, conv_ref, kp_ref, vp_ref = refs[1 + tps + nw:8 + tps + nw]
    scratch = refs[8 + tps + nw:]
    buf = dict(zip(_PROJ_NAMES, scratch[:len(_PROJ_NAMES)]))
    ctail, kvc, ym_s, ya_s = scratch[len(_PROJ_NAMES):]
    st = (c_ref, n_ref, m_ref, ctail, kvc, ym_s, ya_s)
    k = pl.program_id(0)
    first = (tps * k) % npb == 0

    @pl.when(k == 0)
    def _prologue():
        _mixer_project(x0_ref[0], 0, w, buf, tm=tm, lc=lc)

    @pl.when(first)
    def _init():
        c_ref[...] = jnp.zeros_like(c_ref)
        n_ref[...] = jnp.zeros_like(n_ref)
        m_ref[...] = jnp.zeros_like(m_ref)
        ctail[...] = jnp.zeros_like(ctail)
        kvc[...] = jnp.zeros_like(kvc)

    for i in range(tps):
        _mixer_project(xt_refs[i][0], (i + 1) % 2, w, buf, tm=tm, lc=lc)
        _mixer_consume(i % 2, first if i == 0 else None, w, buf, st, x1_ref, i * tm, tm=tm, lc=lc)

    @pl.when((tps * k + tps - 1) % npb == npb - 1)
    def _state_out():
        conv_ref[0] = ctail[SUBLANES - (CONV_W - 1):SUBLANES, :]
        kp_ref[0] = buf["kvl"][(tps - 1) % 2, :, 0:LANES]
        vp_ref[0] = buf["kvl"][(tps - 1) % 2, :, LANES:2 * LANES]


def _prompt_mixer(x, wts, *, tm=256, lc=128, tps=4):
    B, S, _ = x.shape
    npb = S // tm
    assert S % tm == 0 and tps % 2 == 0 and npb % tps == 0 and tm % lc == 0 and tm % WIN == 0
    nt = B * npb
    hpb = npb // tps
    cs = _const_spec

    def tile_map(off):
        def imap(k):
            t = jnp.minimum(tps * k + off, nt - 1)
            return (t // npb, t % npb, 0)
        return imap

    in_specs = [
        pl.BlockSpec((1, tm, D), lambda k: (0, 0, 0), pipeline_mode=pl.Buffered(1)),
        *[pl.BlockSpec((1, tm, D), tile_map(i + 1)) for i in range(tps)],
        cs((1, D)), cs((1, D)),
        cs((D, W1_W)), cs((D, W2_W)), cs((D, W3_W)), cs((SUBLANES, D)), cs((SUBLANES, 1)),
        cs((CONV_W, QK_CH)), cs((1, QK_CH)), cs((1, D)),
        pl.BlockSpec(memory_space=pltpu.SMEM),
        cs((D, W_HALF)), cs((D, W_HALF)),
    ]
    out_shape = (
        jax.ShapeDtypeStruct((B, S, D), F32),
        jax.ShapeDtypeStruct((B, MH, DK, DV), F32),
        jax.ShapeDtypeStruct((B, MH, DK), F32),
        jax.ShapeDtypeStruct((B, SUBLANES, LANES), F32),
        jax.ShapeDtypeStruct((B, CONV_W - 1, QK_CH), F32),
        jax.ShapeDtypeStruct((B, WIN, KVH * HD), F32),
        jax.ShapeDtypeStruct((B, WIN, KVH * HD), F32),
    )
    out_specs = (
        pl.BlockSpec((1, tps * tm, D), lambda k: (k // hpb, k % hpb, 0)),
        pl.BlockSpec((1, MH, DK, DV), lambda k: (k // hpb, 0, 0, 0)),
        pl.BlockSpec((1, MH, DK), lambda k: (k // hpb, 0, 0)),
        pl.BlockSpec((1, SUBLANES, LANES), lambda k: (k // hpb, 0, 0)),
        pl.BlockSpec((1, CONV_W - 1, QK_CH), lambda k: (k // hpb, 0, 0)),
        pl.BlockSpec((1, WIN, KVH * HD), lambda k: (k // hpb, 0, 0)),
        pl.BlockSpec((1, WIN, KVH * HD), lambda k: (k // hpb, 0, 0)),
    )
    proj_scratch = {
        "xn": pltpu.VMEM((2, tm, D), F32), "zqk": pltpu.VMEM((2, tm, QK_CH), F32),
        "vm": pltpu.VMEM((2, tm, D), BF16), "smo": pltpu.VMEM((2, tm, D), F32),
        "aq": pltpu.VMEM((2, tm // WIN, KVH, GROUP * WIN, LANES), BF16),
        "k2": pltpu.VMEM((2, KVH, tm, LANES), BF16), "v2": pltpu.VMEM((2, KVH, tm, LANES), BF16),
        "kvl": pltpu.VMEM((2, WIN, 2 * LANES), F32),
        "sgm": pltpu.VMEM((2, tm, D), F32), "sga": pltpu.VMEM((2, tm, D), F32),
        "gcol": pltpu.VMEM((2, tm, LANES), F32), "grow": pltpu.VMEM((2, SUBLANES, tm), F32),
        "brow": pltpu.VMEM((2, SUBLANES, tm), F32),
    }
    scratch = [proj_scratch[n] for n in _PROJ_NAMES] + [
        pltpu.VMEM((SUBLANES, QK_CH), F32),
        pltpu.VMEM((2 * KVH, WIN, LANES), BF16),
        pltpu.VMEM((tm, D), F32),
        pltpu.VMEM((tm, D), F32),
    ]
    return pl.pallas_call(
        functools.partial(_mixer_kernel, tm=tm, lc=lc, npb=npb, tps=tps),
        grid=(nt // tps,), in_specs=in_specs, out_specs=out_specs, out_shape=out_shape,
        scratch_shapes=scratch, name="prompt_mixer",
        compiler_params=pltpu.CompilerParams(dimension_semantics=("arbitrary",), vmem_limit_bytes=VMEM_LIMIT),
    )(*([x] * (tps + 1)), wts["ln_in_g"], wts["ln_in_b"], wts["w1"], wts["w2"], wts["w3"], wts["w_ift"], wts["b_ift"],
      wts["conv_w"], wts["conv_b"], wts["m_norm_g"], wts["sinks"], wts["w_out_a"], wts["w_out_b"])


FFN_CHUNKS = ((0, 6 * MXU_DIM), (6 * MXU_DIM, DFF))


def _ffn_kernel(x_ref, p_ref, l1g_ref, l1b_ref, wgu_ref, wd_ref, wpg_ref, wp_ref, g_ref, b_ref, o_ref, *, sub):
    for r0 in range(0, x_ref.shape[0], sub):
        x = _layer_norm(x_ref[r0:r0 + sub, :], l1g_ref[...], l1b_ref[...])
        xb = x.astype(BF16)
        acc = ALPHA * x + (_sigmoid(_dot(xb, wpg_ref[...]))
                           * _dot(p_ref[r0:r0 + sub, :].astype(BF16), wp_ref[...]))
        for c0, c1 in FFN_CHUNKS:
            g = _dot(xb, wgu_ref[:, c0:c1])
            u = _dot(xb, wgu_ref[:, DFF + c0:DFF + c1])
            hcat = (g * _sigmoid(g) * u).astype(BF16)
            acc = acc + _dot(hcat, wd_ref[c0:c1, :])
        o_ref[r0:r0 + sub, :] = _layer_norm(acc, g_ref[...], b_ref[...])


def _ffn(x, p, wts, *, tm, sub):
    N = x.shape[0]
    assert N % tm == 0 and tm % sub == 0
    cs = _const_spec
    return pl.pallas_call(
        functools.partial(_ffn_kernel, sub=sub),
        grid=(N // tm,),
        in_specs=[pl.BlockSpec((tm, D), lambda i: (i, 0)), pl.BlockSpec((tm, PD), lambda i: (i, 0)),
                  cs((1, D)), cs((1, D)),
                  cs((D, 2 * DFF)), cs((DFF, D)), cs((D, D)), cs((PD, D)), cs((1, D)), cs((1, D))],
        out_specs=pl.BlockSpec((tm, D), lambda i: (i, 0)),
        out_shape=jax.ShapeDtypeStruct((N, D), F32),
        name="ffn",
        compiler_params=pltpu.CompilerParams(dimension_semantics=("arbitrary",), vmem_limit_bytes=VMEM_LIMIT),
    )(x, p, wts["ln1_g"], wts["ln1_b"], wts["w_gu"], wts["w_d"], wts["w_pg"], wts["w_p"],
      wts["ln2_g"], wts["ln2_b"])


Z_V = 0
Z_MO = Z_V + D
Z_GM = Z_MO + D
Z_GA = Z_GM + D
Z_XN = Z_GA + D
Z_Q = Z_XN + D
Z_KW = Z_Q + MH * DK
Z_SC = Z_KW + MH * DK
Z_KN = Z_SC + LANES
Z_VN = Z_KN + LANES
Z_W = Z_VN + LANES


def _sample_pre_kernel(x_ref, lng_ref, lnb_ref, w1_ref, w2_ref, w3_ref, wif_ref, bif_ref,
                       cw_ref, cb_ref, cs_ref, n_ref, m_ref,
                       z_ref, qs_ref, convo_ref, no_ref, mo_ref, *, nb, tb):
    w = {"w1": w1_ref, "w2": w2_ref, "w3": w3_ref}
    xn = _layer_norm(x_ref[...], lng_ref[...], lnb_ref[...])
    xb = xn.astype(BF16)
    zqk = _proj(xb, w, "qk")
    cw = cw_ref[...]
    y = (cw[3:4] * zqk + cw[2:3] * cs_ref[:, 2 * QK_CH:3 * QK_CH] + cw[1:2] * cs_ref[:, QK_CH:2 * QK_CH]
         + cw[0:1] * cs_ref[:, 0:QK_CH] + cb_ref[...])
    qk = y * _sigmoid(y)
    convo_ref[:, 0:2 * QK_CH] = cs_ref[:, QK_CH:3 * QK_CH]
    convo_ref[:, 2 * QK_CH:3 * QK_CH] = zqk

    gcol = _dot(xb, wif_ref[...]) + bif_ref[...]
    ig = gcol[:, 0:MH]
    m_inter = _log_sigmoid(gcol[:, MH:2 * MH]) + m_ref[...]
    m_t = jnp.maximum(m_inter, ig)
    inter = jnp.exp(m_inter - m_t)
    wi = jnp.exp(ig - m_t)
    enm = jnp.exp(-m_t)
    mo_ref[...] = m_t

    z_ref[:, Z_V:Z_V + D] = _proj(xb, w, "mv")
    z_ref[:, Z_MO:Z_MO + W_HALF] = _proj(xb, w, "mo_lo")
    z_ref[:, Z_MO + W_HALF:Z_MO + D] = _proj(xb, w, "mo_hi")
    z_ref[:, Z_GM:Z_GM + D] = _proj(xb, w, "gm")
    z_ref[:, Z_GA:Z_GA + D] = _proj(xb, w, "ga")
    z_ref[:, Z_XN:Z_XN + D] = xn
    z_ref[:, Z_Q:Z_Q + MH * DK] = qk[:, 0:MH * DK]
    sc = jnp.zeros((nb, LANES), F32)
    lane = lax.broadcasted_iota(jnp.int32, (nb, LANES), 1)
    for h in range(MH):
        qh = qk[:, h * DK:(h + 1) * DK]
        kh = qk[:, MH * DK + h * DK:MH * DK + (h + 1) * DK] * (DK ** -0.5)
        nh = n_ref[:, h * DK:(h + 1) * DK]
        wv = wi[:, h:h + 1] * jnp.sum(qh * kh, -1, keepdims=True)
        den = wv + inter[:, h:h + 1] * jnp.sum(qh * nh, -1, keepdims=True)
        denom = jnp.maximum(jnp.abs(den), enm[:, h:h + 1])
        kw = wi[:, h:h + 1] * kh
        z_ref[:, Z_KW + h * DK:Z_KW + (h + 1) * DK] = kw
        no_ref[:, h * DK:(h + 1) * DK] = inter[:, h:h + 1] * nh + kw
        sc = jnp.where(lane == h, inter[:, h:h + 1], sc)
        sc = jnp.where(lane == MH + h, wv, sc)
        sc = jnp.where(lane == 2 * MH + h, denom, sc)
    z_ref[:, Z_SC:Z_SC + LANES] = sc

    akv = _proj(xb, w, "akv")
    z_ref[:, Z_KN:Z_KN + LANES] = akv[:, 0:LANES]
    z_ref[:, Z_VN:Z_VN + LANES] = akv[:, LANES:2 * LANES]
    aq = _proj(xb, w, "aq") * (HD ** -0.5)
    lo = lax.broadcasted_iota(jnp.int32, (nb, LANES), 1) < HD
    for h in range(AH):
        pair = aq[:, (h // 2) * LANES:(h // 2 + 1) * LANES]
        qm = jnp.where(lo if h % 2 == 0 else ~lo, pair, 0.0)
        for t in range(nb // tb):
            qs_ref[(t * AH + h) * tb:(t * AH + h + 1) * tb, :] = qm[t * tb:(t + 1) * tb, :]


def _sample_state_kernel(z_ref, qs_ref, sink_ref, c_ref, ck_ref, cv_ref,
                         co_ref, qc_ref, os_ref, ko_ref, vo_ref, *, tb):
    lo = lax.broadcasted_iota(jnp.int32, (WIN, LANES), 1) < HD
    row = lax.broadcasted_iota(jnp.int32, (WIN, LANES), 0)
    rowt = lax.broadcasted_iota(jnp.int32, (tb, DK), 0)
    sinks = sink_ref[...]
    q_all = z_ref[:, Z_Q:Z_Q + MH * DK]
    kw_all = z_ref[:, Z_KW:Z_KW + MH * DK]
    v_all = z_ref[:, Z_V:Z_V + D]
    sc = z_ref[:, Z_SC:Z_SC + LANES]
    kn_all = z_ref[:, Z_KN:Z_KN + LANES]
    vn_all = z_ref[:, Z_VN:Z_VN + LANES]
    kn2_all = _dup_halves(kn_all, lo[0:tb])
    vn2_all = _dup_halves(vn_all, lo[0:tb])
    for i in range(tb):
        for h in range(MH):
            cb = c_ref[i, h]
            qc = _dot(q_all[:, h * DK:(h + 1) * DK].astype(BF16), cb.astype(BF16))
            qc_ref[i:i + 1, h * DV:(h + 1) * DV] = qc[i:i + 1]
            kw_i = jnp.where(rowt == i, kw_all[:, h * DK:(h + 1) * DK], 0.0)
            outer = _dot_tn(kw_i, v_all[:, h * DV:(h + 1) * DV])
            co_ref[i, h] = sc[i:i + 1, h:h + 1] * cb + outer
        kc = ck_ref[i]
        vc = cv_ref[i]
        kn = kn_all[i:i + 1]
        vn = vn_all[i:i + 1]
        k2 = _dup_halves(kc, lo)
        v2 = _dup_halves(vc, lo)
        kn2 = [a[i:i + 1] for a in kn2_all]
        vn2 = [a[i:i + 1] for a in vn2_all]
        qrow = qs_ref[pl.ds(i, AH, stride=tb), :]
        outs = []
        for g in range(KVH):
            qg = qrow[g * GROUP:(g + 1) * GROUP]
            s = _dot_nt(qg.astype(BF16), k2[g].astype(BF16))
            s_self = jnp.sum(qg * kn2[g], -1, keepdims=True)
            sg = sinks[g * GROUP:(g + 1) * GROUP]
            mx = jnp.maximum(jnp.maximum(jnp.max(s, -1, keepdims=True), s_self), sg)
            e = jnp.exp(s - mx)
            es = jnp.exp(s_self - mx)
            dsum = jnp.sum(e, -1, keepdims=True) + es + jnp.exp(sg - mx)
            outs.append((_dot(e.astype(BF16), v2[g].astype(BF16)) + es * vn2[g]) / dsum)
        os_ref[pl.ds(i, AH, stride=tb), :] = jnp.concatenate(outs, axis=0)
        ko_ref[i] = jnp.where(row == WIN - 1, kn, pltpu.roll(kc, WIN - 1, 0))
        vo_ref[i] = jnp.where(row == WIN - 1, vn, pltpu.roll(vc, WIN - 1, 0))


def _sample_post_kernel(z_ref, qc_ref, os_ref, mng_ref, wouta_ref, woutb_ref, x1_ref, *, nb, tb):
    sc = z_ref[:, Z_SC:Z_SC + LANES]
    mng = mng_ref[...]
    lo = lax.broadcasted_iota(jnp.int32, (nb, LANES), 1) < HD
    ym = []
    for h in range(MH):
        v = z_ref[:, Z_V + h * DV:Z_V + (h + 1) * DV]
        num = sc[:, MH + h:MH + h + 1] * v + sc[:, h:h + 1] * qc_ref[:, h * DV:(h + 1) * DV]
        hh = num / sc[:, 2 * MH + h:2 * MH + h + 1]
        hn = hh * lax.rsqrt(jnp.mean(hh * hh, -1, keepdims=True) + RMS_EPS)
        ym.append(_sigmoid(z_ref[:, Z_MO + h * DV:Z_MO + (h + 1) * DV]) * hn * mng[:, h * DV:(h + 1) * DV])
    ym = jnp.concatenate(ym, axis=1)
    ya = []
    for pj in range(AH // 2):
        halves = []
        for par in range(2):
            h = 2 * pj + par
            halves.append(jnp.concatenate(
                [os_ref[(t * AH + h) * tb:(t * AH + h + 1) * tb, :] for t in range(nb // tb)], axis=0))
        ya.append(jnp.where(lo, halves[0], halves[1]))
    ya = jnp.concatenate(ya, axis=1)
    mixed = _sigmoid(z_ref[:, Z_GM:Z_GM + D]) * ym + _sigmoid(z_ref[:, Z_GA:Z_GA + D]) * ya
    mixed = mixed.astype(BF16)
    r = jnp.concatenate([_dot(mixed, wouta_ref[...]), _dot(mixed, woutb_ref[...])], axis=1)
    x1_ref[...] = ALPHA * z_ref[:, Z_XN:Z_XN + D] + r


def _sample_mixer(x, c0, n0, m0, conv0, k0, v0, wts, *, tb=8):
    nb = x.shape[0]
    assert nb % tb == 0
    vmem = pltpu.CompilerParams(dimension_semantics=("arbitrary",), vmem_limit_bytes=VMEM_LIMIT)
    full = lambda shape: pl.BlockSpec(shape, lambda i: (0,) * len(shape))
    pre_in = [x, wts["ln_in_g"], wts["ln_in_b"], wts["w1"], wts["w2"], wts["w3"], wts["w_if"], wts["b_if"],
              wts["conv_w"], wts["conv_b"], conv0, n0, m0]
    z, qs, conv_new, n_new, m_new = pl.pallas_call(
        functools.partial(_sample_pre_kernel, nb=nb, tb=tb),
        grid=(1,),
        in_specs=[full(a.shape) for a in pre_in],
        out_specs=(full((nb, Z_W)), full((nb * AH, LANES)), full((nb, (CONV_W - 1) * QK_CH)),
                   full((nb, MH * DK)), full((nb, MH))),
        out_shape=(jax.ShapeDtypeStruct((nb, Z_W), F32), jax.ShapeDtypeStruct((nb * AH, LANES), F32),
                   jax.ShapeDtypeStruct((nb, (CONV_W - 1) * QK_CH), F32),
                   jax.ShapeDtypeStruct((nb, MH * DK), F32), jax.ShapeDtypeStruct((nb, MH), F32)),
        name="sample_pre", compiler_params=vmem,
    )(*pre_in)

    c_new, qc, os_, k_new, v_new = pl.pallas_call(
        functools.partial(_sample_state_kernel, tb=tb),
        grid=(nb // tb,),
        in_specs=[pl.BlockSpec((tb, Z_W), lambda i: (i, 0)),
                  pl.BlockSpec((tb * AH, LANES), lambda i: (i, 0)),
                  pl.BlockSpec((AH, 1), lambda i: (0, 0)),
                  pl.BlockSpec((tb, MH, DK, DV), lambda i: (i, 0, 0, 0)),
                  pl.BlockSpec((tb, WIN, LANES), lambda i: (i, 0, 0)),
                  pl.BlockSpec((tb, WIN, LANES), lambda i: (i, 0, 0))],
        out_specs=(pl.BlockSpec((tb, MH, DK, DV), lambda i: (i, 0, 0, 0)),
                   pl.BlockSpec((tb, D), lambda i: (i, 0)),
                   pl.BlockSpec((tb * AH, LANES), lambda i: (i, 0)),
                   pl.BlockSpec((tb, WIN, LANES), lambda i: (i, 0, 0)),
                   pl.BlockSpec((tb, WIN, LANES), lambda i: (i, 0, 0))),
        out_shape=(jax.ShapeDtypeStruct((nb, MH, DK, DV), F32), jax.ShapeDtypeStruct((nb, D), F32),
                   jax.ShapeDtypeStruct((nb * AH, LANES), F32),
                   jax.ShapeDtypeStruct((nb, WIN, LANES), F32), jax.ShapeDtypeStruct((nb, WIN, LANES), F32)),
        name="sample_state", compiler_params=vmem,
    )(z, qs, wts["sinks_col"], c0, k0, v0)

    post_in = [z, qc, os_, wts["m_norm_g"], wts["w_out_a"], wts["w_out_b"]]
    x1 = pl.pallas_call(
        functools.partial(_sample_post_kernel, nb=nb, tb=tb),
        grid=(1,),
        in_specs=[full(a.shape) for a in post_in],
        out_specs=full((nb, D)),
        out_shape=jax.ShapeDtypeStruct((nb, D), F32),
        name="sample_post", compiler_params=vmem,
    )(*post_in)
    return x1, c_new, n_new, m_new, conv_new, k_new, v_new


def _prep_weights(ln_in_g, ln_in_b, w_in, b_igate, b_fgate, conv_w, conv_b, m_norm_g, attn_sinks, w_out,
                  ln1_g, ln1_b, w_gate_up, w_down, ln2_g, ln2_b, w_ple, w_ple_gate):
    w = w_in[0]
    row = lambda a: a.reshape(1, -1).astype(F32)
    w_if = jnp.pad(w[:, O_MI:O_MO], ((0, 0), (0, LANES - 2 * MH)))
    b_if = jnp.pad(jnp.concatenate([b_igate[0], b_fgate[0]]), (0, LANES - 2 * MH))
    cols = lambda a, b: w[:, a:b].astype(BF16)
    w1 = jnp.concatenate([cols(O_QK, O_MI), cols(O_MO, O_MO + W_HALF)], axis=1)
    w2 = jnp.concatenate([cols(O_MO + W_HALF, O_AK), cols(O_GM, O_GA)], axis=1)
    w3 = jnp.concatenate([cols(O_GA, IN_WIDTH), cols(O_AK, O_GM)], axis=1)
    wo = w_out[0].astype(BF16)
    return {
        "ln_in_g": row(ln_in_g), "ln_in_b": row(ln_in_b),
        "w1": w1, "w2": w2, "w3": w3,
        "w_if": w_if.astype(BF16), "w_ift": w[:, O_MI:O_MO].T.astype(BF16),
        "b_if": b_if.reshape(1, LANES), "b_ift": b_if[:2 * MH].reshape(2 * MH, 1),
        "conv_w": conv_w[0], "conv_b": row(conv_b[0]), "m_norm_g": row(m_norm_g[0]),
        "sinks": attn_sinks[0], "sinks_col": attn_sinks[0].reshape(AH, 1),
        "w_out_a": wo[:, :W_HALF], "w_out_b": wo[:, W_HALF:], "ln1_g": row(ln1_g[0]), "ln1_b": row(ln1_b[0]),
        "w_gu": w_gate_up[0].astype(BF16),
        "w_d": w_down[0].astype(BF16), "w_pg": w_ple_gate[0].astype(BF16), "w_p": w_ple[0].astype(BF16),
        "ln2_g": row(ln2_g[0]), "ln2_b": row(ln2_b[0]),
    }


def _prompt_path(x, p, wts, *, tm=256, lc=128, tmf=1024):
    B, S, _ = x.shape
    x1, c, n, m, conv, k, v = _prompt_mixer(x, wts, tm=tm, lc=lc)
    y = _ffn(x1.reshape(B * S, D), p.reshape(B * S, PD), wts, tm=tmf, sub=tmf // 2).reshape(B, S, D)
    return (y, c[None], n[None], m[None, :, :MH, 0], conv[None],
            k.reshape(1, B, WIN, KVH, HD), v.reshape(1, B, WIN, KVH, HD))


def _sample_path(x, p, c0, n0, m0, conv0, k0, v0, wts, *, tb=8):
    nb = x.shape[0]
    x1, c, n, m, conv, k, v = _sample_mixer(
        x.reshape(nb, D), c0[0], n0[0].reshape(nb, MH * DK), m0[0],
        conv0[0].reshape(nb, (CONV_W - 1) * QK_CH), k0[0].reshape(nb, WIN, LANES),
        v0[0].reshape(nb, WIN, LANES), wts, tb=tb)
    y = _ffn(x1, p.reshape(nb, PD), wts, tm=nb, sub=nb).reshape(nb, 1, D)
    return (y, c[None], n.reshape(1, nb, MH, DK), m[None], conv.reshape(1, nb, CONV_W - 1, QK_CH),
            k.reshape(1, nb, WIN, KVH, HD), v.reshape(1, nb, WIN, KVH, HD))


def kernel(x_prompt, x_sample, state_mlstm_C, state_mlstm_n, state_mlstm_m, state_conv, cache_win_k, cache_win_v, p_prompt, p_sample, ln_in_g, ln_in_b, w_in, b_igate, b_fgate, conv_w, conv_b, m_norm_g, attn_sinks, w_out, ln1_g, ln1_b, w_gate_up, w_down, ln2_g, ln2_b, w_ple, w_ple_gate):
    wts = _prep_weights(ln_in_g, ln_in_b, w_in, b_igate, b_fgate, conv_w, conv_b, m_norm_g, attn_sinks,
                        w_out, ln1_g, ln1_b, w_gate_up, w_down, ln2_g, ln2_b, w_ple, w_ple_gate)
    yp, c_p, n_p, m_p, conv_p, k_p, v_p = _prompt_path(x_prompt, p_prompt[0], wts)
    ys, c_s, n_s, m_s, conv_s, k_s, v_s = _sample_path(
        x_sample, p_sample[0], state_mlstm_C, state_mlstm_n, state_mlstm_m, state_conv,
        cache_win_k, cache_win_v, wts)
    return (yp, ys, c_p, n_p, m_p, conv_p, k_p, v_p, c_s, n_s, m_s, conv_s, k_s, v_s)
```

```python
import functools

import jax
import jax.numpy as jnp
from jax import lax
from jax.experimental import pallas as pl
from jax.experimental.pallas import tpu as pltpu

F32 = jnp.float32
BF16 = jnp.bfloat16

D = 1024
MH = 4
DV = D // MH
DK = DV // 2
CONV_W = 4
QK_CH = 2 * MH * DK
AH = 16
KVH = 2
HD = D // AH
GROUP = AH // KVH
WIN = 128
DFF = 2816
PD = 256
LN_EPS = 1e-5
RMS_EPS = 1e-6
ALPHA = 2.0 ** 0.25
LOG2E = 1.4426950408889634
NEG_INF = float("-inf")

LANES = 128
SUBLANES = 8
MXU_DIM = 256
VMEM_LIMIT = 56 * 1024 * 1024

O_QK = 0
O_MV = O_QK + QK_CH
O_MI = O_MV + D
O_MF = O_MI + MH
O_MO = O_MF + MH
O_AQ = O_MO + D
O_AK = O_AQ + D
O_AV = O_AK + KVH * HD
O_GM = O_AV + KVH * HD
O_GA = O_GM + D
IN_WIDTH = O_GA + D

W_HALF = D // 2
W1_W = 2 * D + W_HALF
W2_W = 2 * D + W_HALF
W3_W = D + 2 * KVH * HD
_WCOL = {"qk": ("w1", 0, D), "mv": ("w1", D, 2 * D), "mo_lo": ("w1", 2 * D, W1_W),
         "mo_hi": ("w2", 0, W_HALF), "aq": ("w2", W_HALF, W_HALF + D), "gm": ("w2", W_HALF + D, W2_W),
         "ga": ("w3", 0, D), "akv": ("w3", D, W3_W)}


def _dot(a, b):
    return jnp.dot(a, b, preferred_element_type=F32)


def _dot_nt(a, b):
    return lax.dot_general(a, b, (((1,), (1,)), ((), ())), preferred_element_type=F32)


def _dot_tn(a, b):
    return lax.dot_general(a, b, (((0,), (0,)), ((), ())), preferred_element_type=F32)


def _proj(xb, w, name):
    blk, a, b = _WCOL[name]
    return _dot(xb, w[blk][:, a:b])


def _layer_norm(x, g, b):
    mu = jnp.mean(x, -1, keepdims=True)
    xc = x - mu
    var = jnp.mean(xc * xc, -1, keepdims=True)
    return xc * lax.rsqrt(var + LN_EPS) * g + b


def _sigmoid(x):
    return 1.0 / (1.0 + jnp.exp(-x))


def _log_sigmoid(x):
    return jnp.minimum(x, 0.0) - jnp.log(1.0 + jnp.exp(-jnp.abs(x)))


def _const_spec(shape):
    zeros = (0,) * len(shape)
    return pl.BlockSpec(shape, lambda i: zeros, pipeline_mode=pl.Buffered(1))


def _dup_halves(x, lo):
    xr = pltpu.roll(x, HD, 1)
    return jnp.where(lo, x, xr), jnp.where(lo, xr, x)


_PROJ_NAMES = ("xn", "zqk", "vm", "smo", "aq", "k2", "v2", "kvl", "sgm", "sga", "gcol", "grow", "brow")
_MIXER_W_NAMES = ("lng", "lnb", "w1", "w2", "w3", "wift", "bift", "cw", "cb", "mng", "sink", "wout_a", "wout_b")


def _mixer_project(x, slot, w, buf, *, tm, lc):
    xn = _layer_norm(x, w["lng"][...], w["lnb"][...])
    xb = xn.astype(BF16)
    buf["xn"][slot] = xn
    buf["zqk"][slot] = _proj(xb, w, "qk")
    buf["vm"][slot] = _proj(xb, w, "mv").astype(BF16)
    mo = jnp.concatenate([_proj(xb, w, "mo_lo"), _proj(xb, w, "mo_hi")], axis=1)
    buf["smo"][slot] = _sigmoid(mo) * w["mng"][...]
    aq = _proj(xb, w, "aq") * (HD ** -0.5 * LOG2E)
    lo_d = (lax.broadcasted_iota(jnp.int32, (tm, D), 1) & (LANES - 1)) < HD
    aqm = (jnp.where(lo_d, aq, 0.0).astype(BF16), jnp.where(lo_d, 0.0, aq).astype(BF16))
    for j in range(tm // WIN):
        for h in range(AH):
            buf["aq"][slot, j, h // GROUP, (h % GROUP) * WIN:(h % GROUP + 1) * WIN, :] = (
                aqm[h % 2][j * WIN:(j + 1) * WIN, (h // 2) * LANES:(h // 2 + 1) * LANES])
    akv = _proj(xb, w, "akv")
    lo = lax.broadcasted_iota(jnp.int32, (tm, LANES), 1) < HD
    k2 = _dup_halves(akv[:, 0:LANES], lo)
    v2 = _dup_halves(akv[:, LANES:2 * LANES], lo)
    for g in range(KVH):
        buf["k2"][slot, g] = k2[g].astype(BF16)
        buf["v2"][slot, g] = v2[g].astype(BF16)
    buf["kvl"][slot] = akv[tm - WIN:tm, :]
    buf["sgm"][slot] = _sigmoid(_proj(xb, w, "gm"))
    buf["sga"][slot] = _sigmoid(_proj(xb, w, "ga"))
    grow = _dot_nt(w["wift"][...], xb) + w["bift"][...]
    buf["grow"][slot] = grow
    lsrow = _log_sigmoid(grow)
    lane = lax.broadcasted_iota(jnp.int32, (SUBLANES, lc), 1)
    pad = jnp.zeros((lc - 2 * SUBLANES, lc), F32)
    for c in range(tm // lc):
        r0 = c * lc
        seg = lsrow[:, r0:r0 + lc]
        sh = 1
        while sh < lc:
            seg = seg + jnp.where(lane >= sh, pltpu.roll(seg, sh, 1), 0.0)
            sh *= 2
        buf["brow"][slot, :, r0:r0 + lc] = seg
        buf["gcol"][slot, r0:r0 + lc, :] = jnp.concatenate([grow[:, r0:r0 + lc], seg, pad], axis=0).T


def _mixer_consume(slot, first, w, buf, st, x1_ref, row0, *, tm, lc):
    c_ref, n_ref, m_ref, ctail, kvc, ym_s, ya_s = st
    x1_ref[0, row0:row0 + tm, :] = ALPHA * buf["xn"][slot]
    z = buf["zqk"][slot]
    tail = ctail[...]
    ctail[...] = z[tm - SUBLANES:tm, :]
    cw = w["cw"][...]
    rows8 = lax.broadcasted_iota(jnp.int32, (SUBLANES, QK_CH), 0)
    y = cw[CONV_W - 1:CONV_W] * z + w["cb"][...]
    for sh in range(1, CONV_W):
        zr = pltpu.roll(z, sh, 0)
        head = jnp.where(rows8 < sh, pltpu.roll(tail, sh, 0), zr[0:SUBLANES])
        zr = jnp.concatenate([head, zr[SUBLANES:]], axis=0)
        y = y + cw[CONV_W - 1 - sh:CONV_W - sh] * zr
    qk = y * _sigmoid(y)

    ri = lax.broadcasted_iota(jnp.int32, (lc, lc), 0)
    ci = lax.broadcasted_iota(jnp.int32, (lc, lc), 1)
    causal = ci <= ri
    for c in range(tm // lc):
        r0 = c * lc
        for h in range(MH):
            qh = qk[r0:r0 + lc, h * DK:(h + 1) * DK]
            kh = qk[r0:r0 + lc, MH * DK + h * DK:MH * DK + (h + 1) * DK] * (DK ** -0.5)
            vh = buf["vm"][slot, r0:r0 + lc, h * DV:(h + 1) * DV]
            qb = qh.astype(BF16)
            bc = buf["gcol"][slot, r0:r0 + lc, SUBLANES + MH + h:SUBLANES + MH + h + 1]
            br = buf["brow"][slot, MH + h:MH + h + 1, r0:r0 + lc]
            li_c = buf["gcol"][slot, r0:r0 + lc, h:h + 1]
            li_r = buf["grow"][slot, h:h + 1, r0:r0 + lc]
            m_prev = m_ref[0, h:h + 1, 0:1]
            c_prev = c_ref[0, h]
            n_prev = n_ref[0, h:h + 1, :]

            dmat = jnp.where(causal, bc - br + li_r, NEG_INF)
            m_inter = bc + m_prev
            m_t = jnp.maximum(m_inter, jnp.max(dmat, -1, keepdims=True))
            wgt = jnp.exp(dmat - m_t) * _dot_nt(qb, kh.astype(BF16))
            inter = jnp.exp(m_inter - m_t)
            num = _dot(wgt.astype(BF16), vh) + inter * _dot(qb, c_prev.astype(BF16))
            den = jnp.sum(wgt, -1, keepdims=True) + inter * jnp.sum(qh * n_prev, -1, keepdims=True)
            denom = jnp.maximum(jnp.abs(den), jnp.exp(-m_t))
            hh = num / denom
            hn = hh * lax.rsqrt(jnp.mean(hh * hh, -1, keepdims=True) + RMS_EPS)
            ym_s[r0:r0 + lc, h * DV:(h + 1) * DV] = buf["smo"][slot, r0:r0 + lc, h * DV:(h + 1) * DV] * hn

            m_new = m_t[lc - 1:lc, :]
            bc_last = bc[lc - 1:lc, :]
            decay = jnp.exp(bc_last + m_prev - m_new)
            ks = kh * jnp.exp(bc_last - bc + li_c - m_new)
            c_ref[0, h] = decay * c_prev + _dot_tn(ks.astype(BF16), vh)
            n_ref[0, h:h + 1, :] = decay * n_prev + jnp.sum(ks, 0, keepdims=True)
            m_ref[0, h:h + 1, :] = jnp.broadcast_to(m_new, (1, LANES))

    lo_q = lax.broadcasted_iota(jnp.int32, (WIN, LANES), 1) < HD
    qi = lax.broadcasted_iota(jnp.int32, (WIN, 2 * WIN), 0)
    kj = lax.broadcasted_iota(jnp.int32, (WIN, 2 * WIN), 1)
    band = (kj >= qi) & (kj <= qi + WIN)
    ones_v = jnp.ones((2 * WIN, LANES), BF16)
    for j in range(tm // WIN):
        r0 = j * WIN
        if j == 0 and first is not None:
            mask = band & (kj >= jnp.where(first, WIN, 0))
        else:
            mask = band
        for g in range(KVH):
            if j == 0:
                kprev, vprev = kvc[g], kvc[KVH + g]
            else:
                kprev, vprev = buf["k2"][slot, g, r0 - WIN:r0, :], buf["v2"][slot, g, r0 - WIN:r0, :]
            kcat = jnp.concatenate([kprev, buf["k2"][slot, g, r0:r0 + WIN, :]], axis=0)
            vcat = jnp.concatenate([vprev, buf["v2"][slot, g, r0:r0 + WIN, :]], axis=0)
            sc_all = _dot_nt(buf["aq"][slot, j, g], kcat)
            es, ds = [], []
            for hh in range(GROUP):
                sink = w["sink"][g * GROUP + hh] * LOG2E
                sc = jnp.where(mask, sc_all[hh * WIN:(hh + 1) * WIN, :], NEG_INF)
                mx = jnp.maximum(jnp.max(sc, -1, keepdims=True), sink)
                es.append(jnp.exp2(sc - mx).astype(BF16))
                ds.append(jnp.exp2(sink - mx))
            o_all = _dot(jnp.concatenate(es, axis=0), jnp.concatenate([vcat, ones_v], axis=1))
            for pp in range(GROUP // 2):
                pj = g * GROUP // 2 + pp
                halves = []
                for hh in (2 * pp, 2 * pp + 1):
                    oh = o_all[hh * WIN:(hh + 1) * WIN, :]
                    halves.append(oh[:, 0:LANES] / (oh[:, LANES:2 * LANES] + ds[hh]))
                ya_s[r0:r0 + WIN, pj * LANES:(pj + 1) * LANES] = jnp.where(lo_q, halves[0], halves[1])
    for g in range(KVH):
        kvc[g] = buf["k2"][slot, g, tm - WIN:tm, :]
        kvc[KVH + g] = buf["v2"][slot, g, tm - WIN:tm, :]

    mixed = (buf["sgm"][slot] * ym_s[...] + buf["sga"][slot] * ya_s[...]).astype(BF16)
    r = jnp.concatenate([_dot(mixed, w["wout_a"][...]), _dot(mixed, w["wout_b"][...])], axis=1)
    x1_ref[0, row0:row0 + tm, :] = x1_ref[0, row0:row0 + tm, :] + r


def _mixer_kernel(*refs, tm, lc, npb, tps):
    nw = len(_MIXER_W_NAMES)
    x0_ref = refs[0]
    xt_refs = refs[1:1 + tps]
    w = dict(zip(_MIXER_W_NAMES, refs[1 + tps:1 + tps + nw]))
    x1_ref, c_ref, n_ref, m_ref, conv_ref, kp_ref, vp_ref = refs[1 + tps + nw:8 + tps + nw]
    scratch = refs[8 + tps + nw:]
    buf = dict(zip(_PROJ_NAMES, scratch[:len(_PROJ_NAMES)]))
    ctail, kvc, ym_s, ya_s = scratch[len(_PROJ_NAMES):]
    st = (c_ref, n_ref, m_ref, ctail, kvc, ym_s, ya_s)
    k = pl.program_id(0)
    first = (tps * k) % npb == 0

    @pl.when(k == 0)
    def _prologue():
        _mixer_project(x0_ref[0], 0, w, buf, tm=tm, lc=lc)

    @pl.when(first)
    def _init():
        c_ref[...] = jnp.zeros_like(c_ref)
        n_ref[...] = jnp.zeros_like(n_ref)
        m_ref[...] = jnp.zeros_like(m_ref)
        ctail[...] = jnp.zeros_like(ctail)
        kvc[...] = jnp.zeros_like(kvc)

    for i in range(tps):
        _mixer_project(xt_refs[i][0], (i + 1) % 2, w, buf, tm=tm, lc=lc)
        _mixer_consume(i % 2, first if i == 0 else None, w, buf, st, x1_ref, i * tm, tm=tm, lc=lc)

    @pl.when((tps * k + tps - 1) % npb == npb - 1)
    def _state_out():
        conv_ref[0] = ctail[SUBLANES - (CONV_W - 1):SUBLANES, :]
        kp_ref[0] = buf["kvl"][(tps - 1) % 2, :, 0:LANES]
        vp_ref[0] = buf["kvl"][(tps - 1) % 2, :, LANES:2 * LANES]


def _prompt_mixer(x, wts, *, tm=256, lc=128, tps=2):
    B, S, _ = x.shape
    npb = S // tm
    assert S % tm == 0 and tps % 2 == 0 and npb % tps == 0 and tm % lc == 0 and tm % WIN == 0
    nt = B * npb
    hpb = npb // tps
    cs = _const_spec

    def tile_map(off):
        def imap(k):
            t = jnp.minimum(tps * k + off, nt - 1)
            return (t // npb, t % npb, 0)
        return imap

    in_specs = [
        pl.BlockSpec((1, tm, D), lambda k: (0, 0, 0), pipeline_mode=pl.Buffered(1)),
        *[pl.BlockSpec((1, tm, D), tile_map(i + 1)) for i in range(tps)],
        cs((1, D)), cs((1, D)),
        cs((D, W1_W)), cs((D, W2_W)), cs((D, W3_W)), cs((SUBLANES, D)), cs((SUBLANES, 1)),
        cs((CONV_W, QK_CH)), cs((1, QK_CH)), cs((1, D)),
        pl.BlockSpec(memory_space=pltpu.SMEM),
        cs((D, W_HALF)), cs((D, W_HALF)),
    ]
    out_shape = (
        jax.ShapeDtypeStruct((B, S, D), F32),
        jax.ShapeDtypeStruct((B, MH, DK, DV), F32),
        jax.ShapeDtypeStruct((B, MH, DK), F32),
        jax.ShapeDtypeStruct((B, SUBLANES, LANES), F32),
        jax.ShapeDtypeStruct((B, CONV_W - 1, QK_CH), F32),
        jax.ShapeDtypeStruct((B, WIN, KVH * HD), F32),
        jax.ShapeDtypeStruct((B, WIN, KVH * HD), F32),
    )
    out_specs = (
        pl.BlockSpec((1, tps * tm, D), lambda k: (k // hpb, k % hpb, 0)),
        pl.BlockSpec((1, MH, DK, DV), lambda k: (k // hpb, 0, 0, 0)),
        pl.BlockSpec((1, MH, DK), lambda k: (k // hpb, 0, 0)),
        pl.BlockSpec((1, SUBLANES, LANES), lambda k: (k // hpb, 0, 0)),
        pl.BlockSpec((1, CONV_W - 1, QK_CH), lambda k: (k // hpb, 0, 0)),
        pl.BlockSpec((1, WIN, KVH * HD), lambda k: (k // hpb, 0, 0)),
        pl.BlockSpec((1, WIN, KVH * HD), lambda k: (k // hpb, 0, 0)),
    )
    proj_scratch = {
        "xn": pltpu.VMEM((2, tm, D), F32), "zqk": pltpu.VMEM((2, tm, QK_CH), F32),
        "vm": pltpu.VMEM((2, tm, D), BF16), "smo": pltpu.VMEM((2, tm, D), F32),
        "aq": pltpu.VMEM((2, tm // WIN, KVH, GROUP * WIN, LANES), BF16),
        "k2": pltpu.VMEM((2, KVH, tm, LANES), BF16), "v2": pltpu.VMEM((2, KVH, tm, LANES), BF16),
        "kvl": pltpu.VMEM((2, WIN, 2 * LANES), F32),
        "sgm": pltpu.VMEM((2, tm, D), F32), "sga": pltpu.VMEM((2, tm, D), F32),
        "gcol": pltpu.VMEM((2, tm, LANES), F32), "grow": pltpu.VMEM((2, SUBLANES, tm), F32),
        "brow": pltpu.VMEM((2, SUBLANES, tm), F32),
    }
    scratch = [proj_scratch[n] for n in _PROJ_NAMES] + [
        pltpu.VMEM((SUBLANES, QK_CH), F32),
        pltpu.VMEM((2 * KVH, WIN, LANES), BF16),
        pltpu.VMEM((tm, D), F32),
        pltpu.VMEM((tm, D), F32),
    ]
    return pl.pallas_call(
        functools.partial(_mixer_kernel, tm=tm, lc=lc, npb=npb, tps=tps),
        grid=(nt // tps,), in_specs=in_specs, out_specs=out_specs, out_shape=out_shape,
        scratch_shapes=scratch, name="prompt_mixer",
        compiler_params=pltpu.CompilerParams(dimension_semantics=("arbitrary",), vmem_limit_bytes=VMEM_LIMIT),
    )(*([x] * (tps + 1)), wts["ln_in_g"], wts["ln_in_b"], wts["w1"], wts["w2"], wts["w3"], wts["w_ift"], wts["b_ift"],
      wts["conv_w"], wts["conv_b"], wts["m_norm_g"], wts["sinks"], wts["w_out_a"], wts["w_out_b"])


FFN_CHUNKS = ((0, 6 * MXU_DIM), (6 * MXU_DIM, DFF))


def _ffn_kernel(x_ref, p_ref, l1g_ref, l1b_ref, wgu_ref, wd_ref, wpg_ref, wp_ref, g_ref, b_ref, o_ref, *, sub):
    for r0 in range(0, x_ref.shape[0], sub):
        x = _layer_norm(x_ref[r0:r0 + sub, :], l1g_ref[...], l1b_ref[...])
        xb = x.astype(BF16)
        acc = ALPHA * x + (_sigmoid(_dot(xb, wpg_ref[...]))
                           * _dot(p_ref[r0:r0 + sub, :].astype(BF16), wp_ref[...]))
        for c0, c1 in FFN_CHUNKS:
            g = _dot(xb, wgu_ref[:, c0:c1])
            u = _dot(xb, wgu_ref[:, DFF + c0:DFF + c1])
            hcat = (g * _sigmoid(g) * u).astype(BF16)
            acc = acc + _dot(hcat, wd_ref[c0:c1, :])
        o_ref[r0:r0 + sub, :] = _layer_norm(acc, g_ref[...], b_ref[...])


def _ffn(x, p, wts, *, tm, sub):
    N = x.shape[0]
    assert N % tm == 0 and tm % sub == 0
    cs = _const_spec
    return pl.pallas_call(
        functools.partial(_ffn_kernel, sub=sub),
        grid=(N // tm,),
        in_specs=[pl.BlockSpec((tm, D), lambda i: (i, 0)), pl.BlockSpec((tm, PD), lambda i: (i, 0)),
                  cs((1, D)), cs((1, D)),
                  cs((D, 2 * DFF)), cs((DFF, D)), cs((D, D)), cs((PD, D)), cs((1, D)), cs((1, D))],
        out_specs=pl.BlockSpec((tm, D), lambda i: (i, 0)),
        out_shape=jax.ShapeDtypeStruct((N, D), F32),
        name="ffn",
        compiler_params=pltpu.CompilerParams(dimension_semantics=("arbitrary",), vmem_limit_bytes=VMEM_LIMIT),
    )(x, p, wts["ln1_g"], wts["ln1_b"], wts["w_gu"], wts["w_d"], wts["w_pg"], wts["w_p"],
      wts["ln2_g"], wts["ln2_b"])


Z_V = 0
Z_MO = Z_V + D
Z_GM = Z_MO + D
Z_GA = Z_GM + D
Z_XN = Z_GA + D
Z_Q = Z_XN + D
Z_KW = Z_Q + MH * DK
Z_SC = Z_KW + MH * DK
Z_KN = Z_SC + LANES
Z_VN = Z_KN + LANES
Z_W = Z_VN + LANES


def _sample_pre_kernel(x_ref, lng_ref, lnb_ref, w1_ref, w2_ref, w3_ref, wif_ref, bif_ref,
                       cw_ref, cb_ref, cs_ref, n_ref, m_ref,
                       z_ref, qs_ref, convo_ref, no_ref, mo_ref, *, nb, tb):
    w = {"w1": w1_ref, "w2": w2_ref, "w3": w3_ref}
    xn = _layer_norm(x_ref[...], lng_ref[...], lnb_ref[...])
    xb = xn.astype(BF16)
    zqk = _proj(xb, w, "qk")
    cw = cw_ref[...]
    y = cw[3:4] * zqk + cw[2:3] * cs_ref[2] + cw[1:2] * cs_ref[1] + cw[0:1] * cs_ref[0] + cb_ref[...]
    qk = y * _sigmoid(y)
    convo_ref[0] = cs_ref[1]
    convo_ref[1] = cs_ref[2]
    convo_ref[2] = zqk

    gcol = _dot(xb, wif_ref[...]) + bif_ref[...]
    ig = gcol[:, 0:MH]
    m_inter = _log_sigmoid(gcol[:, MH:2 * MH]) + m_ref[...]
    m_t = jnp.maximum(m_inter, ig)
    inter = jnp.exp(m_inter - m_t)
    wi = jnp.exp(ig - m_t)
    enm = jnp.exp(-m_t)
    mo_ref[...] = m_t

    z_ref[:, Z_V:Z_V + D] = _proj(xb, w, "mv")
    z_ref[:, Z_MO:Z_MO + W_HALF] = _proj(xb, w, "mo_lo")
    z_ref[:, Z_MO + W_HALF:Z_MO + D] = _proj(xb, w, "mo_hi")
    z_ref[:, Z_GM:Z_GM + D] = _proj(xb, w, "gm")
    z_ref[:, Z_GA:Z_GA + D] = _proj(xb, w, "ga")
    z_ref[:, Z_XN:Z_XN + D] = xn
    z_ref[:, Z_Q:Z_Q + MH * DK] = qk[:, 0:MH * DK]
    sc = jnp.zeros((nb, LANES), F32)
    lane = lax.broadcasted_iota(jnp.int32, (nb, LANES), 1)
    for h in range(MH):
        qh = qk[:, h * DK:(h + 1) * DK]
        kh = qk[:, MH * DK + h * DK:MH * DK + (h + 1) * DK] * (DK ** -0.5)
        nh = n_ref[:, h * DK:(h + 1) * DK]
        wv = wi[:, h:h + 1] * jnp.sum(qh * kh, -1, keepdims=True)
        den = wv + inter[:, h:h + 1] * jnp.sum(qh * nh, -1, keepdims=True)
        denom = jnp.maximum(jnp.abs(den), enm[:, h:h + 1])
        kw = wi[:, h:h + 1] * kh
        z_ref[:, Z_KW + h * DK:Z_KW + (h + 1) * DK] = kw
        no_ref[:, h * DK:(h + 1) * DK] = inter[:, h:h + 1] * nh + kw
        sc = jnp.where(lane == h, inter[:, h:h + 1], sc)
        sc = jnp.where(lane == MH + h, wv, sc)
        sc = jnp.where(lane == 2 * MH + h, denom, sc)
    z_ref[:, Z_SC:Z_SC + LANES] = sc

    akv = _proj(xb, w, "akv")
    z_ref[:, Z_KN:Z_KN + LANES] = akv[:, 0:LANES]
    z_ref[:, Z_VN:Z_VN + LANES] = akv[:, LANES:2 * LANES]
    aq = _proj(xb, w, "aq") * (HD ** -0.5)
    lo = lax.broadcasted_iota(jnp.int32, (nb, LANES), 1) < HD
    for h in range(AH):
        pair = aq[:, (h // 2) * LANES:(h // 2 + 1) * LANES]
        qm = jnp.where(lo, pair if h % 2 == 0 else pltpu.roll(pair, HD, 1), 0.0)
        for t in range(nb // tb):
            qs_ref[(t * AH + h) * tb:(t * AH + h + 1) * tb, :] = qm[t * tb:(t + 1) * tb, :]


def _sample_state_kernel(z_ref, qs_ref, sink_ref, c_ref, ck_ref, cv_ref,
                         co_ref, qc_ref, os_ref, ko_ref, vo_ref, *, tb):
    lo = lax.broadcasted_iota(jnp.int32, (WIN, LANES), 1) < HD
    rowt = lax.broadcasted_iota(jnp.int32, (tb, DK), 0)
    newest = lax.broadcasted_iota(jnp.int32, (HD, WIN), 1) == WIN - 1
    zpad = jnp.zeros((GROUP, LANES - HD), F32)
    sinks = sink_ref[...]
    q_all = z_ref[:, Z_Q:Z_Q + MH * DK]
    kw_all = z_ref[:, Z_KW:Z_KW + MH * DK]
    v_all = z_ref[:, Z_V:Z_V + D]
    sc = z_ref[:, Z_SC:Z_SC + LANES]
    kn_all = z_ref[:, Z_KN:Z_KN + LANES]
    vn_all = z_ref[:, Z_VN:Z_VN + LANES]
    kn2_all = _dup_halves(kn_all, lo[0:tb])
    vn2_all = _dup_halves(vn_all, lo[0:tb])
    rpad = jnp.zeros((LANES - tb, LANES), F32)
    knt = [jnp.concatenate([a, rpad], axis=0).T for a in kn2_all]
    vnt = [jnp.concatenate([a, rpad], axis=0).T for a in vn2_all]
    for i in range(tb):
        for h in range(MH):
            cb = c_ref[i, h]
            qc = _dot(q_all[:, h * DK:(h + 1) * DK].astype(BF16), cb.astype(BF16))
            qc_ref[i:i + 1, h * DV:(h + 1) * DV] = qc[i:i + 1]
            kw_i = jnp.where(rowt == i, kw_all[:, h * DK:(h + 1) * DK], 0.0)
            outer = _dot_tn(kw_i, v_all[:, h * DV:(h + 1) * DV])
            co_ref[i, h] = sc[i:i + 1, h:h + 1] * cb + outer
        qrow = qs_ref[pl.ds(i, AH, stride=tb), :]
        outs = []
        for g in range(KVH):
            kt = ck_ref[i, g]
            vt = cv_ref[i, g]
            kn_g = kn2_all[g][:, 0:HD]
            vn_g = vn2_all[g][:, 0:HD]
            qg = qrow[g * GROUP:(g + 1) * GROUP, 0:HD]
            s = _dot(qg.astype(BF16), kt.astype(BF16))
            s_self = jnp.sum(qg * kn_g[i:i + 1], -1, keepdims=True)
            sg = sinks[g * GROUP:(g + 1) * GROUP]
            mx = jnp.maximum(jnp.maximum(jnp.max(s, -1, keepdims=True), s_self), sg)
            e = jnp.exp(s - mx)
            es = jnp.exp(s_self - mx)
            dsum = jnp.sum(e, -1, keepdims=True) + es + jnp.exp(sg - mx)
            o = (_dot_nt(e.astype(BF16), vt.astype(BF16)) + es * vn_g[i:i + 1]) / dsum
            outs.append(jnp.concatenate([o, zpad], axis=1))
            kcol = jnp.broadcast_to(knt[g][0:HD, i:i + 1], (HD, WIN))
            vcol = jnp.broadcast_to(vnt[g][0:HD, i:i + 1], (HD, WIN))
            ko_ref[i, g] = jnp.where(newest, kcol, pltpu.roll(kt, WIN - 1, 1))
            vo_ref[i, g] = jnp.where(newest, vcol, pltpu.roll(vt, WIN - 1, 1))
        os_ref[pl.ds(i, AH, stride=tb), :] = jnp.concatenate(outs, axis=0)


def _sample_post_kernel(z_ref, qc_ref, os_ref, mng_ref, wouta_ref, woutb_ref, x1_ref, *, nb, tb):
    sc = z_ref[:, Z_SC:Z_SC + LANES]
    mng = mng_ref[...]
    lo = lax.broadcasted_iota(jnp.int32, (nb, LANES), 1) < HD
    ym = []
    for h in range(MH):
        v = z_ref[:, Z_V + h * DV:Z_V + (h + 1) * DV]
        num = sc[:, MH + h:MH + h + 1] * v + sc[:, h:h + 1] * qc_ref[:, h * DV:(h + 1) * DV]
        hh = num / sc[:, 2 * MH + h:2 * MH + h + 1]
        hn = hh * lax.rsqrt(jnp.mean(hh * hh, -1, keepdims=True) + RMS_EPS)
        ym.append(_sigmoid(z_ref[:, Z_MO + h * DV:Z_MO + (h + 1) * DV]) * hn * mng[:, h * DV:(h + 1) * DV])
    ym = jnp.concatenate(ym, axis=1)
    ya = []
    for pj in range(AH // 2):
        halves = []
        for par in range(2):
            h = 2 * pj + par
            halves.append(jnp.concatenate(
                [os_ref[(t * AH + h) * tb:(t * AH + h + 1) * tb, :] for t in range(nb // tb)], axis=0))
        ya.append(jnp.where(lo, halves[0], pltpu.roll(halves[1], HD, 1)))
    ya = jnp.concatenate(ya, axis=1)
    mixed = _sigmoid(z_ref[:, Z_GM:Z_GM + D]) * ym + _sigmoid(z_ref[:, Z_GA:Z_GA + D]) * ya
    mixed = mixed.astype(BF16)
    r = jnp.concatenate([_dot(mixed, wouta_ref[...]), _dot(mixed, woutb_ref[...])], axis=1)
    x1_ref[...] = ALPHA * z_ref[:, Z_XN:Z_XN + D] + r


def _sample_mixer(x, c0, n0, m0, conv0, k0, v0, wts, *, tb=8):
    nb = x.shape[0]
    assert nb % tb == 0
    vmem = pltpu.CompilerParams(dimension_semantics=("arbitrary",), vmem_limit_bytes=VMEM_LIMIT)
    full = lambda shape: pl.BlockSpec(shape, lambda i: (0,) * len(shape))
    pre_in = [x, wts["ln_in_g"], wts["ln_in_b"], wts["w1"], wts["w2"], wts["w3"], wts["w_if"], wts["b_if"],
              wts["conv_w"], wts["conv_b"], conv0, n0, m0]
    z, qs, conv_new, n_new, m_new = pl.pallas_call(
        functools.partial(_sample_pre_kernel, nb=nb, tb=tb),
        grid=(1,),
        in_specs=[full(a.shape) for a in pre_in],
        out_specs=(full((nb, Z_W)), full((nb * AH, LANES)), full((CONV_W - 1, nb, QK_CH)),
                   full((nb, MH * DK)), full((nb, MH))),
        out_shape=(jax.ShapeDtypeStruct((nb, Z_W), F32), jax.ShapeDtypeStruct((nb * AH, LANES), F32),
                   jax.ShapeDtypeStruct((CONV_W - 1, nb, QK_CH), F32),
                   jax.ShapeDtypeStruct((nb, MH * DK), F32), jax.ShapeDtypeStruct((nb, MH), F32)),
        name="sample_pre", compiler_params=vmem,
    )(*pre_in)

    c_new, qc, os_, k_new, v_new = pl.pallas_call(
        functools.partial(_sample_state_kernel, tb=tb),
        grid=(nb // tb,),
        in_specs=[pl.BlockSpec((tb, Z_W), lambda i: (i, 0)),
                  pl.BlockSpec((tb * AH, LANES), lambda i: (i, 0)),
                  pl.BlockSpec((AH, 1), lambda i: (0, 0)),
                  pl.BlockSpec((tb, MH, DK, DV), lambda i: (i, 0, 0, 0)),
                  pl.BlockSpec((tb, KVH, HD, WIN), lambda i: (i, 0, 0, 0)),
                  pl.BlockSpec((tb, KVH, HD, WIN), lambda i: (i, 0, 0, 0))],
        out_specs=(pl.BlockSpec((tb, MH, DK, DV), lambda i: (i, 0, 0, 0)),
                   pl.BlockSpec((tb, D), lambda i: (i, 0)),
                   pl.BlockSpec((tb * AH, LANES), lambda i: (i, 0)),
                   pl.BlockSpec((tb, KVH, HD, WIN), lambda i: (i, 0, 0, 0)),
                   pl.BlockSpec((tb, KVH, HD, WIN), lambda i: (i, 0, 0, 0))),
        out_shape=(jax.ShapeDtypeStruct((nb, MH, DK, DV), F32), jax.ShapeDtypeStruct((nb, D), F32),
                   jax.ShapeDtypeStruct((nb * AH, LANES), F32),
                   jax.ShapeDtypeStruct((nb, KVH, HD, WIN), F32), jax.ShapeDtypeStruct((nb, KVH, HD, WIN), F32)),
        name="sample_state", compiler_params=vmem,
    )(z, qs, wts["sinks_col"], c0, k0, v0)

    post_in = [z, qc, os_, wts["m_norm_g"], wts["w_out_a"], wts["w_out_b"]]
    x1 = pl.pallas_call(
        functools.partial(_sample_post_kernel, nb=nb, tb=tb),
        grid=(1,),
        in_specs=[full(a.shape) for a in post_in],
        out_specs=full((nb, D)),
        out_shape=jax.ShapeDtypeStruct((nb, D), F32),
        name="sample_post", compiler_params=vmem,
    )(*post_in)
    return x1, c_new, n_new, m_new, conv_new, k_new, v_new


def _regroup_kernel(wt_ref, w1_ref, w2_ref, w3_ref, wif_ref, wift_ref):
    cols = lambda a, b: wt_ref[a:b, :].T.astype(BF16)
    w1_ref[:, 0:2 * D] = cols(O_QK, O_MI)
    w1_ref[:, 2 * D:W1_W] = cols(O_MO, O_MO + W_HALF)
    w2_ref[:, 0:W_HALF + D] = cols(O_MO + W_HALF, O_AK)
    w2_ref[:, W_HALF + D:W2_W] = cols(O_GM, O_GA)
    w3_ref[:, 0:D] = cols(O_GA, IN_WIDTH)
    w3_ref[:, D:W3_W] = cols(O_AK, O_GM)
    gates_t = wt_ref[O_MI:O_MO, :]
    rb = gates_t.shape[1]
    wift_ref[...] = gates_t.astype(BF16)
    wif_ref[...] = jnp.concatenate([gates_t, jnp.zeros((LANES - 2 * MH, rb), F32)], axis=0).T.astype(BF16)


def _regroup_w_in(wt, *, rb=128):
    assert D % rb == 0 and rb == LANES
    return pl.pallas_call(
        _regroup_kernel, grid=(D // rb,),
        in_specs=[pl.BlockSpec((IN_WIDTH, rb), lambda i: (0, i))],
        out_specs=(pl.BlockSpec((rb, W1_W), lambda i: (i, 0)), pl.BlockSpec((rb, W2_W), lambda i: (i, 0)),
                   pl.BlockSpec((rb, W3_W), lambda i: (i, 0)), pl.BlockSpec((rb, LANES), lambda i: (i, 0)),
                   pl.BlockSpec((2 * MH, rb), lambda i: (0, i))),
        out_shape=(jax.ShapeDtypeStruct((D, W1_W), BF16), jax.ShapeDtypeStruct((D, W2_W), BF16),
                   jax.ShapeDtypeStruct((D, W3_W), BF16), jax.ShapeDtypeStruct((D, LANES), BF16),
                   jax.ShapeDtypeStruct((2 * MH, D), BF16)),
        name="regroup_w_in",
        compiler_params=pltpu.CompilerParams(dimension_semantics=("arbitrary",)),
    )(wt)


def _prep_weights(ln_in_g, ln_in_b, w_in, b_igate, b_fgate, conv_w, conv_b, m_norm_g, attn_sinks, w_out,
                  ln1_g, ln1_b, w_gate_up, w_down, ln2_g, ln2_b, w_ple, w_ple_gate):
    row = lambda a: a.reshape(1, -1).astype(F32)
    w1, w2, w3, w_if, w_ift = _regroup_w_in(jnp.transpose(w_in[0], (1, 0)))
    b_if = jnp.pad(jnp.concatenate([b_igate[0], b_fgate[0]]), (0, LANES - 2 * MH))
    wo = w_out[0].astype(BF16)
    return {
        "ln_in_g": row(ln_in_g), "ln_in_b": row(ln_in_b),
        "w1": w1, "w2": w2, "w3": w3,
        "w_if": w_if, "w_ift": w_ift,
        "b_if": b_if.reshape(1, LANES), "b_ift": b_if[:2 * MH].reshape(2 * MH, 1),
        "conv_w": conv_w[0], "conv_b": row(conv_b[0]), "m_norm_g": row(m_norm_g[0]),
        "sinks": attn_sinks[0], "sinks_col": attn_sinks[0].reshape(AH, 1),
        "w_out_a": wo[:, :W_HALF], "w_out_b": wo[:, W_HALF:], "ln1_g": row(ln1_g[0]), "ln1_b": row(ln1_b[0]),
        "w_gu": w_gate_up[0].astype(BF16),
        "w_d": w_down[0].astype(BF16), "w_pg": w_ple_gate[0].astype(BF16), "w_p": w_ple[0].astype(BF16),
        "ln2_g": row(ln2_g[0]), "ln2_b": row(ln2_b[0]),
    }


def _prompt_path(x, p, wts, *, tm=256, lc=128, tmf=1024):
    B, S, _ = x.shape
    x1, c, n, m, conv, k, v = _prompt_mixer(x, wts, tm=tm, lc=lc)
    y = _ffn(x1.reshape(B * S, D), p.reshape(B * S, PD), wts, tm=tmf, sub=tmf // 2).reshape(B, S, D)
    return (y, c[None], n[None], m[None, :, :MH, 0], conv[None],
            k.reshape(1, B, WIN, KVH, HD), v.reshape(1, B, WIN, KVH, HD))


def _sample_path(x, p, c0, n0, m0, conv0, k0, v0, wts, *, tb=8):
    nb = x.shape[0]
    x1, c, n, m, conv, k, v = _sample_mixer(
        x.reshape(nb, D), c0[0], n0[0].reshape(nb, MH * DK), m0[0],
        jnp.transpose(conv0[0], (1, 0, 2)), jnp.transpose(k0[0], (0, 2, 3, 1)),
        jnp.transpose(v0[0], (0, 2, 3, 1)), wts, tb=tb)
    y = _ffn(x1, p.reshape(nb, PD), wts, tm=nb, sub=nb).reshape(nb, 1, D)
    return (y, c[None], n.reshape(1, nb, MH, DK), m[None], jnp.transpose(conv, (1, 0, 2))[None],
            jnp.transpose(k, (0, 3, 1, 2))[None], jnp.transpose(v, (0, 3, 1, 2))[None])


def kernel(x_prompt, x_sample, state_mlstm_C, state_mlstm_n, state_mlstm_m, state_conv, cache_win_k, cache_win_v, p_prompt, p_sample, ln_in_g, ln_in_b, w_in, b_igate, b_fgate, conv_w, conv_b, m_norm_g, attn_sinks, w_out, ln1_g, ln1_b, w_gate_up, w_down, ln2_g, ln2_b, w_ple, w_ple_gate):
    wts = _prep_weights(ln_in_g, ln_in_b, w_in, b_igate, b_fgate, conv_w, conv_b, m_norm_g, attn_sinks,
                        w_out, ln1_g, ln1_b, w_gate_up, w_down, ln2_g, ln2_b, w_ple, w_ple_gate)
    yp, c_p, n_p, m_p, conv_p, k_p, v_p = _prompt_path(x_prompt, p_prompt[0], wts)
    ys, c_s, n_s, m_s, conv_s, k_s, v_s = _sample_path(
        x_sample, p_sample[0], state_mlstm_C, state_mlstm_n, state_mlstm_m, state_conv,
        cache_win_k, cache_win_v, wts)
    return (yp, ys, c_p, n_p, m_p, conv_p, k_p, v_p, c_s, n_s, m_s, conv_s, k_s, v_s)
```

```python
import functools

import jax
import jax.numpy as jnp
from jax import lax
from jax.experimental import pallas as pl
from jax.experimental.pallas import tpu as pltpu

F32 = jnp.float32
BF16 = jnp.bfloat16

D = 1024
MH = 4
DV = D // MH
DK = DV // 2
CONV_W = 4
QK_CH = 2 * MH * DK
AH = 16
KVH = 2
HD = D // AH
GROUP = AH // KVH
WIN = 128
DFF = 2816
PD = 256
LN_EPS = 1e-5
RMS_EPS = 1e-6
ALPHA = 2.0 ** 0.25
LOG2E = 1.4426950408889634
NEG_INF = float("-inf")

LANES = 128
SUBLANES = 8
MXU_DIM = 256
VMEM_LIMIT = 56 * 1024 * 1024

O_QK = 0
O_MV = O_QK + QK_CH
O_MI = O_MV + D
O_MF = O_MI + MH
O_MO = O_MF + MH
O_AQ = O_MO + D
O_AK = O_AQ + D
O_AV = O_AK + KVH * HD
O_GM = O_AV + KVH * HD
O_GA = O_GM + D
IN_WIDTH = O_GA + D

W_HALF = D // 2
W1_W = 2 * D + W_HALF
W2_W = 2 * D + W_HALF
W3_W = D + 2 * KVH * HD
_WCOL = {"qk": ("w1", 0, D), "mv": ("w1", D, 2 * D), "mo_lo": ("w1", 2 * D, W1_W),
         "mo_hi": ("w2", 0, W_HALF), "aq": ("w2", W_HALF, W_HALF + D), "gm": ("w2", W_HALF + D, W2_W),
         "ga": ("w3", 0, D), "akv": ("w3", D, W3_W)}


def _dot(a, b):
    return jnp.dot(a, b, preferred_element_type=F32)


def _dot_nt(a, b):
    return lax.dot_general(a, b, (((1,), (1,)), ((), ())), preferred_element_type=F32)


def _dot_tn(a, b):
    return lax.dot_general(a, b, (((0,), (0,)), ((), ())), preferred_element_type=F32)


def _proj(xb, w, name):
    blk, a, b = _WCOL[name]
    return _dot(xb, w[blk][:, a:b])


def _layer_norm(x, g, b):
    mu = jnp.mean(x, -1, keepdims=True)
    xc = x - mu
    var = jnp.mean(xc * xc, -1, keepdims=True)
    return xc * lax.rsqrt(var + LN_EPS) * g + b


def _sigmoid(x):
    return 1.0 / (1.0 + jnp.exp(-x))


def _log_sigmoid(x):
    return jnp.minimum(x, 0.0) - jnp.log(1.0 + jnp.exp(-jnp.abs(x)))


def _const_spec(shape):
    zeros = (0,) * len(shape)
    return pl.BlockSpec(shape, lambda i: zeros, pipeline_mode=pl.Buffered(1))


def _dup_halves(x, lo):
    xr = pltpu.roll(x, HD, 1)
    return jnp.where(lo, x, xr), jnp.where(lo, xr, x)


_PROJ_NAMES = ("xn", "zqk", "vm", "smo", "aq", "k2", "v2", "kvl", "sgm", "sga", "gcol", "grow", "brow")
_MIXER_W_NAMES = ("lng", "lnb", "w1", "w2", "w3", "wift", "bift", "cw", "cb", "mng", "sink", "wout_a", "wout_b")


def _mixer_project(x, slot, w, buf, *, tm, lc):
    xn = _layer_norm(x, w["lng"][...], w["lnb"][...])
    xb = xn.astype(BF16)
    buf["xn"][slot] = xn
    buf["zqk"][slot] = _proj(xb, w, "qk")
    buf["vm"][slot] = _proj(xb, w, "mv").astype(BF16)
    mo = jnp.concatenate([_proj(xb, w, "mo_lo"), _proj(xb, w, "mo_hi")], axis=1)
    buf["smo"][slot] = _sigmoid(mo) * w["mng"][...]
    aq = _proj(xb, w, "aq") * (HD ** -0.5 * LOG2E)
    lo_d = (lax.broadcasted_iota(jnp.int32, (tm, D), 1) & (LANES - 1)) < HD
    aqm = (jnp.where(lo_d, aq, 0.0).astype(BF16), jnp.where(lo_d, 0.0, aq).astype(BF16))
    for j in range(tm // WIN):
        for h in range(AH):
            buf["aq"][slot, j, h // GROUP, (h % GROUP) * WIN:(h % GROUP + 1) * WIN, :] = (
                aqm[h % 2][j * WIN:(j + 1) * WIN, (h // 2) * LANES:(h // 2 + 1) * LANES])
    akv = _proj(xb, w, "akv")
    lo = lax.broadcasted_iota(jnp.int32, (tm, LANES), 1) < HD
    k2 = _dup_halves(akv[:, 0:LANES], lo)
    v2 = _dup_halves(akv[:, LANES:2 * LANES], lo)
    for g in range(KVH):
        buf["k2"][slot, g] = k2[g].astype(BF16)
        buf["v2"][slot, g] = v2[g].astype(BF16)
    buf["kvl"][slot] = akv[tm - WIN:tm, :]
    buf["sgm"][slot] = _sigmoid(_proj(xb, w, "gm"))
    buf["sga"][slot] = _sigmoid(_proj(xb, w, "ga"))
    grow = _dot_nt(w["wift"][...], xb) + w["bift"][...]
    buf["grow"][slot] = grow
    lsrow = _log_sigmoid(grow)
    lane = lax.broadcasted_iota(jnp.int32, (SUBLANES, lc), 1)
    pad = jnp.zeros((lc - 2 * SUBLANES, lc), F32)
    for c in range(tm // lc):
        r0 = c * lc
        seg = lsrow[:, r0:r0 + lc]
        sh = 1
        while sh < lc:
            seg = seg + jnp.where(lane >= sh, pltpu.roll(seg, sh, 1), 0.0)
            sh *= 2
        buf["brow"][slot, :, r0:r0 + lc] = seg
        buf["gcol"][slot, r0:r0 + lc, :] = jnp.concatenate([grow[:, r0:r0 + lc], seg, pad], axis=0).T


def _mixer_consume(slot, first, w, buf, st, x1_ref, row0, *, tm, lc):
    c_ref, n_ref, m_ref, ctail, kvc, ym_s, ya_s = st
    x1_ref[0, row0:row0 + tm, :] = ALPHA * buf["xn"][slot]
    z = buf["zqk"][slot]
    tail = ctail[...]
    ctail[...] = z[tm - SUBLANES:tm, :]
    cw = w["cw"][...]
    rows8 = lax.broadcasted_iota(jnp.int32, (SUBLANES, QK_CH), 0)
    y = cw[CONV_W - 1:CONV_W] * z + w["cb"][...]
    for sh in range(1, CONV_W):
        zr = pltpu.roll(z, sh, 0)
        head = jnp.where(rows8 < sh, pltpu.roll(tail, sh, 0), zr[0:SUBLANES])
        zr = jnp.concatenate([head, zr[SUBLANES:]], axis=0)
        y = y + cw[CONV_W - 1 - sh:CONV_W - sh] * zr
    qk = y * _sigmoid(y)

    ri = lax.broadcasted_iota(jnp.int32, (lc, lc), 0)
    ci = lax.broadcasted_iota(jnp.int32, (lc, lc), 1)
    causal = ci <= ri
    for c in range(tm // lc):
        r0 = c * lc
        for h in range(MH):
            qh = qk[r0:r0 + lc, h * DK:(h + 1) * DK]
            kh = qk[r0:r0 + lc, MH * DK + h * DK:MH * DK + (h + 1) * DK] * (DK ** -0.5)
            vh = buf["vm"][slot, r0:r0 + lc, h * DV:(h + 1) * DV]
            qb = qh.astype(BF16)
            bc = buf["gcol"][slot, r0:r0 + lc, SUBLANES + MH + h:SUBLANES + MH + h + 1]
            br = buf["brow"][slot, MH + h:MH + h + 1, r0:r0 + lc]
            li_c = buf["gcol"][slot, r0:r0 + lc, h:h + 1]
            li_r = buf["grow"][slot, h:h + 1, r0:r0 + lc]
            m_prev = m_ref[0, h:h + 1, 0:1]
            c_prev = c_ref[0, h]
            n_prev = n_ref[0, h:h + 1, :]

            dmat = jnp.where(causal, bc - br + li_r, NEG_INF)
            m_inter = bc + m_prev
            m_t = jnp.maximum(m_inter, jnp.max(dmat, -1, keepdims=True))
            wgt = jnp.exp(dmat - m_t) * _dot_nt(qb, kh.astype(BF16))
            inter = jnp.exp(m_inter - m_t)
            num = _dot(wgt.astype(BF16), vh) + inter * _dot(qb, c_prev.astype(BF16))
            den = jnp.sum(wgt, -1, keepdims=True) + inter * jnp.sum(qh * n_prev, -1, keepdims=True)
            denom = jnp.maximum(jnp.abs(den), jnp.exp(-m_t))
            hh = num / denom
            hn = hh * lax.rsqrt(jnp.mean(hh * hh, -1, keepdims=True) + RMS_EPS)
            ym_s[r0:r0 + lc, h * DV:(h + 1) * DV] = buf["smo"][slot, r0:r0 + lc, h * DV:(h + 1) * DV] * hn

            m_new = m_t[lc - 1:lc, :]
            bc_last = bc[lc - 1:lc, :]
            decay = jnp.exp(bc_last + m_prev - m_new)
            ks = kh * jnp.exp(bc_last - bc + li_c - m_new)
            c_ref[0, h] = decay * c_prev + _dot_tn(ks.astype(BF16), vh)
            n_ref[0, h:h + 1, :] = decay * n_prev + jnp.sum(ks, 0, keepdims=True)
            m_ref[0, h:h + 1, :] = jnp.broadcast_to(m_new, (1, LANES))

    lo_q = lax.broadcasted_iota(jnp.int32, (WIN, LANES), 1) < HD
    qi = lax.broadcasted_iota(jnp.int32, (WIN, 2 * WIN), 0)
    kj = lax.broadcasted_iota(jnp.int32, (WIN, 2 * WIN), 1)
    band = (kj >= qi) & (kj <= qi + WIN)
    ones_v = jnp.ones((2 * WIN, LANES), BF16)
    for j in range(tm // WIN):
        r0 = j * WIN
        if j == 0 and first is not None:
            mask = band & (kj >= jnp.where(first, WIN, 0))
        else:
            mask = band
        for g in range(KVH):
            if j == 0:
                kprev, vprev = kvc[g], kvc[KVH + g]
            else:
                kprev, vprev = buf["k2"][slot, g, r0 - WIN:r0, :], buf["v2"][slot, g, r0 - WIN:r0, :]
            kcat = jnp.concatenate([kprev, buf["k2"][slot, g, r0:r0 + WIN, :]], axis=0)
            vcat = jnp.concatenate([vprev, buf["v2"][slot, g, r0:r0 + WIN, :]], axis=0)
            sc_all = _dot_nt(buf["aq"][slot, j, g], kcat)
            es, ds = [], []
            for hh in range(GROUP):
                sink = w["sink"][g * GROUP + hh] * LOG2E
                sc = jnp.where(mask, sc_all[hh * WIN:(hh + 1) * WIN, :], NEG_INF)
                mx = jnp.maximum(jnp.max(sc, -1, keepdims=True), sink)
                es.append(jnp.exp2(sc - mx).astype(BF16))
                ds.append(jnp.exp2(sink - mx))
            o_all = _dot(jnp.concatenate(es, axis=0), jnp.concatenate([vcat, ones_v], axis=1))
            for pp in range(GROUP // 2):
                pj = g * GROUP // 2 + pp
                halves = []
                for hh in (2 * pp, 2 * pp + 1):
                    oh = o_all[hh * WIN:(hh + 1) * WIN, :]
                    halves.append(oh[:, 0:LANES] / (oh[:, LANES:2 * LANES] + ds[hh]))
                ya_s[r0:r0 + WIN, pj * LANES:(pj + 1) * LANES] = jnp.where(lo_q, halves[0], halves[1])
    for g in range(KVH):
        kvc[g] = buf["k2"][slot, g, tm - WIN:tm, :]
        kvc[KVH + g] = buf["v2"][slot, g, tm - WIN:tm, :]

    mixed = (buf["sgm"][slot] * ym_s[...] + buf["sga"][slot] * ya_s[...]).astype(BF16)
    r = jnp.concatenate([_dot(mixed, w["wout_a"][...]), _dot(mixed, w["wout_b"][...])], axis=1)
    x1_ref[0, row0:row0 + tm, :] = x1_ref[0, row0:row0 + tm, :] + r


def _mixer_kernel(*refs, tm, lc, npb, tps):
    nw = len(_MIXER_W_NAMES)
    x0_ref = refs[0]
    xt_refs = refs[1:1 + tps]
    w = dict(zip(_MIXER_W_NAMES, refs[1 + tps:1 + tps + nw]))
    x1_ref, c_ref, n_ref, m_ref, conv_ref, kp_ref, vp_ref = refs[1 + tps + nw:8 + tps + nw]
    scratch = refs[8 + tps + nw:]
    buf = dict(zip(_PROJ_NAMES, scratch[:len(_PROJ_NAMES)]))
    ctail, kvc, ym_s, ya_s = scratch[len(_PROJ_NAMES):]
    st = (c_ref, n_ref, m_ref, ctail, kvc, ym_s, ya_s)
    k = pl.program_id(0)
    first = (tps * k) % npb == 0

    @pl.when(k == 0)
    def _prologue():
        _mixer_project(x0_ref[0], 0, w, buf, tm=tm, lc=lc)

    @pl.when(first)
    def _init():
        c_ref[...] = jnp.zeros_like(c_ref)
        n_ref[...] = jnp.zeros_like(n_ref)
        m_ref[...] = jnp.zeros_like(m_ref)
        ctail[...] = jnp.zeros_like(ctail)
        kvc[...] = jnp.zeros_like(kvc)

    for i in range(tps):
        _mixer_project(xt_refs[i][0], (i + 1) % 2, w, buf, tm=tm, lc=lc)
        _mixer_consume(i % 2, first if i == 0 else None, w, buf, st, x1_ref, i * tm, tm=tm, lc=lc)

    @pl.when((tps * k + tps - 1) % npb == npb - 1)
    def _state_out():
        conv_ref[0] = ctail[SUBLANES - (CONV_W - 1):SUBLANES, :]
        kp_ref[0] = buf["kvl"][(tps - 1) % 2, :, 0:LANES]
        vp_ref[0] = buf["kvl"][(tps - 1) % 2, :, LANES:2 * LANES]


def _prompt_mixer(x, wts, *, tm=256, lc=128, tps=2):
    B, S, _ = x.shape
    npb = S // tm
    assert S % tm == 0 and tps % 2 == 0 and npb % tps == 0 and tm % lc == 0 and tm % WIN == 0
    nt = B * npb
    hpb = npb // tps
    cs = _const_spec

    def tile_map(off):
        def imap(k):
            t = jnp.minimum(tps * k + off, nt - 1)
            return (t // npb, t % npb, 0)
        return imap

    in_specs = [
        pl.BlockSpec((1, tm, D), lambda k: (0, 0, 0), pipeline_mode=pl.Buffered(1)),
        *[pl.BlockSpec((1, tm, D), tile_map(i + 1)) for i in range(tps)],
        cs((1, D)), cs((1, D)),
        cs((D, W1_W)), cs((D, W2_W)), cs((D, W3_W)), cs((SUBLANES, D)), cs((SUBLANES, 1)),
        cs((CONV_W, QK_CH)), cs((1, QK_CH)), cs((1, D)),
        pl.BlockSpec(memory_space=pltpu.SMEM),
        cs((D, W_HALF)), cs((D, W_HALF)),
    ]
    out_shape = (
        jax.ShapeDtypeStruct((B, S, D), F32),
        jax.ShapeDtypeStruct((B, MH, DK, DV), F32),
        jax.ShapeDtypeStruct((B, MH, DK), F32),
        jax.ShapeDtypeStruct((B, SUBLANES, LANES), F32),
        jax.ShapeDtypeStruct((B, CONV_W - 1, QK_CH), F32),
        jax.ShapeDtypeStruct((B, WIN, KVH * HD), F32),
        jax.ShapeDtypeStruct((B, WIN, KVH * HD), F32),
    )
    out_specs = (
        pl.BlockSpec((1, tps * tm, D), lambda k: (k // hpb, k % hpb, 0)),
        pl.BlockSpec((1, MH, DK, DV), lambda k: (k // hpb, 0, 0, 0)),
        pl.BlockSpec((1, MH, DK), lambda k: (k // hpb, 0, 0)),
        pl.BlockSpec((1, SUBLANES, LANES), lambda k: (k // hpb, 0, 0)),
        pl.BlockSpec((1, CONV_W - 1, QK_CH), lambda k: (k // hpb, 0, 0)),
        pl.BlockSpec((1, WIN, KVH * HD), lambda k: (k // hpb, 0, 0)),
        pl.BlockSpec((1, WIN, KVH * HD), lambda k: (k // hpb, 0, 0)),
    )
    proj_scratch = {
        "xn": pltpu.VMEM((2, tm, D), F32), "zqk": pltpu.VMEM((2, tm, QK_CH), F32),
        "vm": pltpu.VMEM((2, tm, D), BF16), "smo": pltpu.VMEM((2, tm, D), F32),
        "aq": pltpu.VMEM((2, tm // WIN, KVH, GROUP * WIN, LANES), BF16),
        "k2": pltpu.VMEM((2, KVH, tm, LANES), BF16), "v2": pltpu.VMEM((2, KVH, tm, LANES), BF16),
        "kvl": pltpu.VMEM((2, WIN, 2 * LANES), F32),
        "sgm": pltpu.VMEM((2, tm, D), F32), "sga": pltpu.VMEM((2, tm, D), F32),
        "gcol": pltpu.VMEM((2, tm, LANES), F32), "grow": pltpu.VMEM((2, SUBLANES, tm), F32),
        "brow": pltpu.VMEM((2, SUBLANES, tm), F32),
    }
    scratch = [proj_scratch[n] for n in _PROJ_NAMES] + [
        pltpu.VMEM((SUBLANES, QK_CH), F32),
        pltpu.VMEM((2 * KVH, WIN, LANES), BF16),
        pltpu.VMEM((tm, D), F32),
        pltpu.VMEM((tm, D), F32),
    ]
    return pl.pallas_call(
        functools.partial(_mixer_kernel, tm=tm, lc=lc, npb=npb, tps=tps),
        grid=(nt // tps,), in_specs=in_specs, out_specs=out_specs, out_shape=out_shape,
        scratch_shapes=scratch, name="prompt_mixer",
        compiler_params=pltpu.CompilerParams(dimension_semantics=("arbitrary",), vmem_limit_bytes=VMEM_LIMIT),
    )(*([x] * (tps + 1)), wts["ln_in_g"], wts["ln_in_b"], wts["w1"], wts["w2"], wts["w3"], wts["w_ift"], wts["b_ift"],
      wts["conv_w"], wts["conv_b"], wts["m_norm_g"], wts["sinks"], wts["w_out_a"], wts["w_out_b"])


FFN_CHUNKS = ((0, 6 * MXU_DIM), (6 * MXU_DIM, DFF))


def _ffn_kernel(x_ref, p_ref, l1g_ref, l1b_ref, wgu_ref, wd_ref, wpg_ref, wp_ref, g_ref, b_ref, o_ref, *, sub):
    n = x_ref.shape[0] // sub

    def pre(j):
        x = _layer_norm(x_ref[j * sub:(j + 1) * sub, :], l1g_ref[...], l1b_ref[...])
        return x, x.astype(BF16)

    def body(j, x, xb):
        acc = ALPHA * x + (_sigmoid(_dot(xb, wpg_ref[...]))
                           * _dot(p_ref[j * sub:(j + 1) * sub, :].astype(BF16), wp_ref[...]))
        for c0, c1 in FFN_CHUNKS:
            g = _dot(xb, wgu_ref[:, c0:c1])
            u = _dot(xb, wgu_ref[:, DFF + c0:DFF + c1])
            hcat = (g * _sigmoid(g) * u).astype(BF16)
            acc = acc + _dot(hcat, wd_ref[c0:c1, :])
        return acc

    def post(j, acc):
        o_ref[j * sub:(j + 1) * sub, :] = _layer_norm(acc, g_ref[...], b_ref[...])

    cur = pre(0)
    acc_prev = None
    for j in range(n):
        nxt = pre(j + 1) if j + 1 < n else None
        acc = body(j, *cur)
        if acc_prev is not None:
            post(j - 1, acc_prev)
        acc_prev, cur = acc, nxt
    post(n - 1, acc_prev)


def _ffn(x, p, wts, *, tm, sub):
    N = x.shape[0]
    assert N % tm == 0 and tm % sub == 0
    cs = _const_spec
    return pl.pallas_call(
        functools.partial(_ffn_kernel, sub=sub),
        grid=(N // tm,),
        in_specs=[pl.BlockSpec((tm, D), lambda i: (i, 0)), pl.BlockSpec((tm, PD), lambda i: (i, 0)),
                  cs((1, D)), cs((1, D)),
                  cs((D, 2 * DFF)), cs((DFF, D)), cs((D, D)), cs((PD, D)), cs((1, D)), cs((1, D))],
        out_specs=pl.BlockSpec((tm, D), lambda i: (i, 0)),
        out_shape=jax.ShapeDtypeStruct((N, D), F32),
        name="ffn",
        compiler_params=pltpu.CompilerParams(dimension_semantics=("arbitrary",), vmem_limit_bytes=VMEM_LIMIT),
    )(x, p, wts["ln1_g"], wts["ln1_b"], wts["w_gu"], wts["w_d"], wts["w_pg"], wts["w_p"],
      wts["ln2_g"], wts["ln2_b"])


Z_V = 0
Z_MO = Z_V + D
Z_GM = Z_MO + D
Z_GA = Z_GM + D
Z_XN = Z_GA + D
Z_Q = Z_XN + D
Z_KW = Z_Q + MH * DK
Z_SC = Z_KW + MH * DK
Z_KN = Z_SC + LANES
Z_VN = Z_KN + LANES
Z_W = Z_VN + LANES


def _sample_pre_kernel(x_ref, lng_ref, lnb_ref, w1_ref, w2_ref, w3_ref, wif_ref, bif_ref,
                       cw_ref, cb_ref, cs_ref, n_ref, m_ref,
                       z_ref, qs_ref, convo_ref, no_ref, mo_ref, *, nb, tb):
    w = {"w1": w1_ref, "w2": w2_ref, "w3": w3_ref}
    xn = _layer_norm(x_ref[...], lng_ref[...], lnb_ref[...])
    xb = xn.astype(BF16)
    zqk = _proj(xb, w, "qk")
    cw = cw_ref[...]
    y = cw[3:4] * zqk + cw[2:3] * cs_ref[2] + cw[1:2] * cs_ref[1] + cw[0:1] * cs_ref[0] + cb_ref[...]
    qk = y * _sigmoid(y)
    convo_ref[0] = cs_ref[1]
    convo_ref[1] = cs_ref[2]
    convo_ref[2] = zqk

    gcol = _dot(xb, wif_ref[...]) + bif_ref[...]
    ig = gcol[:, 0:MH]
    m_inter = _log_sigmoid(gcol[:, MH:2 * MH]) + m_ref[...]
    m_t = jnp.maximum(m_inter, ig)
    inter = jnp.exp(m_inter - m_t)
    wi = jnp.exp(ig - m_t)
    enm = jnp.exp(-m_t)
    mo_ref[...] = m_t

    z_ref[:, Z_V:Z_V + D] = _proj(xb, w, "mv")
    z_ref[:, Z_MO:Z_MO + W_HALF] = _proj(xb, w, "mo_lo")
    z_ref[:, Z_MO + W_HALF:Z_MO + D] = _proj(xb, w, "mo_hi")
    z_ref[:, Z_GM:Z_GM + D] = _proj(xb, w, "gm")
    z_ref[:, Z_GA:Z_GA + D] = _proj(xb, w, "ga")
    z_ref[:, Z_XN:Z_XN + D] = xn
    z_ref[:, Z_Q:Z_Q + MH * DK] = qk[:, 0:MH * DK]
    sc = jnp.zeros((nb, LANES), F32)
    lane = lax.broadcasted_iota(jnp.int32, (nb, LANES), 1)
    for h in range(MH):
        qh = qk[:, h * DK:(h + 1) * DK]
        kh = qk[:, MH * DK + h * DK:MH * DK + (h + 1) * DK] * (DK ** -0.5)
        nh = n_ref[:, h * DK:(h + 1) * DK]
        wv = wi[:, h:h + 1] * jnp.sum(qh * kh, -1, keepdims=True)
        den = wv + inter[:, h:h + 1] * jnp.sum(qh * nh, -1, keepdims=True)
        denom = jnp.maximum(jnp.abs(den), enm[:, h:h + 1])
        kw = wi[:, h:h + 1] * kh
        z_ref[:, Z_KW + h * DK:Z_KW + (h + 1) * DK] = kw
        no_ref[:, h * DK:(h + 1) * DK] = inter[:, h:h + 1] * nh + kw
        sc = jnp.where(lane == h, inter[:, h:h + 1], sc)
        sc = jnp.where(lane == MH + h, wv, sc)
        sc = jnp.where(lane == 2 * MH + h, denom, sc)
    z_ref[:, Z_SC:Z_SC + LANES] = sc

    akv = _proj(xb, w, "akv")
    z_ref[:, Z_KN:Z_KN + LANES] = akv[:, 0:LANES]
    z_ref[:, Z_VN:Z_VN + LANES] = akv[:, LANES:2 * LANES]
    aq = _proj(xb, w, "aq") * (HD ** -0.5)
    lo = lax.broadcasted_iota(jnp.int32, (nb, LANES), 1) < HD
    for h in range(AH):
        pair = aq[:, (h // 2) * LANES:(h // 2 + 1) * LANES]
        qm = jnp.where(lo, pair if h % 2 == 0 else pltpu.roll(pair, HD, 1), 0.0)
        for t in range(nb // tb):
            qs_ref[(t * AH + h) * tb:(t * AH + h + 1) * tb, :] = qm[t * tb:(t + 1) * tb, :]


def _sample_state_kernel(z_ref, qs_ref, sink_ref, c_ref, ck_ref, cv_ref,
                         co_ref, qc_ref, os_ref, ko_ref, vo_ref, *, tb):
    lo = lax.broadcasted_iota(jnp.int32, (WIN, LANES), 1) < HD
    rowt = lax.broadcasted_iota(jnp.int32, (tb, DK), 0)
    newest = lax.broadcasted_iota(jnp.int32, (HD, WIN), 1) == WIN - 1
    zpad = jnp.zeros((GROUP, LANES - HD), F32)
    sinks = sink_ref[...]
    q_all = z_ref[:, Z_Q:Z_Q + MH * DK]
    kw_all = z_ref[:, Z_KW:Z_KW + MH * DK]
    v_all = z_ref[:, Z_V:Z_V + D]
    sc = z_ref[:, Z_SC:Z_SC + LANES]
    kn_all = z_ref[:, Z_KN:Z_KN + LANES]
    vn_all = z_ref[:, Z_VN:Z_VN + LANES]
    kn2_all = _dup_halves(kn_all, lo[0:tb])
    vn2_all = _dup_halves(vn_all, lo[0:tb])
    rpad = jnp.zeros((LANES - tb, LANES), F32)
    knt = [jnp.concatenate([a, rpad], axis=0).T for a in kn2_all]
    vnt = [jnp.concatenate([a, rpad], axis=0).T for a in vn2_all]
    for i in range(tb):
        for h in range(MH):
            cb = c_ref[i, h]
            qc = _dot(q_all[:, h * DK:(h + 1) * DK].astype(BF16), cb.astype(BF16))
            qc_ref[i:i + 1, h * DV:(h + 1) * DV] = qc[i:i + 1]
            kw_i = jnp.where(rowt == i, kw_all[:, h * DK:(h + 1) * DK], 0.0)
            outer = _dot_tn(kw_i, v_all[:, h * DV:(h + 1) * DV])
            co_ref[i, h] = sc[i:i + 1, h:h + 1] * cb + outer
        qrow = qs_ref[pl.ds(i, AH, stride=tb), :]
        outs = []
        for g in range(KVH):
            kt = ck_ref[i, g]
            vt = cv_ref[i, g]
            kn_g = kn2_all[g][:, 0:HD]
            vn_g = vn2_all[g][:, 0:HD]
            qg = qrow[g * GROUP:(g + 1) * GROUP, 0:HD]
            s = _dot(qg.astype(BF16), kt.astype(BF16))
            s_self = jnp.sum(qg * kn_g[i:i + 1], -1, keepdims=True)
            sg = sinks[g * GROUP:(g + 1) * GROUP]
            mx = jnp.maximum(jnp.maximum(jnp.max(s, -1, keepdims=True), s_self), sg)
            e = jnp.exp(s - mx)
            es = jnp.exp(s_self - mx)
            dsum = jnp.sum(e, -1, keepdims=True) + es + jnp.exp(sg - mx)
            o = (_dot_nt(e.astype(BF16), vt.astype(BF16)) + es * vn_g[i:i + 1]) / dsum
            outs.append(jnp.concatenate([o, zpad], axis=1))
            kcol = jnp.broadcast_to(knt[g][0:HD, i:i + 1], (HD, WIN))
            vcol = jnp.broadcast_to(vnt[g][0:HD, i:i + 1], (HD, WIN))
            ko_ref[i, g] = jnp.where(newest, kcol, pltpu.roll(kt, WIN - 1, 1))
            vo_ref[i, g] = jnp.where(newest, vcol, pltpu.roll(vt, WIN - 1, 1))
        os_ref[pl.ds(i, AH, stride=tb), :] = jnp.concatenate(outs, axis=0)


def _sample_post_kernel(z_ref, qc_ref, os_ref, mng_ref, wouta_ref, woutb_ref, x1_ref, *, nb, tb):
    sc = z_ref[:, Z_SC:Z_SC + LANES]
    mng = mng_ref[...]
    lo = lax.broadcasted_iota(jnp.int32, (nb, LANES), 1) < HD
    ym = []
    for h in range(MH):
        v = z_ref[:, Z_V + h * DV:Z_V + (h + 1) * DV]
        num = sc[:, MH + h:MH + h + 1] * v + sc[:, h:h + 1] * qc_ref[:, h * DV:(h + 1) * DV]
        hh = num / sc[:, 2 * MH + h:2 * MH + h + 1]
        hn = hh * lax.rsqrt(jnp.mean(hh * hh, -1, keepdims=True) + RMS_EPS)
        ym.append(_sigmoid(z_ref[:, Z_MO + h * DV:Z_MO + (h + 1) * DV]) * hn * mng[:, h * DV:(h + 1) * DV])
    ym = jnp.concatenate(ym, axis=1)
    ya = []
    for pj in range(AH // 2):
        halves = []
        for par in range(2):
            h = 2 * pj + par
            halves.append(jnp.concatenate(
                [os_ref[(t * AH + h) * tb:(t * AH + h + 1) * tb, :] for t in range(nb // tb)], axis=0))
        ya.append(jnp.where(lo, halves[0], pltpu.roll(halves[1], HD, 1)))
    ya = jnp.concatenate(ya, axis=1)
    mixed = _sigmoid(z_ref[:, Z_GM:Z_GM + D]) * ym + _sigmoid(z_ref[:, Z_GA:Z_GA + D]) * ya
    mixed = mixed.astype(BF16)
    r = jnp.concatenate([_dot(mixed, wouta_ref[...]), _dot(mixed, woutb_ref[...])], axis=1)
    x1_ref[...] = ALPHA * z_ref[:, Z_XN:Z_XN + D] + r


def _sample_mixer(x, c0, n0, m0, conv0, k0, v0, wts, *, tb=8):
    nb = x.shape[0]
    assert nb % tb == 0
    vmem = pltpu.CompilerParams(dimension_semantics=("arbitrary",), vmem_limit_bytes=VMEM_LIMIT)
    full = lambda shape: pl.BlockSpec(shape, lambda i: (0,) * len(shape))
    pre_in = [x, wts["ln_in_g"], wts["ln_in_b"], wts["w1"], wts["w2"], wts["w3"], wts["w_if"], wts["b_if"],
              wts["conv_w"], wts["conv_b"], conv0, n0, m0]
    z, qs, conv_new, n_new, m_new = pl.pallas_call(
        functools.partial(_sample_pre_kernel, nb=nb, tb=tb),
        grid=(1,),
        in_specs=[full(a.shape) for a in pre_in],
        out_specs=(full((nb, Z_W)), full((nb * AH, LANES)), full((CONV_W - 1, nb, QK_CH)),
                   full((nb, MH * DK)), full((nb, MH))),
        out_shape=(jax.ShapeDtypeStruct((nb, Z_W), F32), jax.ShapeDtypeStruct((nb * AH, LANES), F32),
                   jax.ShapeDtypeStruct((CONV_W - 1, nb, QK_CH), F32),
                   jax.ShapeDtypeStruct((nb, MH * DK), F32), jax.ShapeDtypeStruct((nb, MH), F32)),
        name="sample_pre", compiler_params=vmem,
    )(*pre_in)

    c_new, qc, os_, k_new, v_new = pl.pallas_call(
        functools.partial(_sample_state_kernel, tb=tb),
        grid=(nb // tb,),
        in_specs=[pl.BlockSpec((tb, Z_W), lambda i: (i, 0)),
                  pl.BlockSpec((tb * AH, LANES), lambda i: (i, 0)),
                  pl.BlockSpec((AH, 1), lambda i: (0, 0)),
                  pl.BlockSpec((tb, MH, DK, DV), lambda i: (i, 0, 0, 0)),
                  pl.BlockSpec((tb, KVH, HD, WIN), lambda i: (i, 0, 0, 0)),
                  pl.BlockSpec((tb, KVH, HD, WIN), lambda i: (i, 0, 0, 0))],
        out_specs=(pl.BlockSpec((tb, MH, DK, DV), lambda i: (i, 0, 0, 0)),
                   pl.BlockSpec((tb, D), lambda i: (i, 0)),
                   pl.BlockSpec((tb * AH, LANES), lambda i: (i, 0)),
                   pl.BlockSpec((tb, KVH, HD, WIN), lambda i: (i, 0, 0, 0)),
                   pl.BlockSpec((tb, KVH, HD, WIN), lambda i: (i, 0, 0, 0))),
        out_shape=(jax.ShapeDtypeStruct((nb, MH, DK, DV), F32), jax.ShapeDtypeStruct((nb, D), F32),
                   jax.ShapeDtypeStruct((nb * AH, LANES), F32),
                   jax.ShapeDtypeStruct((nb, KVH, HD, WIN), F32), jax.ShapeDtypeStruct((nb, KVH, HD, WIN), F32)),
        name="sample_state", compiler_params=vmem,
    )(z, qs, wts["sinks_col"], c0, k0, v0)

    post_in = [z, qc, os_, wts["m_norm_g"], wts["w_out_a"], wts["w_out_b"]]
    x1 = pl.pallas_call(
        functools.partial(_sample_post_kernel, nb=nb, tb=tb),
        grid=(1,),
        in_specs=[full(a.shape) for a in post_in],
        out_specs=full((nb, D)),
        out_shape=jax.ShapeDtypeStruct((nb, D), F32),
        name="sample_post", compiler_params=vmem,
    )(*post_in)
    return x1, c_new, n_new, m_new, conv_new, k_new, v_new


def _regroup_kernel(wt_ref, w1_ref, w2_ref, w3_ref, wif_ref, wift_ref):
    cols = lambda a, b: wt_ref[a:b, :].T.astype(BF16)
    w1_ref[:, 0:2 * D] = cols(O_QK, O_MI)
    w1_ref[:, 2 * D:W1_W] = cols(O_MO, O_MO + W_HALF)
    w2_ref[:, 0:W_HALF + D] = cols(O_MO + W_HALF, O_AK)
    w2_ref[:, W_HALF + D:W2_W] = cols(O_GM, O_GA)
    w3_ref[:, 0:D] = cols(O_GA, IN_WIDTH)
    w3_ref[:, D:W3_W] = cols(O_AK, O_GM)
    gates_t = wt_ref[O_MI:O_MO, :]
    rb = gates_t.shape[1]
    wift_ref[...] = gates_t.astype(BF16)
    wif_ref[...] = jnp.concatenate([gates_t, jnp.zeros((LANES - 2 * MH, rb), F32)], axis=0).T.astype(BF16)


def _regroup_w_in(wt, *, rb=128):
    assert D % rb == 0 and rb == LANES
    return pl.pallas_call(
        _regroup_kernel, grid=(D // rb,),
        in_specs=[pl.BlockSpec((IN_WIDTH, rb), lambda i: (0, i))],
        out_specs=(pl.BlockSpec((rb, W1_W), lambda i: (i, 0)), pl.BlockSpec((rb, W2_W), lambda i: (i, 0)),
                   pl.BlockSpec((rb, W3_W), lambda i: (i, 0)), pl.BlockSpec((rb, LANES), lambda i: (i, 0)),
                   pl.BlockSpec((2 * MH, rb), lambda i: (0, i))),
        out_shape=(jax.ShapeDtypeStruct((D, W1_W), BF16), jax.ShapeDtypeStruct((D, W2_W), BF16),
                   jax.ShapeDtypeStruct((D, W3_W), BF16), jax.ShapeDtypeStruct((D, LANES), BF16),
                   jax.ShapeDtypeStruct((2 * MH, D), BF16)),
        name="regroup_w_in",
        compiler_params=pltpu.CompilerParams(dimension_semantics=("arbitrary",)),
    )(wt)


def _prep_weights(ln_in_g, ln_in_b, w_in, b_igate, b_fgate, conv_w, conv_b, m_norm_g, attn_sinks, w_out,
                  ln1_g, ln1_b, w_gate_up, w_down, ln2_g, ln2_b, w_ple, w_ple_gate):
    row = lambda a: a.reshape(1, -1).astype(F32)
    w1, w2, w3, w_if, w_ift = _regroup_w_in(jnp.transpose(w_in[0], (1, 0)))
    b_if = jnp.pad(jnp.concatenate([b_igate[0], b_fgate[0]]), (0, LANES - 2 * MH))
    wo = w_out[0].astype(BF16)
    return {
        "ln_in_g": row(ln_in_g), "ln_in_b": row(ln_in_b),
        "w1": w1, "w2": w2, "w3": w3,
        "w_if": w_if, "w_ift": w_ift,
        "b_if": b_if.reshape(1, LANES), "b_ift": b_if[:2 * MH].reshape(2 * MH, 1),
        "conv_w": conv_w[0], "conv_b": row(conv_b[0]), "m_norm_g": row(m_norm_g[0]),
        "sinks": attn_sinks[0], "sinks_col": attn_sinks[0].reshape(AH, 1),
        "w_out_a": wo[:, :W_HALF], "w_out_b": wo[:, W_HALF:], "ln1_g": row(ln1_g[0]), "ln1_b": row(ln1_b[0]),
        "w_gu": w_gate_up[0].astype(BF16),
        "w_d": w_down[0].astype(BF16), "w_pg": w_ple_gate[0].astype(BF16), "w_p": w_ple[0].astype(BF16),
        "ln2_g": row(ln2_g[0]), "ln2_b": row(ln2_b[0]),
    }


def _prompt_path(x, p, wts, *, tm=256, lc=128, tmf=1024):
    B, S, _ = x.shape
    x1, c, n, m, conv, k, v = _prompt_mixer(x, wts, tm=tm, lc=lc)
    y = _ffn(x1.reshape(B * S, D), p.reshape(B * S, PD), wts, tm=tmf, sub=tmf // 4).reshape(B, S, D)
    return (y, c[None], n[None], m[None, :, :MH, 0], conv[None],
            k.reshape(1, B, WIN, KVH, HD), v.reshape(1, B, WIN, KVH, HD))


def _sample_path(x, p, c0, n0, m0, conv0, k0, v0, wts, *, tb=8):
    nb = x.shape[0]
    x1, c, n, m, conv, k, v = _sample_mixer(
        x.reshape(nb, D), c0[0], n0[0].reshape(nb, MH * DK), m0[0],
        jnp.transpose(conv0[0], (1, 0, 2)), jnp.transpose(k0[0], (0, 2, 3, 1)),
        jnp.transpose(v0[0], (0, 2, 3, 1)), wts, tb=tb)
    y = _ffn(x1, p.reshape(nb, PD), wts, tm=nb, sub=nb).reshape(nb, 1, D)
    return (y, c[None], n.reshape(1, nb, MH, DK), m[None], jnp.transpose(conv, (1, 0, 2))[None],
            jnp.transpose(k, (0, 3, 1, 2))[None], jnp.transpose(v, (0, 3, 1, 2))[None])


def kernel(x_prompt, x_sample, state_mlstm_C, state_mlstm_n, state_mlstm_m, state_conv, cache_win_k, cache_win_v, p_prompt, p_sample, ln_in_g, ln_in_b, w_in, b_igate, b_fgate, conv_w, conv_b, m_norm_g, attn_sinks, w_out, ln1_g, ln1_b, w_gate_up, w_down, ln2_g, ln2_b, w_ple, w_ple_gate):
    wts = _prep_weights(ln_in_g, ln_in_b, w_in, b_igate, b_fgate, conv_w, conv_b, m_norm_g, attn_sinks,
                        w_out, ln1_g, ln1_b, w_gate_up, w_down, ln2_g, ln2_b, w_ple, w_ple_gate)
    yp, c_p, n_p, m_p, conv_p, k_p, v_p = _prompt_path(x_prompt, p_prompt[0], wts)
    ys, c_s, n_s, m_s, conv_s, k_s, v_s = _sample_path(
        x_sample, p_sample[0], state_mlstm_C, state_mlstm_n, state_mlstm_m, state_conv,
        cache_win_k, cache_win_v, wts)
    return (yp, ys, c_p, n_p, m_p, conv_p, k_p, v_p, c_s, n_s, m_s, conv_s, k_s, v_s)
```

```python
import functools

import jax
import jax.numpy as jnp
from jax import lax
from jax.experimental import pallas as pl
from jax.experimental.pallas import tpu as pltpu

F32 = jnp.float32
BF16 = jnp.bfloat16

D = 1024
MH = 4
DV = D // MH
DK = DV // 2
CONV_W = 4
QK_CH = 2 * MH * DK
AH = 16
KVH = 2
HD = D // AH
GROUP = AH // KVH
WIN = 128
DFF = 2816
PD = 256
LN_EPS = 1e-5
RMS_EPS = 1e-6
ALPHA = 2.0 ** 0.25
LOG2E = 1.4426950408889634
NEG_INF = float("-inf")

LANES = 128
SUBLANES = 8
MXU_DIM = 256
VMEM_LIMIT = 56 * 1024 * 1024

O_QK = 0
O_MV = O_QK + QK_CH
O_MI = O_MV + D
O_MF = O_MI + MH
O_MO = O_MF + MH
O_AQ = O_MO + D
O_AK = O_AQ + D
O_AV = O_AK + KVH * HD
O_GM = O_AV + KVH * HD
O_GA = O_GM + D
IN_WIDTH = O_GA + D

W_HALF = D // 2
W1_W = 2 * D + W_HALF
W2_W = 2 * D + W_HALF
W3_W = D + 2 * KVH * HD
_WCOL = {"qk": ("w1", 0, D), "mv": ("w1", D, 2 * D), "mo_lo": ("w1", 2 * D, W1_W),
         "mo_hi": ("w2", 0, W_HALF), "aq": ("w2", W_HALF, W_HALF + D), "gm": ("w2", W_HALF + D, W2_W),
         "ga": ("w3", 0, D), "akv": ("w3", D, W3_W)}


def _dot(a, b):
    return jnp.dot(a, b, preferred_element_type=F32)


def _dot_nt(a, b):
    return lax.dot_general(a, b, (((1,), (1,)), ((), ())), preferred_element_type=F32)


def _dot_tn(a, b):
    return lax.dot_general(a, b, (((0,), (0,)), ((), ())), preferred_element_type=F32)


def _proj(xb, w, name):
    blk, a, b = _WCOL[name]
    return _dot(xb, w[blk][:, a:b])


def _layer_norm(x, g, b):
    mu = jnp.mean(x, -1, keepdims=True)
    xc = x - mu
    var = jnp.mean(xc * xc, -1, keepdims=True)
    return xc * lax.rsqrt(var + LN_EPS) * g + b


def _sigmoid(x):
    return 1.0 / (1.0 + jnp.exp(-x))


def _log_sigmoid(x):
    return jnp.minimum(x, 0.0) - jnp.log(1.0 + jnp.exp(-jnp.abs(x)))


def _const_spec(shape):
    zeros = (0,) * len(shape)
    return pl.BlockSpec(shape, lambda i: zeros, pipeline_mode=pl.Buffered(1))


def _dup_halves(x, lo):
    xr = pltpu.roll(x, HD, 1)
    return jnp.where(lo, x, xr), jnp.where(lo, xr, x)


_PROJ_NAMES = ("xn", "zqk", "vm", "smo", "aq", "k2", "v2", "kvl", "sga", "gcol", "grow", "brow")
_MIXER_W_NAMES = ("lng", "lnb", "w1", "w2", "w3", "wift", "bift", "cw", "cb", "mng", "sink", "wout_a", "wout_b")


def _mixer_project(x, slot, w, buf, *, tm, lc):
    xn = _layer_norm(x, w["lng"][...], w["lnb"][...])
    xb = xn.astype(BF16)
    buf["xn"][slot] = xn
    buf["zqk"][slot] = _proj(xb, w, "qk")
    buf["vm"][slot] = _proj(xb, w, "mv").astype(BF16)
    mo = jnp.concatenate([_proj(xb, w, "mo_lo"), _proj(xb, w, "mo_hi")], axis=1)
    buf["smo"][slot] = _sigmoid(mo) * w["mng"][...] * _sigmoid(_proj(xb, w, "gm"))
    aq = _proj(xb, w, "aq") * (HD ** -0.5 * LOG2E)
    lo_d = (lax.broadcasted_iota(jnp.int32, (tm, D), 1) & (LANES - 1)) < HD
    aqm = (jnp.where(lo_d, aq, 0.0).astype(BF16), jnp.where(lo_d, 0.0, aq).astype(BF16))
    for j in range(tm // WIN):
        for h in range(AH):
            buf["aq"][slot, j, h // GROUP, (h % GROUP) * WIN:(h % GROUP + 1) * WIN, :] = (
                aqm[h % 2][j * WIN:(j + 1) * WIN, (h // 2) * LANES:(h // 2 + 1) * LANES])
    akv = _proj(xb, w, "akv")
    lo = lax.broadcasted_iota(jnp.int32, (tm, LANES), 1) < HD
    k2 = _dup_halves(akv[:, 0:LANES], lo)
    v2 = _dup_halves(akv[:, LANES:2 * LANES], lo)
    for g in range(KVH):
        buf["k2"][slot, g] = k2[g].astype(BF16)
        buf["v2"][slot, g] = v2[g].astype(BF16)
    buf["kvl"][slot] = akv[tm - WIN:tm, :]
    buf["sga"][slot] = _sigmoid(_proj(xb, w, "ga"))
    grow = _dot_nt(w["wift"][...], xb) + w["bift"][...]
    buf["grow"][slot] = grow
    lsrow = _log_sigmoid(grow)
    lane = lax.broadcasted_iota(jnp.int32, (SUBLANES, lc), 1)
    pad = jnp.zeros((lc - 2 * SUBLANES, lc), F32)
    for c in range(tm // lc):
        r0 = c * lc
        seg = lsrow[:, r0:r0 + lc]
        sh = 1
        while sh < lc:
            seg = seg + jnp.where(lane >= sh, pltpu.roll(seg, sh, 1), 0.0)
            sh *= 2
        buf["brow"][slot, :, r0:r0 + lc] = seg
        buf["gcol"][slot, r0:r0 + lc, :] = jnp.concatenate([grow[:, r0:r0 + lc], seg, pad], axis=0).T


def _mixer_consume(slot, first, w, buf, st, x1_ref, row0, *, tm, lc):
    c_ref, n_ref, m_ref, ctail, kvc, ym_s, mix_s = st
    x1_ref[0, row0:row0 + tm, :] = ALPHA * buf["xn"][slot]
    z = buf["zqk"][slot]
    tail = ctail[...]
    ctail[...] = z[tm - SUBLANES:tm, :]
    cw = w["cw"][...]
    rows8 = lax.broadcasted_iota(jnp.int32, (SUBLANES, QK_CH), 0)
    y = cw[CONV_W - 1:CONV_W] * z + w["cb"][...]
    for sh in range(1, CONV_W):
        zr = pltpu.roll(z, sh, 0)
        head = jnp.where(rows8 < sh, pltpu.roll(tail, sh, 0), zr[0:SUBLANES])
        zr = jnp.concatenate([head, zr[SUBLANES:]], axis=0)
        y = y + cw[CONV_W - 1 - sh:CONV_W - sh] * zr
    qk = y * _sigmoid(y)

    ri = lax.broadcasted_iota(jnp.int32, (lc, lc), 0)
    ci = lax.broadcasted_iota(jnp.int32, (lc, lc), 1)
    causal = ci <= ri
    for c in range(tm // lc):
        r0 = c * lc
        for h in range(MH):
            qh = qk[r0:r0 + lc, h * DK:(h + 1) * DK]
            kh = qk[r0:r0 + lc, MH * DK + h * DK:MH * DK + (h + 1) * DK] * (DK ** -0.5)
            vh = buf["vm"][slot, r0:r0 + lc, h * DV:(h + 1) * DV]
            qb = qh.astype(BF16)
            bc = buf["gcol"][slot, r0:r0 + lc, SUBLANES + MH + h:SUBLANES + MH + h + 1]
            br = buf["brow"][slot, MH + h:MH + h + 1, r0:r0 + lc]
            li_c = buf["gcol"][slot, r0:r0 + lc, h:h + 1]
            li_r = buf["grow"][slot, h:h + 1, r0:r0 + lc]
            m_prev = m_ref[0, h:h + 1, 0:1]
            c_prev = c_ref[0, h]
            n_prev = n_ref[0, h:h + 1, :]

            dmat = jnp.where(causal, bc - br + li_r, NEG_INF)
            m_inter = bc + m_prev
            m_t = jnp.maximum(m_inter, jnp.max(dmat, -1, keepdims=True))
            wgt = jnp.exp(dmat - m_t) * _dot_nt(qb, kh.astype(BF16))
            inter = jnp.exp(m_inter - m_t)
            num = _dot(jnp.concatenate([wgt.astype(BF16), (inter * qh).astype(BF16)], axis=1),
                       jnp.concatenate([vh, c_prev.astype(BF16)], axis=0))
            den = jnp.sum(wgt, -1, keepdims=True) + inter * jnp.sum(qh * n_prev, -1, keepdims=True)
            denom = jnp.maximum(jnp.abs(den), jnp.exp(-m_t))
            hh = num / denom
            hn = hh * lax.rsqrt(jnp.mean(hh * hh, -1, keepdims=True) + RMS_EPS)
            ym_s[r0:r0 + lc, h * DV:(h + 1) * DV] = buf["smo"][slot, r0:r0 + lc, h * DV:(h + 1) * DV] * hn

            m_new = m_t[lc - 1:lc, :]
            bc_last = bc[lc - 1:lc, :]
            decay = jnp.exp(bc_last + m_prev - m_new)
            ks = kh * jnp.exp(bc_last - bc + li_c - m_new)
            c_ref[0, h] = decay * c_prev + _dot_tn(ks.astype(BF16), vh)
            n_ref[0, h:h + 1, :] = decay * n_prev + jnp.sum(ks, 0, keepdims=True)
            m_ref[0, h:h + 1, :] = jnp.broadcast_to(m_new, (1, LANES))

    lo_q = lax.broadcasted_iota(jnp.int32, (WIN, LANES), 1) < HD
    qi = lax.broadcasted_iota(jnp.int32, (WIN, 2 * WIN), 0)
    kj = lax.broadcasted_iota(jnp.int32, (WIN, 2 * WIN), 1)
    band = (kj >= qi) & (kj <= qi + WIN)
    ones_v = jnp.ones((2 * WIN, LANES), BF16)
    for j in range(tm // WIN):
        r0 = j * WIN
        if j == 0 and first is not None:
            mask = band & (kj >= jnp.where(first, WIN, 0))
        else:
            mask = band
        for g in range(KVH):
            if j == 0:
                kprev, vprev = kvc[g], kvc[KVH + g]
            else:
                kprev, vprev = buf["k2"][slot, g, r0 - WIN:r0, :], buf["v2"][slot, g, r0 - WIN:r0, :]
            kcat = jnp.concatenate([kprev, buf["k2"][slot, g, r0:r0 + WIN, :]], axis=0)
            vcat = jnp.concatenate([vprev, buf["v2"][slot, g, r0:r0 + WIN, :]], axis=0)
            sc_all = _dot_nt(buf["aq"][slot, j, g], kcat)
            es, ds = [], []
            for hh in range(GROUP):
                sink = w["sink"][g * GROUP + hh] * LOG2E
                sc = jnp.where(mask, sc_all[hh * WIN:(hh + 1) * WIN, :], NEG_INF)
                mx = jnp.maximum(jnp.max(sc, -1, keepdims=True), sink)
                es.append(jnp.exp2(sc - mx).astype(BF16))
                ds.append(jnp.exp2(sink - mx))
            o_all = _dot(jnp.concatenate(es, axis=0), jnp.concatenate([vcat, ones_v], axis=1))
            for pp in range(GROUP // 2):
                pj = g * GROUP // 2 + pp
                halves = []
                for hh in (2 * pp, 2 * pp + 1):
                    oh = o_all[hh * WIN:(hh + 1) * WIN, :]
                    halves.append(oh[:, 0:LANES] / (oh[:, LANES:2 * LANES] + ds[hh]))
                cols = slice(pj * LANES, (pj + 1) * LANES)
                ya = jnp.where(lo_q, halves[0], halves[1])
                mix_s[r0:r0 + WIN, cols] = (ym_s[r0:r0 + WIN, cols]
                                            + buf["sga"][slot, r0:r0 + WIN, cols] * ya).astype(BF16)
    for g in range(KVH):
        kvc[g] = buf["k2"][slot, g, tm - WIN:tm, :]
        kvc[KVH + g] = buf["v2"][slot, g, tm - WIN:tm, :]

    mixed = mix_s[...]
    r = jnp.concatenate([_dot(mixed, w["wout_a"][...]), _dot(mixed, w["wout_b"][...])], axis=1)
    x1_ref[0, row0:row0 + tm, :] = x1_ref[0, row0:row0 + tm, :] + r


def _mixer_kernel(*refs, tm, lc, npb, tps):
    nw = len(_MIXER_W_NAMES)
    x0_ref = refs[0]
    xt_refs = refs[1:1 + tps]
    w = dict(zip(_MIXER_W_NAMES, refs[1 + tps:1 + tps + nw]))
    x1_ref, c_ref, n_ref, m_ref, conv_ref, kp_ref, vp_ref = refs[1 + tps + nw:8 + tps + nw]
    scratch = refs[8 + tps + nw:]
    buf = dict(zip(_PROJ_NAMES, scratch[:len(_PROJ_NAMES)]))
    ctail, kvc, ym_s, mix_s = scratch[len(_PROJ_NAMES):]
    st = (c_ref, n_ref, m_ref, ctail, kvc, ym_s, mix_s)
    k = pl.program_id(0)
    first = (tps * k) % npb == 0

    @pl.when(k == 0)
    def _prologue():
        _mixer_project(x0_ref[0], 0, w, buf, tm=tm, lc=lc)

    @pl.when(first)
    def _init():
        c_ref[...] = jnp.zeros_like(c_ref)
        n_ref[...] = jnp.zeros_like(n_ref)
        m_ref[...] = jnp.zeros_like(m_ref)
        ctail[...] = jnp.zeros_like(ctail)
        kvc[...] = jnp.zeros_like(kvc)

    for i in range(tps):
        _mixer_project(xt_refs[i][0], (i + 1) % 2, w, buf, tm=tm, lc=lc)
        _mixer_consume(i % 2, first if i == 0 else None, w, buf, st, x1_ref, i * tm, tm=tm, lc=lc)

    @pl.when((tps * k + tps - 1) % npb == npb - 1)
    def _state_out():
        conv_ref[0] = ctail[SUBLANES - (CONV_W - 1):SUBLANES, :]
        kp_ref[0] = buf["kvl"][(tps - 1) % 2, :, 0:LANES]
        vp_ref[0] = buf["kvl"][(tps - 1) % 2, :, LANES:2 * LANES]


def _prompt_mixer(x, wts, *, tm=256, lc=128, tps=2):
    B, S, _ = x.shape
    npb = S // tm
    assert S % tm == 0 and tps % 2 == 0 and npb % tps == 0 and tm % lc == 0 and tm % WIN == 0
    nt = B * npb
    hpb = npb // tps
    cs = _const_spec

    def tile_map(off):
        def imap(k):
            t = jnp.minimum(tps * k + off, nt - 1)
            return (t // npb, t % npb, 0)
        return imap

    in_specs = [
        pl.BlockSpec((1, tm, D), lambda k: (0, 0, 0), pipeline_mode=pl.Buffered(1)),
        *[pl.BlockSpec((1, tm, D), tile_map(i + 1)) for i in range(tps)],
        cs((1, D)), cs((1, D)),
        cs((D, W1_W)), cs((D, W2_W)), cs((D, W3_W)), cs((SUBLANES, D)), cs((SUBLANES, 1)),
        cs((CONV_W, QK_CH)), cs((1, QK_CH)), cs((1, D)),
        pl.BlockSpec(memory_space=pltpu.SMEM),
        cs((D, W_HALF)), cs((D, W_HALF)),
    ]
    out_shape = (
        jax.ShapeDtypeStruct((B, S, D), F32),
        jax.ShapeDtypeStruct((B, MH, DK, DV), F32),
        jax.ShapeDtypeStruct((B, MH, DK), F32),
        jax.ShapeDtypeStruct((B, SUBLANES, LANES), F32),
        jax.ShapeDtypeStruct((B, CONV_W - 1, QK_CH), F32),
        jax.ShapeDtypeStruct((B, WIN, KVH * HD), F32),
        jax.ShapeDtypeStruct((B, WIN, KVH * HD), F32),
    )
    out_specs = (
        pl.BlockSpec((1, tps * tm, D), lambda k: (k // hpb, k % hpb, 0)),
        pl.BlockSpec((1, MH, DK, DV), lambda k: (k // hpb, 0, 0, 0)),
        pl.BlockSpec((1, MH, DK), lambda k: (k // hpb, 0, 0)),
        pl.BlockSpec((1, SUBLANES, LANES), lambda k: (k // hpb, 0, 0)),
        pl.BlockSpec((1, CONV_W - 1, QK_CH), lambda k: (k // hpb, 0, 0)),
        pl.BlockSpec((1, WIN, KVH * HD), lambda k: (k // hpb, 0, 0)),
        pl.BlockSpec((1, WIN, KVH * HD), lambda k: (k // hpb, 0, 0)),
    )
    proj_scratch = {
        "xn": pltpu.VMEM((2, tm, D), F32), "zqk": pltpu.VMEM((2, tm, QK_CH), F32),
        "vm": pltpu.VMEM((2, tm, D), BF16), "smo": pltpu.VMEM((2, tm, D), F32),
        "aq": pltpu.VMEM((2, tm // WIN, KVH, GROUP * WIN, LANES), BF16),
        "k2": pltpu.VMEM((2, KVH, tm, LANES), BF16), "v2": pltpu.VMEM((2, KVH, tm, LANES), BF16),
        "kvl": pltpu.VMEM((2, WIN, 2 * LANES), F32),
        "sga": pltpu.VMEM((2, tm, D), F32),
        "gcol": pltpu.VMEM((2, tm, LANES), F32), "grow": pltpu.VMEM((2, SUBLANES, tm), F32),
        "brow": pltpu.VMEM((2, SUBLANES, tm), F32),
    }
    scratch = [proj_scratch[n] for n in _PROJ_NAMES] + [
        pltpu.VMEM((SUBLANES, QK_CH), F32),
        pltpu.VMEM((2 * KVH, WIN, LANES), BF16),
        pltpu.VMEM((tm, D), F32),
        pltpu.VMEM((tm, D), BF16),
    ]
    return pl.pallas_call(
        functools.partial(_mixer_kernel, tm=tm, lc=lc, npb=npb, tps=tps),
        grid=(nt // tps,), in_specs=in_specs, out_specs=out_specs, out_shape=out_shape,
        scratch_shapes=scratch, name="prompt_mixer",
        compiler_params=pltpu.CompilerParams(dimension_semantics=("arbitrary",), vmem_limit_bytes=VMEM_LIMIT),
    )(*([x] * (tps + 1)), wts["ln_in_g"], wts["ln_in_b"], wts["w1"], wts["w2"], wts["w3"], wts["w_ift"], wts["b_ift"],
      wts["conv_w"], wts["conv_b"], wts["m_norm_g"], wts["sinks"], wts["w_out_a"], wts["w_out_b"])


FFN_CHUNKS = ((0, 6 * MXU_DIM), (6 * MXU_DIM, DFF))


def _ffn_kernel(x_ref, p_ref, l1g_ref, l1b_ref, wgu_ref, wd_ref, wpg_ref, wp_ref, g_ref, b_ref, o_ref, *, sub):
    n = x_ref.shape[0] // sub

    def pre(j):
        x = _layer_norm(x_ref[j * sub:(j + 1) * sub, :], l1g_ref[...], l1b_ref[...])
        return x, x.astype(BF16)

    def body(j, x, xb):
        acc = ALPHA * x + (_sigmoid(_dot(xb, wpg_ref[...]))
                           * _dot(p_ref[j * sub:(j + 1) * sub, :].astype(BF16), wp_ref[...]))
        for c0, c1 in FFN_CHUNKS:
            g = _dot(xb, wgu_ref[:, c0:c1])
            u = _dot(xb, wgu_ref[:, DFF + c0:DFF + c1])
            hcat = (g * _sigmoid(g) * u).astype(BF16)
            acc = acc + _dot(hcat, wd_ref[c0:c1, :])
        return acc

    def post(j, acc):
        o_ref[j * sub:(j + 1) * sub, :] = _layer_norm(acc, g_ref[...], b_ref[...])

    cur = pre(0)
    acc_prev = None
    for j in range(n):
        nxt = pre(j + 1) if j + 1 < n else None
        acc = body(j, *cur)
        if acc_prev is not None:
            post(j - 1, acc_prev)
        acc_prev, cur = acc, nxt
    post(n - 1, acc_prev)


def _ffn(x, p, wts, *, tm, sub):
    N = x.shape[0]
    assert N % tm == 0 and tm % sub == 0
    cs = _const_spec
    return pl.pallas_call(
        functools.partial(_ffn_kernel, sub=sub),
        grid=(N // tm,),
        in_specs=[pl.BlockSpec((tm, D), lambda i: (i, 0)), pl.BlockSpec((tm, PD), lambda i: (i, 0)),
                  cs((1, D)), cs((1, D)),
                  cs((D, 2 * DFF)), cs((DFF, D)), cs((D, D)), cs((PD, D)), cs((1, D)), cs((1, D))],
        out_specs=pl.BlockSpec((tm, D), lambda i: (i, 0)),
        out_shape=jax.ShapeDtypeStruct((N, D), F32),
        name="ffn",
        compiler_params=pltpu.CompilerParams(dimension_semantics=("arbitrary",), vmem_limit_bytes=VMEM_LIMIT),
    )(x, p, wts["ln1_g"], wts["ln1_b"], wts["w_gu"], wts["w_d"], wts["w_pg"], wts["w_p"],
      wts["ln2_g"], wts["ln2_b"])


Z_V = 0
Z_MO = Z_V + D
Z_GM = Z_MO + D
Z_GA = Z_GM + D
Z_XN = Z_GA + D
Z_Q = Z_XN + D
Z_KW = Z_Q + MH * DK
Z_SC = Z_KW + MH * DK
Z_KN = Z_SC + LANES
Z_VN = Z_KN + LANES
Z_W = Z_VN + LANES


def _sample_pre_kernel(x_ref, lng_ref, lnb_ref, w1_ref, w2_ref, w3_ref, wif_ref, bif_ref,
                       cw_ref, cb_ref, cs_ref, n_ref, m_ref,
                       z_ref, qs_ref, convo_ref, no_ref, mo_ref, *, nb, tb):
    w = {"w1": w1_ref, "w2": w2_ref, "w3": w3_ref}
    xn = _layer_norm(x_ref[...], lng_ref[...], lnb_ref[...])
    xb = xn.astype(BF16)
    zqk = _proj(xb, w, "qk")
    cw = cw_ref[...]
    y = cw[3:4] * zqk + cw[2:3] * cs_ref[2] + cw[1:2] * cs_ref[1] + cw[0:1] * cs_ref[0] + cb_ref[...]
    qk = y * _sigmoid(y)
    convo_ref[0] = cs_ref[1]
    convo_ref[1] = cs_ref[2]
    convo_ref[2] = zqk

    gcol = _dot(xb, wif_ref[...]) + bif_ref[...]
    ig = gcol[:, 0:MH]
    m_inter = _log_sigmoid(gcol[:, MH:2 * MH]) + m_ref[...]
    m_t = jnp.maximum(m_inter, ig)
    inter = jnp.exp(m_inter - m_t)
    wi = jnp.exp(ig - m_t)
    enm = jnp.exp(-m_t)
    mo_ref[...] = m_t

    z_ref[:, Z_V:Z_V + D] = _proj(xb, w, "mv")
    z_ref[:, Z_MO:Z_MO + W_HALF] = _proj(xb, w, "mo_lo")
    z_ref[:, Z_MO + W_HALF:Z_MO + D] = _proj(xb, w, "mo_hi")
    z_ref[:, Z_GM:Z_GM + D] = _proj(xb, w, "gm")
    z_ref[:, Z_GA:Z_GA + D] = _proj(xb, w, "ga")
    z_ref[:, Z_XN:Z_XN + D] = xn
    z_ref[:, Z_Q:Z_Q + MH * DK] = qk[:, 0:MH * DK]
    sc = jnp.zeros((nb, LANES), F32)
    lane = lax.broadcasted_iota(jnp.int32, (nb, LANES), 1)
    for h in range(MH):
        qh = qk[:, h * DK:(h + 1) * DK]
        kh = qk[:, MH * DK + h * DK:MH * DK + (h + 1) * DK] * (DK ** -0.5)
        nh = n_ref[:, h * DK:(h + 1) * DK]
        wv = wi[:, h:h + 1] * jnp.sum(qh * kh, -1, keepdims=True)
        den = wv + inter[:, h:h + 1] * jnp.sum(qh * nh, -1, keepdims=True)
        denom = jnp.maximum(jnp.abs(den), enm[:, h:h + 1])
        kw = wi[:, h:h + 1] * kh
        z_ref[:, Z_KW + h * DK:Z_KW + (h + 1) * DK] = kw
        no_ref[:, h * DK:(h + 1) * DK] = inter[:, h:h + 1] * nh + kw
        sc = jnp.where(lane == h, inter[:, h:h + 1], sc)
        sc = jnp.where(lane == MH + h, wv, sc)
        sc = jnp.where(lane == 2 * MH + h, denom, sc)
    z_ref[:, Z_SC:Z_SC + LANES] = sc

    akv = _proj(xb, w, "akv")
    z_ref[:, Z_KN:Z_KN + LANES] = akv[:, 0:LANES]
    z_ref[:, Z_VN:Z_VN + LANES] = akv[:, LANES:2 * LANES]
    aq = _proj(xb, w, "aq") * (HD ** -0.5)
    lo = lax.broadcasted_iota(jnp.int32, (nb, LANES), 1) < HD
    for h in range(AH):
        pair = aq[:, (h // 2) * LANES:(h // 2 + 1) * LANES]
        qs_ref[pl.ds(h, nb, stride=AH), :] = jnp.where(lo, pair if h % 2 == 0 else pltpu.roll(pair, HD, 1), 0.0)


def _sample_state_kernel(z_ref, qs_ref, sink_ref, c_ref, ck_ref, cv_ref,
                         co_ref, qc_ref, os_ref, ko_ref, vo_ref, *, tb):
    lo = lax.broadcasted_iota(jnp.int32, (WIN, LANES), 1) < HD
    newest = lax.broadcasted_iota(jnp.int32, (HD, WIN), 1) == WIN - 1
    sinks = sink_ref[...]
    q_all = z_ref[:, Z_Q:Z_Q + MH * DK]
    kw_all = z_ref[:, Z_KW:Z_KW + MH * DK]
    v_all = z_ref[:, Z_V:Z_V + D]
    sc = z_ref[:, Z_SC:Z_SC + LANES]
    kn_all = z_ref[:, Z_KN:Z_KN + LANES]
    vn_all = z_ref[:, Z_VN:Z_VN + LANES]
    kn2_all = _dup_halves(kn_all, lo[0:tb])
    vn2_all = _dup_halves(vn_all, lo[0:tb])
    rpad = jnp.zeros((LANES - tb, LANES), F32)
    knt = [jnp.concatenate([a, rpad], axis=0).T for a in kn2_all]
    vnt = [jnp.concatenate([a, rpad], axis=0).T for a in vn2_all]
    lane_sel = lax.broadcasted_iota(jnp.int32, (DK, LANES), 1)
    for h in range(MH):
        q_h = q_all[:, h * DK:(h + 1) * DK].astype(BF16)
        kwt_h = jnp.concatenate([kw_all[:, h * DK:(h + 1) * DK], rpad], axis=0).T
        v_h = jnp.concatenate([v_all[:, h * DV:(h + 1) * DV], jnp.zeros((LANES - tb, DV), F32)], axis=0).astype(BF16)
        for i in range(tb):
            cb = c_ref[i, h]
            qc = _dot(q_h, cb.astype(BF16))
            qc_ref[i:i + 1, h * DV:(h + 1) * DV] = qc[i:i + 1]
            outer = _dot(jnp.where(lane_sel == i, kwt_h, 0.0).astype(BF16), v_h)
            co_ref[i, h] = sc[i:i + 1, h:h + 1] * cb + outer

    q_rows = qs_ref[:, 0:HD]
    kn_rows = jnp.concatenate([jnp.broadcast_to(kn2_all[g][i:i + 1, 0:HD], (GROUP, HD))
                               for i in range(tb) for g in range(KVH)], axis=0)
    vn_rows = jnp.concatenate([jnp.broadcast_to(vn2_all[g][i:i + 1, 0:HD], (GROUP, HD))
                               for i in range(tb) for g in range(KVH)], axis=0)
    sink_rows = jnp.concatenate([sinks] * tb, axis=0)
    q_bf = q_rows.astype(BF16)
    s = jnp.concatenate([_dot(q_bf[(i * KVH + g) * GROUP:(i * KVH + g + 1) * GROUP], ck_ref[i, g].astype(BF16))
                         for i in range(tb) for g in range(KVH)], axis=0)
    s_self = jnp.sum(q_rows * kn_rows, -1, keepdims=True)
    mx = jnp.maximum(jnp.maximum(jnp.max(s, -1, keepdims=True), s_self), sink_rows)
    e = jnp.exp(s - mx)
    es = jnp.exp(s_self - mx)
    dsum = jnp.sum(e, -1, keepdims=True) + es + jnp.exp(sink_rows - mx)
    e_bf = e.astype(BF16)
    pv = jnp.concatenate([_dot_nt(e_bf[(i * KVH + g) * GROUP:(i * KVH + g + 1) * GROUP], cv_ref[i, g].astype(BF16))
                          for i in range(tb) for g in range(KVH)], axis=0)
    o = (pv + es * vn_rows) / dsum
    os_ref[...] = jnp.concatenate([o, jnp.zeros((tb * AH, LANES - HD), F32)], axis=1)
    for i in range(tb):
        for g in range(KVH):
            kcol = jnp.broadcast_to(knt[g][0:HD, i:i + 1], (HD, WIN))
            vcol = jnp.broadcast_to(vnt[g][0:HD, i:i + 1], (HD, WIN))
            ko_ref[i, g] = jnp.where(newest, kcol, pltpu.roll(ck_ref[i, g], WIN - 1, 1))
            vo_ref[i, g] = jnp.where(newest, vcol, pltpu.roll(cv_ref[i, g], WIN - 1, 1))


def _sample_post_kernel(z_ref, qc_ref, os_ref, mng_ref, wouta_ref, woutb_ref, x1_ref, *, nb, tb):
    sc = z_ref[:, Z_SC:Z_SC + LANES]
    mng = mng_ref[...]
    lo = lax.broadcasted_iota(jnp.int32, (nb, LANES), 1) < HD
    ym = []
    for h in range(MH):
        v = z_ref[:, Z_V + h * DV:Z_V + (h + 1) * DV]
        num = sc[:, MH + h:MH + h + 1] * v + sc[:, h:h + 1] * qc_ref[:, h * DV:(h + 1) * DV]
        hh = num / sc[:, 2 * MH + h:2 * MH + h + 1]
        hn = hh * lax.rsqrt(jnp.mean(hh * hh, -1, keepdims=True) + RMS_EPS)
        ym.append(_sigmoid(z_ref[:, Z_MO + h * DV:Z_MO + (h + 1) * DV]) * hn * mng[:, h * DV:(h + 1) * DV])
    ym = jnp.concatenate(ym, axis=1)
    ya = []
    for pj in range(AH // 2):
        halves = []
        for par in range(2):
            halves.append(os_ref[pl.ds(2 * pj + par, nb, stride=AH), :])
        ya.append(jnp.where(lo, halves[0], pltpu.roll(halves[1], HD, 1)))
    ya = jnp.concatenate(ya, axis=1)
    mixed = _sigmoid(z_ref[:, Z_GM:Z_GM + D]) * ym + _sigmoid(z_ref[:, Z_GA:Z_GA + D]) * ya
    mixed = mixed.astype(BF16)
    r = jnp.concatenate([_dot(mixed, wouta_ref[...]), _dot(mixed, woutb_ref[...])], axis=1)
    x1_ref[...] = ALPHA * z_ref[:, Z_XN:Z_XN + D] + r


def _sample_mixer(x, c0, n0, m0, conv0, k0, v0, wts, *, tb=8):
    nb = x.shape[0]
    assert nb % tb == 0
    vmem = pltpu.CompilerParams(dimension_semantics=("arbitrary",), vmem_limit_bytes=VMEM_LIMIT)
    full = lambda shape: pl.BlockSpec(shape, lambda i: (0,) * len(shape))
    pre_in = [x, wts["ln_in_g"], wts["ln_in_b"], wts["w1"], wts["w2"], wts["w3"], wts["w_if"], wts["b_if"],
              wts["conv_w"], wts["conv_b"], conv0, n0, m0]
    z, qs, conv_new, n_new, m_new = pl.pallas_call(
        functools.partial(_sample_pre_kernel, nb=nb, tb=tb),
        grid=(1,),
        in_specs=[full(a.shape) for a in pre_in],
        out_specs=(full((nb, Z_W)), full((nb * AH, LANES)), full((CONV_W - 1, nb, QK_CH)),
                   full((nb, MH * DK)), full((nb, MH))),
        out_shape=(jax.ShapeDtypeStruct((nb, Z_W), F32), jax.ShapeDtypeStruct((nb * AH, LANES), F32),
                   jax.ShapeDtypeStruct((CONV_W - 1, nb, QK_CH), F32),
                   jax.ShapeDtypeStruct((nb, MH * DK), F32), jax.ShapeDtypeStruct((nb, MH), F32)),
        name="sample_pre", compiler_params=vmem,
    )(*pre_in)

    c_new, qc, os_, k_new, v_new = pl.pallas_call(
        functools.partial(_sample_state_kernel, tb=tb),
        grid=(nb // tb,),
        in_specs=[pl.BlockSpec((tb, Z_W), lambda i: (i, 0)),
                  pl.BlockSpec((tb * AH, LANES), lambda i: (i, 0)),
                  pl.BlockSpec((AH, 1), lambda i: (0, 0)),
                  pl.BlockSpec((tb, MH, DK, DV), lambda i: (i, 0, 0, 0)),
                  pl.BlockSpec((tb, KVH, HD, WIN), lambda i: (i, 0, 0, 0)),
                  pl.BlockSpec((tb, KVH, HD, WIN), lambda i: (i, 0, 0, 0))],
        out_specs=(pl.BlockSpec((tb, MH, DK, DV), lambda i: (i, 0, 0, 0)),
                   pl.BlockSpec((tb, D), lambda i: (i, 0)),
                   pl.BlockSpec((tb * AH, LANES), lambda i: (i, 0)),
                   pl.BlockSpec((tb, KVH, HD, WIN), lambda i: (i, 0, 0, 0)),
                   pl.BlockSpec((tb, KVH, HD, WIN), lambda i: (i, 0, 0, 0))),
        out_shape=(jax.ShapeDtypeStruct((nb, MH, DK, DV), F32), jax.ShapeDtypeStruct((nb, D), F32),
                   jax.ShapeDtypeStruct((nb * AH, LANES), F32),
                   jax.ShapeDtypeStruct((nb, KVH, HD, WIN), F32), jax.ShapeDtypeStruct((nb, KVH, HD, WIN), F32)),
        name="sample_state", compiler_params=vmem,
    )(z, qs, wts["sinks_col"], c0, k0, v0)

    post_in = [z, qc, os_, wts["m_norm_g"], wts["w_out_a"], wts["w_out_b"]]
    x1 = pl.pallas_call(
        functools.partial(_sample_post_kernel, nb=nb, tb=tb),
        grid=(1,),
        in_specs=[full(a.shape) for a in post_in],
        out_specs=full((nb, D)),
        out_shape=jax.ShapeDtypeStruct((nb, D), F32),
        name="sample_post", compiler_params=vmem,
    )(*post_in)
    return x1, c_new, n_new, m_new, conv_new, k_new, v_new


def _regroup_kernel(wt_ref, w1_ref, w2_ref, w3_ref, wif_ref, wift_ref):
    cols = lambda a, b: wt_ref[a:b, :].T.astype(BF16)
    w1_ref[:, 0:2 * D] = cols(O_QK, O_MI)
    w1_ref[:, 2 * D:W1_W] = cols(O_MO, O_MO + W_HALF)
    w2_ref[:, 0:W_HALF + D] = cols(O_MO + W_HALF, O_AK)
    w2_ref[:, W_HALF + D:W2_W] = cols(O_GM, O_GA)
    w3_ref[:, 0:D] = cols(O_GA, IN_WIDTH)
    w3_ref[:, D:W3_W] = cols(O_AK, O_GM)
    gates_t = wt_ref[O_MI:O_MO, :]
    rb = gates_t.shape[1]
    wift_ref[...] = gates_t.astype(BF16)
    wif_ref[...] = jnp.concatenate([gates_t, jnp.zeros((LANES - 2 * MH, rb), F32)], axis=0).T.astype(BF16)


def _regroup_w_in(wt, *, rb=256):
    assert D % rb == 0 and rb % LANES == 0
    return pl.pallas_call(
        _regroup_kernel, grid=(D // rb,),
        in_specs=[pl.BlockSpec((IN_WIDTH, rb), lambda i: (0, i))],
        out_specs=(pl.BlockSpec((rb, W1_W), lambda i: (i, 0)), pl.BlockSpec((rb, W2_W), lambda i: (i, 0)),
                   pl.BlockSpec((rb, W3_W), lambda i: (i, 0)), pl.BlockSpec((rb, LANES), lambda i: (i, 0)),
                   pl.BlockSpec((2 * MH, rb), lambda i: (0, i))),
        out_shape=(jax.ShapeDtypeStruct((D, W1_W), BF16), jax.ShapeDtypeStruct((D, W2_W), BF16),
                   jax.ShapeDtypeStruct((D, W3_W), BF16), jax.ShapeDtypeStruct((D, LANES), BF16),
                   jax.ShapeDtypeStruct((2 * MH, D), BF16)),
        name="regroup_w_in",
        compiler_params=pltpu.CompilerParams(dimension_semantics=("arbitrary",), vmem_limit_bytes=VMEM_LIMIT),
    )(wt)


def _prep_weights(ln_in_g, ln_in_b, w_in, b_igate, b_fgate, conv_w, conv_b, m_norm_g, attn_sinks, w_out,
                  ln1_g, ln1_b, w_gate_up, w_down, ln2_g, ln2_b, w_ple, w_ple_gate):
    row = lambda a: a.reshape(1, -1).astype(F32)
    w1, w2, w3, w_if, w_ift = _regroup_w_in(jnp.transpose(w_in[0], (1, 0)))
    b_if = jnp.pad(jnp.concatenate([b_igate[0], b_fgate[0]]), (0, LANES - 2 * MH))
    wo = w_out[0].astype(BF16)
    return {
        "ln_in_g": row(ln_in_g), "ln_in_b": row(ln_in_b),
        "w1": w1, "w2": w2, "w3": w3,
        "w_if": w_if, "w_ift": w_ift,
        "b_if": b_if.reshape(1, LANES), "b_ift": b_if[:2 * MH].reshape(2 * MH, 1),
        "conv_w": conv_w[0], "conv_b": row(conv_b[0]), "m_norm_g": row(m_norm_g[0]),
        "sinks": attn_sinks[0], "sinks_col": attn_sinks[0].reshape(AH, 1),
        "w_out_a": wo[:, :W_HALF], "w_out_b": wo[:, W_HALF:], "ln1_g": row(ln1_g[0]), "ln1_b": row(ln1_b[0]),
        "w_gu": w_gate_up[0].astype(BF16),
        "w_d": w_down[0].astype(BF16), "w_pg": w_ple_gate[0].astype(BF16), "w_p": w_ple[0].astype(BF16),
        "ln2_g": row(ln2_g[0]), "ln2_b": row(ln2_b[0]),
    }


def _prompt_path(x, p, wts, *, tm=256, lc=128, tmf=1024):
    B, S, _ = x.shape
    x1, c, n, m, conv, k, v = _prompt_mixer(x, wts, tm=tm, lc=lc)
    y = _ffn(x1.reshape(B * S, D), p.reshape(B * S, PD), wts, tm=tmf, sub=tmf // 4).reshape(B, S, D)
    return (y, c[None], n[None], m[None, :, :MH, 0], conv[None],
            k.reshape(1, B, WIN, KVH, HD), v.reshape(1, B, WIN, KVH, HD))


def _sample_path(x, p, c0, n0, m0, conv0, k0, v0, wts, *, tb=8):
    nb = x.shape[0]
    x1, c, n, m, conv, k, v = _sample_mixer(
        x.reshape(nb, D), c0[0], n0[0].reshape(nb, MH * DK), m0[0],
        jnp.transpose(conv0[0], (1, 0, 2)), jnp.transpose(k0[0], (0, 2, 3, 1)),
        jnp.transpose(v0[0], (0, 2, 3, 1)), wts, tb=tb)
    y = _ffn(x1, p.reshape(nb, PD), wts, tm=nb, sub=nb).reshape(nb, 1, D)
    return (y, c[None], n.reshape(1, nb, MH, DK), m[None], jnp.transpose(conv, (1, 0, 2))[None],
            jnp.transpose(k, (0, 3, 1, 2))[None], jnp.transpose(v, (0, 3, 1, 2))[None])


def kernel(x_prompt, x_sample, state_mlstm_C, state_mlstm_n, state_mlstm_m, state_conv, cache_win_k, cache_win_v, p_prompt, p_sample, ln_in_g, ln_in_b, w_in, b_igate, b_fgate, conv_w, conv_b, m_norm_g, attn_sinks, w_out, ln1_g, ln1_b, w_gate_up, w_down, ln2_g, ln2_b, w_ple, w_ple_gate):
    wts = _prep_weights(ln_in_g, ln_in_b, w_in, b_igate, b_fgate, conv_w, conv_b, m_norm_g, attn_sinks,
                        w_out, ln1_g, ln1_b, w_gate_up, w_down, ln2_g, ln2_b, w_ple, w_ple_gate)
    yp, c_p, n_p, m_p, conv_p, k_p, v_p = _prompt_path(x_prompt, p_prompt[0], wts)
    ys, c_s, n_s, m_s, conv_s, k_s, v_s = _sample_path(
        x_sample, p_sample[0], state_mlstm_C, state_mlstm_n, state_mlstm_m, state_conv,
        cache_win_k, cache_win_v, wts)
    return (yp, ys, c_p, n_p, m_p, conv_p, k_p, v_p, c_s, n_s, m_s, conv_s, k_s, v_s)
```

```python
import functools

import jax
import jax.numpy as jnp
from jax import lax
from jax.experimental import pallas as pl
from jax.experimental.pallas import tpu as pltpu

F32 = jnp.float32
BF16 = jnp.bfloat16

D = 1024
MH = 4
DV = D // MH
DK = DV // 2
CONV_W = 4
QK_CH = 2 * MH * DK
AH = 16
KVH = 2
HD = D // AH
GROUP = AH // KVH
WIN = 128
DFF = 2816
PD = 256
LN_EPS = 1e-5
RMS_EPS = 1e-6
ALPHA = 2.0 ** 0.25
LOG2E = 1.4426950408889634
NEG_INF = float("-inf")

LANES = 128
SUBLANES = 8
MXU_DIM = 256
VMEM_LIMIT = 56 * 1024 * 1024

O_QK = 0
O_MV = O_QK + QK_CH
O_MI = O_MV + D
O_MF = O_MI + MH
O_MO = O_MF + MH
O_AQ = O_MO + D
O_AK = O_AQ + D
O_AV = O_AK + KVH * HD
O_GM = O_AV + KVH * HD
O_GA = O_GM + D
IN_WIDTH = O_GA + D

W_HALF = D // 2
W1_W = 2 * D + W_HALF
W2_W = 2 * D + W_HALF
W3_W = D + 2 * KVH * HD
_WCOL = {"qk": ("w1", 0, D), "mv": ("w1", D, 2 * D), "mo_lo": ("w1", 2 * D, W1_W),
         "mo_hi": ("w2", 0, W_HALF), "aq": ("w2", W_HALF, W_HALF + D), "gm": ("w2", W_HALF + D, W2_W),
         "ga": ("w3", 0, D), "akv": ("w3", D, W3_W)}


def _dot(a, b):
    return jnp.dot(a, b, preferred_element_type=F32)


def _dot_nt(a, b):
    return lax.dot_general(a, b, (((1,), (1,)), ((), ())), preferred_element_type=F32)


def _dot_tn(a, b):
    return lax.dot_general(a, b, (((0,), (0,)), ((), ())), preferred_element_type=F32)


def _proj(xb, w, name):
    blk, a, b = _WCOL[name]
    return _dot(xb, w[blk][:, a:b])


def _layer_norm(x, g, b):
    mu = jnp.mean(x, -1, keepdims=True)
    xc = x - mu
    var = jnp.mean(xc * xc, -1, keepdims=True)
    return xc * lax.rsqrt(var + LN_EPS) * g + b


def _sigmoid(x):
    return 1.0 / (1.0 + jnp.exp(-x))


def _log_sigmoid(x):
    return jnp.minimum(x, 0.0) - jnp.log(1.0 + jnp.exp(-jnp.abs(x)))


def _const_spec(shape):
    zeros = (0,) * len(shape)
    return pl.BlockSpec(shape, lambda i: zeros, pipeline_mode=pl.Buffered(1))


def _dup_halves(x, lo):
    xr = pltpu.roll(x, HD, 1)
    return jnp.where(lo, x, xr), jnp.where(lo, xr, x)


_PROJ_NAMES = ("xn", "zqk", "vm", "smo", "aq", "k2", "v2", "kvl", "sga", "gcol", "grow", "brow")
_MIXER_W_NAMES = ("lng", "lnb", "w1", "w2", "w3", "wift", "bift", "cw", "cb", "mng", "sink", "wout_a", "wout_b")


def _mixer_project(x, slot, w, buf, *, tm, lc):
    xn = _layer_norm(x, w["lng"][...], w["lnb"][...])
    xb = xn.astype(BF16)
    buf["xn"][slot] = xn
    buf["zqk"][slot] = _proj(xb, w, "qk")
    buf["vm"][slot] = _proj(xb, w, "mv").astype(BF16)
    mo = jnp.concatenate([_proj(xb, w, "mo_lo"), _proj(xb, w, "mo_hi")], axis=1)
    buf["smo"][slot] = _sigmoid(mo) * w["mng"][...] * _sigmoid(_proj(xb, w, "gm"))
    aq = _proj(xb, w, "aq") * (HD ** -0.5 * LOG2E)
    lo_d = (lax.broadcasted_iota(jnp.int32, (tm, D), 1) & (LANES - 1)) < HD
    aqm = (jnp.where(lo_d, aq, 0.0).astype(BF16), jnp.where(lo_d, 0.0, aq).astype(BF16))
    for j in range(tm // WIN):
        for h in range(AH):
            buf["aq"][slot, j, h // GROUP, (h % GROUP) * WIN:(h % GROUP + 1) * WIN, :] = (
                aqm[h % 2][j * WIN:(j + 1) * WIN, (h // 2) * LANES:(h // 2 + 1) * LANES])
    akv = _proj(xb, w, "akv")
    lo = lax.broadcasted_iota(jnp.int32, (tm, LANES), 1) < HD
    k2 = _dup_halves(akv[:, 0:LANES], lo)
    v2 = _dup_halves(akv[:, LANES:2 * LANES], lo)
    for g in range(KVH):
        buf["k2"][slot, g] = k2[g].astype(BF16)
        buf["v2"][slot, g] = v2[g].astype(BF16)
    buf["kvl"][slot] = akv[tm - WIN:tm, :]
    buf["sga"][slot] = _sigmoid(_proj(xb, w, "ga"))
    grow = _dot_nt(w["wift"][...], xb) + w["bift"][...]
    buf["grow"][slot] = grow
    lsrow = _log_sigmoid(grow)
    lane = lax.broadcasted_iota(jnp.int32, (SUBLANES, lc), 1)
    pad = jnp.zeros((lc - 2 * SUBLANES, lc), F32)
    for c in range(tm // lc):
        r0 = c * lc
        seg = lsrow[:, r0:r0 + lc]
        sh = 1
        while sh < lc:
            seg = seg + jnp.where(lane >= sh, pltpu.roll(seg, sh, 1), 0.0)
            sh *= 2
        buf["brow"][slot, :, r0:r0 + lc] = seg
        buf["gcol"][slot, r0:r0 + lc, :] = jnp.concatenate([grow[:, r0:r0 + lc], seg, pad], axis=0).T


def _mixer_consume(slot, first, w, buf, st, x1_ref, row0, *, tm, lc):
    c_ref, n_ref, m_ref, ctail, kvc, ym_s, mix_s = st
    x1_ref[0, row0:row0 + tm, :] = ALPHA * buf["xn"][slot]
    z = buf["zqk"][slot]
    tail = ctail[...]
    ctail[...] = z[tm - SUBLANES:tm, :]
    cw = w["cw"][...]
    rows8 = lax.broadcasted_iota(jnp.int32, (SUBLANES, QK_CH), 0)
    y = cw[CONV_W - 1:CONV_W] * z + w["cb"][...]
    for sh in range(1, CONV_W):
        zr = pltpu.roll(z, sh, 0)
        head = jnp.where(rows8 < sh, pltpu.roll(tail, sh, 0), zr[0:SUBLANES])
        zr = jnp.concatenate([head, zr[SUBLANES:]], axis=0)
        y = y + cw[CONV_W - 1 - sh:CONV_W - sh] * zr
    qk = y * _sigmoid(y)

    ri = lax.broadcasted_iota(jnp.int32, (lc, lc), 0)
    ci = lax.broadcasted_iota(jnp.int32, (lc, lc), 1)
    causal = ci <= ri
    for c in range(tm // lc):
        r0 = c * lc
        for h in range(MH):
            qh = qk[r0:r0 + lc, h * DK:(h + 1) * DK]
            kh = qk[r0:r0 + lc, MH * DK + h * DK:MH * DK + (h + 1) * DK] * (DK ** -0.5)
            vh = buf["vm"][slot, r0:r0 + lc, h * DV:(h + 1) * DV]
            qb = qh.astype(BF16)
            bc = buf["gcol"][slot, r0:r0 + lc, SUBLANES + MH + h:SUBLANES + MH + h + 1]
            br = buf["brow"][slot, MH + h:MH + h + 1, r0:r0 + lc]
            li_c = buf["gcol"][slot, r0:r0 + lc, h:h + 1]
            li_r = buf["grow"][slot, h:h + 1, r0:r0 + lc]
            m_prev = m_ref[0, h:h + 1, 0:1]
            c_prev = c_ref[0, h]
            n_prev = n_ref[0, h:h + 1, :]

            dmat = jnp.where(causal, bc - br + li_r, NEG_INF)
            m_inter = bc + m_prev
            m_t = jnp.maximum(m_inter, jnp.max(dmat, -1, keepdims=True))
            wgt = jnp.exp(dmat - m_t) * _dot_nt(qb, kh.astype(BF16))
            inter = jnp.exp(m_inter - m_t)
            num = _dot(jnp.concatenate([wgt.astype(BF16), (inter * qh).astype(BF16)], axis=1),
                       jnp.concatenate([vh, c_prev.astype(BF16)], axis=0))
            den = jnp.sum(wgt, -1, keepdims=True) + inter * jnp.sum(qh * n_prev, -1, keepdims=True)
            denom = jnp.maximum(jnp.abs(den), jnp.exp(-m_t))
            hh = num / denom
            hn = hh * lax.rsqrt(jnp.mean(hh * hh, -1, keepdims=True) + RMS_EPS)
            ym_s[r0:r0 + lc, h * DV:(h + 1) * DV] = buf["smo"][slot, r0:r0 + lc, h * DV:(h + 1) * DV] * hn

            m_new = m_t[lc - 1:lc, :]
            bc_last = bc[lc - 1:lc, :]
            decay = jnp.exp(bc_last + m_prev - m_new)
            ks = kh * jnp.exp(bc_last - bc + li_c - m_new)
            c_ref[0, h] = decay * c_prev + _dot_tn(ks.astype(BF16), vh)
            n_ref[0, h:h + 1, :] = decay * n_prev + jnp.sum(ks, 0, keepdims=True)
            m_ref[0, h:h + 1, :] = jnp.broadcast_to(m_new, (1, LANES))

    lo_q = lax.broadcasted_iota(jnp.int32, (WIN, LANES), 1) < HD
    qi = lax.broadcasted_iota(jnp.int32, (WIN, 2 * WIN), 0)
    kj = lax.broadcasted_iota(jnp.int32, (WIN, 2 * WIN), 1)
    band = (kj >= qi) & (kj <= qi + WIN)
    ones_v = jnp.ones((2 * WIN, LANES), BF16)
    for j in range(tm // WIN):
        r0 = j * WIN
        if j == 0 and first is not None:
            mask = band & (kj >= jnp.where(first, WIN, 0))
        else:
            mask = band
        for g in range(KVH):
            if j == 0:
                kprev, vprev = kvc[g], kvc[KVH + g]
            else:
                kprev, vprev = buf["k2"][slot, g, r0 - WIN:r0, :], buf["v2"][slot, g, r0 - WIN:r0, :]
            kcat = jnp.concatenate([kprev, buf["k2"][slot, g, r0:r0 + WIN, :]], axis=0)
            vcat = jnp.concatenate([vprev, buf["v2"][slot, g, r0:r0 + WIN, :]], axis=0)
            sc_all = _dot_nt(buf["aq"][slot, j, g], kcat)
            es, ds = [], []
            for hh in range(GROUP):
                sink = w["sink"][g * GROUP + hh] * LOG2E
                sc = jnp.where(mask, sc_all[hh * WIN:(hh + 1) * WIN, :], NEG_INF)
                mx = jnp.maximum(jnp.max(sc, -1, keepdims=True), sink)
                es.append(jnp.exp2(sc - mx).astype(BF16))
                ds.append(jnp.exp2(sink - mx))
            o_all = _dot(jnp.concatenate(es, axis=0), jnp.concatenate([vcat, ones_v], axis=1))
            for pp in range(GROUP // 2):
                pj = g * GROUP // 2 + pp
                halves = []
                for hh in (2 * pp, 2 * pp + 1):
                    oh = o_all[hh * WIN:(hh + 1) * WIN, :]
                    halves.append(oh[:, 0:LANES] / (oh[:, LANES:2 * LANES] + ds[hh]))
                cols = slice(pj * LANES, (pj + 1) * LANES)
                ya = jnp.where(lo_q, halves[0], halves[1])
                mix_s[r0:r0 + WIN, cols] = (ym_s[r0:r0 + WIN, cols]
                                            + buf["sga"][slot, r0:r0 + WIN, cols] * ya).astype(BF16)
    for g in range(KVH):
        kvc[g] = buf["k2"][slot, g, tm - WIN:tm, :]
        kvc[KVH + g] = buf["v2"][slot, g, tm - WIN:tm, :]

    mixed = mix_s[...]
    r = jnp.concatenate([_dot(mixed, w["wout_a"][...]), _dot(mixed, w["wout_b"][...])], axis=1)
    x1_ref[0, row0:row0 + tm, :] = x1_ref[0, row0:row0 + tm, :] + r


def _mixer_kernel(*refs, tm, lc, npb, tps):
    nw = len(_MIXER_W_NAMES)
    x0_ref = refs[0]
    xt_refs = refs[1:1 + tps]
    w = dict(zip(_MIXER_W_NAMES, refs[1 + tps:1 + tps + nw]))
    n_in = 1 + tps + nw + N_CAST
    cast_in = refs[n_in - N_CAST:n_in]
    x1_ref, c_ref, n_ref, m_ref, conv_ref, kp_ref, vp_ref = refs[n_in:n_in + 7]
    cast_out = refs[n_in + 7:n_in + 7 + N_CAST]
    scratch = refs[n_in + 7 + N_CAST:]
    buf = dict(zip(_PROJ_NAMES, scratch[:len(_PROJ_NAMES)]))
    ctail, kvc, ym_s, mix_s = scratch[len(_PROJ_NAMES):]
    st = (c_ref, n_ref, m_ref, ctail, kvc, ym_s, mix_s)
    k = pl.program_id(0)
    first = (tps * k) % npb == 0

    @pl.when(k == 0)
    def _prologue():
        _mixer_project(x0_ref[0], 0, w, buf, tm=tm, lc=lc)

    @pl.when(first)
    def _init():
        c_ref[...] = jnp.zeros_like(c_ref)
        n_ref[...] = jnp.zeros_like(n_ref)
        m_ref[...] = jnp.zeros_like(m_ref)
        ctail[...] = jnp.zeros_like(ctail)
        kvc[...] = jnp.zeros_like(kvc)

    for i in range(tps):
        _mixer_project(xt_refs[i][0], (i + 1) % 2, w, buf, tm=tm, lc=lc)
        _mixer_consume(i % 2, first if i == 0 else None, w, buf, st, x1_ref, i * tm, tm=tm, lc=lc)

    for src, dst in zip(cast_in, cast_out):
        dst[...] = src[...].astype(BF16)

    @pl.when((tps * k + tps - 1) % npb == npb - 1)
    def _state_out():
        conv_ref[0] = ctail[SUBLANES - (CONV_W - 1):SUBLANES, :]
        kp_ref[0] = buf["kvl"][(tps - 1) % 2, :, 0:LANES]
        vp_ref[0] = buf["kvl"][(tps - 1) % 2, :, LANES:2 * LANES]


N_CAST = 4
BF16_ROWS = 16


def _cast_slab(rows, steps):
    tiles = rows // BF16_ROWS
    for d in range(1, tiles + 1):
        if tiles % d == 0 and tiles // d <= steps:
            return BF16_ROWS * d
    raise ValueError((rows, steps))


def _prompt_mixer(x, wts, ffn_w, *, tm=256, lc=128, tps=2):
    B, S, _ = x.shape
    npb = S // tm
    assert S % tm == 0 and tps % 2 == 0 and npb % tps == 0 and tm % lc == 0 and tm % WIN == 0
    nt = B * npb
    hpb = npb // tps
    cs = _const_spec

    def tile_map(off):
        def imap(k):
            t = jnp.minimum(tps * k + off, nt - 1)
            return (t // npb, t % npb, 0)
        return imap

    in_specs = [
        pl.BlockSpec((1, tm, D), lambda k: (0, 0, 0), pipeline_mode=pl.Buffered(1)),
        *[pl.BlockSpec((1, tm, D), tile_map(i + 1)) for i in range(tps)],
        cs((1, D)), cs((1, D)),
        cs((D, W1_W)), cs((D, W2_W)), cs((D, W3_W)), cs((SUBLANES, D)), cs((SUBLANES, 1)),
        cs((CONV_W, QK_CH)), cs((1, QK_CH)), cs((1, D)),
        pl.BlockSpec(memory_space=pltpu.SMEM),
        cs((D, W_HALF)), cs((D, W_HALF)),
    ]
    steps = nt // tps
    cast_specs = []
    for a in ffn_w:
        assert a.shape[0] % BF16_ROWS == 0
        slab = _cast_slab(a.shape[0], steps)
        last = a.shape[0] // slab - 1
        cast_specs.append(pl.BlockSpec((slab, a.shape[1]), lambda k, last=last: (jnp.minimum(k, last), 0)))
    in_specs += cast_specs
    out_shape = (
        jax.ShapeDtypeStruct((B, S, D), F32),
        jax.ShapeDtypeStruct((B, MH, DK, DV), F32),
        jax.ShapeDtypeStruct((B, MH, DK), F32),
        jax.ShapeDtypeStruct((B, SUBLANES, LANES), F32),
        jax.ShapeDtypeStruct((B, CONV_W - 1, QK_CH), F32),
        jax.ShapeDtypeStruct((B, WIN, KVH * HD), F32),
        jax.ShapeDtypeStruct((B, WIN, KVH * HD), F32),
        *[jax.ShapeDtypeStruct(a.shape, BF16) for a in ffn_w],
    )
    out_specs = (
        pl.BlockSpec((1, tps * tm, D), lambda k: (k // hpb, k % hpb, 0)),
        pl.BlockSpec((1, MH, DK, DV), lambda k: (k // hpb, 0, 0, 0)),
        pl.BlockSpec((1, MH, DK), lambda k: (k // hpb, 0, 0)),
        pl.BlockSpec((1, SUBLANES, LANES), lambda k: (k // hpb, 0, 0)),
        pl.BlockSpec((1, CONV_W - 1, QK_CH), lambda k: (k // hpb, 0, 0)),
        pl.BlockSpec((1, WIN, KVH * HD), lambda k: (k // hpb, 0, 0)),
        pl.BlockSpec((1, WIN, KVH * HD), lambda k: (k // hpb, 0, 0)),
        *cast_specs,
    )
    proj_scratch = {
        "xn": pltpu.VMEM((2, tm, D), F32), "zqk": pltpu.VMEM((2, tm, QK_CH), F32),
        "vm": pltpu.VMEM((2, tm, D), BF16), "smo": pltpu.VMEM((2, tm, D), F32),
        "aq": pltpu.VMEM((2, tm // WIN, KVH, GROUP * WIN, LANES), BF16),
        "k2": pltpu.VMEM((2, KVH, tm, LANES), BF16), "v2": pltpu.VMEM((2, KVH, tm, LANES), BF16),
        "kvl": pltpu.VMEM((2, WIN, 2 * LANES), F32),
        "sga": pltpu.VMEM((2, tm, D), F32),
        "gcol": pltpu.VMEM((2, tm, LANES), F32), "grow": pltpu.VMEM((2, SUBLANES, tm), F32),
        "brow": pltpu.VMEM((2, SUBLANES, tm), F32),
    }
    scratch = [proj_scratch[n] for n in _PROJ_NAMES] + [
        pltpu.VMEM((SUBLANES, QK_CH), F32),
        pltpu.VMEM((2 * KVH, WIN, LANES), BF16),
        pltpu.VMEM((tm, D), F32),
        pltpu.VMEM((tm, D), BF16),
    ]
    return pl.pallas_call(
        functools.partial(_mixer_kernel, tm=tm, lc=lc, npb=npb, tps=tps),
        grid=(nt // tps,), in_specs=in_specs, out_specs=out_specs, out_shape=out_shape,
        scratch_shapes=scratch, name="prompt_mixer",
        compiler_params=pltpu.CompilerParams(dimension_semantics=("arbitrary",), vmem_limit_bytes=VMEM_LIMIT),
    )(*([x] * (tps + 1)), wts["ln_in_g"], wts["ln_in_b"], wts["w1"], wts["w2"], wts["w3"], wts["w_ift"], wts["b_ift"],
      wts["conv_w"], wts["conv_b"], wts["m_norm_g"], wts["sinks"], wts["w_out_a"], wts["w_out_b"], *ffn_w)


FFN_CHUNKS = ((0, 6 * MXU_DIM), (6 * MXU_DIM, DFF))


def _ffn_kernel(x_ref, p_ref, l1g_ref, l1b_ref, wgu_ref, wd_ref, wpg_ref, wp_ref, g_ref, b_ref, o_ref, *, sub):
    n = x_ref.shape[0] // sub

    def pre(j):
        x = _layer_norm(x_ref[j * sub:(j + 1) * sub, :], l1g_ref[...], l1b_ref[...])
        return x, x.astype(BF16)

    def body(j, x, xb):
        acc = ALPHA * x + (_sigmoid(_dot(xb, wpg_ref[...]))
                           * _dot(p_ref[j * sub:(j + 1) * sub, :].astype(BF16), wp_ref[...]))
        for c0, c1 in FFN_CHUNKS:
            g = _dot(xb, wgu_ref[:, c0:c1])
            u = _dot(xb, wgu_ref[:, DFF + c0:DFF + c1])
            hcat = (g * _sigmoid(g) * u).astype(BF16)
            acc = acc + _dot(hcat, wd_ref[c0:c1, :])
        return acc

    def post(j, acc):
        o_ref[j * sub:(j + 1) * sub, :] = _layer_norm(acc, g_ref[...], b_ref[...])

    cur = pre(0)
    acc_prev = None
    for j in range(n):
        nxt = pre(j + 1) if j + 1 < n else None
        acc = body(j, *cur)
        if acc_prev is not None:
            post(j - 1, acc_prev)
        acc_prev, cur = acc, nxt
    post(n - 1, acc_prev)


def _ffn(x, p, wts, *, tm, sub):
    N = x.shape[0]
    assert N % tm == 0 and tm % sub == 0
    cs = _const_spec
    return pl.pallas_call(
        functools.partial(_ffn_kernel, sub=sub),
        grid=(N // tm,),
        in_specs=[pl.BlockSpec((tm, D), lambda i: (i, 0)), pl.BlockSpec((tm, PD), lambda i: (i, 0)),
                  cs((1, D)), cs((1, D)),
                  cs((D, 2 * DFF)), cs((DFF, D)), cs((D, D)), cs((PD, D)), cs((1, D)), cs((1, D))],
        out_specs=pl.BlockSpec((tm, D), lambda i: (i, 0)),
        out_shape=jax.ShapeDtypeStruct((N, D), F32),
        name="ffn",
        compiler_params=pltpu.CompilerParams(dimension_semantics=("arbitrary",), vmem_limit_bytes=VMEM_LIMIT),
    )(x, p, wts["ln1_g"], wts["ln1_b"], wts["w_gu"], wts["w_d"], wts["w_pg"], wts["w_p"],
      wts["ln2_g"], wts["ln2_b"])


Z_V = 0
Z_MO = Z_V + D
Z_GM = Z_MO + D
Z_GA = Z_GM + D
Z_XN = Z_GA + D
Z_Q = Z_XN + D
Z_KW = Z_Q + MH * DK
Z_SC = Z_KW + MH * DK
Z_KN = Z_SC + LANES
Z_VN = Z_KN + LANES
Z_W = Z_VN + LANES


def _sample_pre_kernel(x_ref, lng_ref, lnb_ref, w1_ref, w2_ref, w3_ref, wif_ref, bif_ref,
                       cw_ref, cb_ref, cs_ref, n_ref, m_ref,
                       z_ref, qs_ref, convo_ref, no_ref, mo_ref, *, nb, tb):
    w = {"w1": w1_ref, "w2": w2_ref, "w3": w3_ref}
    xn = _layer_norm(x_ref[...], lng_ref[...], lnb_ref[...])
    xb = xn.astype(BF16)
    zqk = _proj(xb, w, "qk")
    cw = cw_ref[...]
    y = cw[3:4] * zqk + cw[2:3] * cs_ref[2] + cw[1:2] * cs_ref[1] + cw[0:1] * cs_ref[0] + cb_ref[...]
    qk = y * _sigmoid(y)
    convo_ref[0] = cs_ref[1]
    convo_ref[1] = cs_ref[2]
    convo_ref[2] = zqk

    gcol = _dot(xb, wif_ref[...]) + bif_ref[...]
    ig = gcol[:, 0:MH]
    m_inter = _log_sigmoid(gcol[:, MH:2 * MH]) + m_ref[...]
    m_t = jnp.maximum(m_inter, ig)
    inter = jnp.exp(m_inter - m_t)
    wi = jnp.exp(ig - m_t)
    enm = jnp.exp(-m_t)
    mo_ref[...] = m_t

    z_ref[:, Z_V:Z_V + D] = _proj(xb, w, "mv")
    z_ref[:, Z_MO:Z_MO + W_HALF] = _proj(xb, w, "mo_lo")
    z_ref[:, Z_MO + W_HALF:Z_MO + D] = _proj(xb, w, "mo_hi")
    z_ref[:, Z_GM:Z_GM + D] = _proj(xb, w, "gm")
    z_ref[:, Z_GA:Z_GA + D] = _proj(xb, w, "ga")
    z_ref[:, Z_XN:Z_XN + D] = xn
    z_ref[:, Z_Q:Z_Q + MH * DK] = qk[:, 0:MH * DK]
    sc = jnp.zeros((nb, LANES), F32)
    lane = lax.broadcasted_iota(jnp.int32, (nb, LANES), 1)
    for h in range(MH):
        qh = qk[:, h * DK:(h + 1) * DK]
        kh = qk[:, MH * DK + h * DK:MH * DK + (h + 1) * DK] * (DK ** -0.5)
        nh = n_ref[:, h * DK:(h + 1) * DK]
        wv = wi[:, h:h + 1] * jnp.sum(qh * kh, -1, keepdims=True)
        den = wv + inter[:, h:h + 1] * jnp.sum(qh * nh, -1, keepdims=True)
        denom = jnp.maximum(jnp.abs(den), enm[:, h:h + 1])
        kw = wi[:, h:h + 1] * kh
        z_ref[:, Z_KW + h * DK:Z_KW + (h + 1) * DK] = kw
        no_ref[:, h * DK:(h + 1) * DK] = inter[:, h:h + 1] * nh + kw
        sc = jnp.where(lane == h, inter[:, h:h + 1], sc)
        sc = jnp.where(lane == MH + h, wv, sc)
        sc = jnp.where(lane == 2 * MH + h, denom, sc)
    z_ref[:, Z_SC:Z_SC + LANES] = sc

    akv = _proj(xb, w, "akv")
    z_ref[:, Z_KN:Z_KN + LANES] = akv[:, 0:LANES]
    z_ref[:, Z_VN:Z_VN + LANES] = akv[:, LANES:2 * LANES]
    aq = _proj(xb, w, "aq") * (HD ** -0.5)
    lo = lax.broadcasted_iota(jnp.int32, (nb, LANES), 1) < HD
    for h in range(AH):
        pair = aq[:, (h // 2) * LANES:(h // 2 + 1) * LANES]
        qs_ref[pl.ds(h, nb, stride=AH), :] = jnp.where(lo, pair if h % 2 == 0 else pltpu.roll(pair, HD, 1), 0.0)


def _sample_state_kernel(z_ref, qs_ref, sink_ref, c_ref, ck_ref, cv_ref,
                         co_ref, qc_ref, os_ref, ko_ref, vo_ref, *, tb):
    lo = lax.broadcasted_iota(jnp.int32, (WIN, LANES), 1) < HD
    newest = lax.broadcasted_iota(jnp.int32, (HD, WIN), 1) == WIN - 1
    sinks = sink_ref[...]
    q_all = z_ref[:, Z_Q:Z_Q + MH * DK]
    kw_all = z_ref[:, Z_KW:Z_KW + MH * DK]
    v_all = z_ref[:, Z_V:Z_V + D]
    sc = z_ref[:, Z_SC:Z_SC + LANES]
    kn_all = z_ref[:, Z_KN:Z_KN + LANES]
    vn_all = z_ref[:, Z_VN:Z_VN + LANES]
    kn2_all = _dup_halves(kn_all, lo[0:tb])
    vn2_all = _dup_halves(vn_all, lo[0:tb])
    rpad = jnp.zeros((LANES - tb, LANES), F32)
    knt = [jnp.concatenate([a, rpad], axis=0).T for a in kn2_all]
    vnt = [jnp.concatenate([a, rpad], axis=0).T for a in vn2_all]
    lane_sel = lax.broadcasted_iota(jnp.int32, (DK, LANES), 1)
    for h in range(MH):
        q_h = q_all[:, h * DK:(h + 1) * DK].astype(BF16)
        kwt_h = jnp.concatenate([kw_all[:, h * DK:(h + 1) * DK], rpad], axis=0).T
        v_h = jnp.concatenate([v_all[:, h * DV:(h + 1) * DV], jnp.zeros((LANES - tb, DV), F32)], axis=0).astype(BF16)
        for i in range(tb):
            cb = c_ref[i, h]
            qc = _dot(q_h, cb.astype(BF16))
            qc_ref[i:i + 1, h * DV:(h + 1) * DV] = qc[i:i + 1]
            outer = _dot(jnp.where(lane_sel == i, kwt_h, 0.0).astype(BF16), v_h)
            co_ref[i, h] = sc[i:i + 1, h:h + 1] * cb + outer

    q_rows = qs_ref[:, 0:HD]
    kn_rows = jnp.concatenate([jnp.broadcast_to(kn2_all[g][i:i + 1, 0:HD], (GROUP, HD))
                               for i in range(tb) for g in range(KVH)], axis=0)
    vn_rows = jnp.concatenate([jnp.broadcast_to(vn2_all[g][i:i + 1, 0:HD], (GROUP, HD))
                               for i in range(tb) for g in range(KVH)], axis=0)
    sink_rows = jnp.concatenate([sinks] * tb, axis=0)
    q_bf = q_rows.astype(BF16)
    s = jnp.concatenate([_dot(q_bf[(i * KVH + g) * GROUP:(i * KVH + g + 1) * GROUP], ck_ref[i, g].astype(BF16))
                         for i in range(tb) for g in range(KVH)], axis=0)
    s_self = jnp.sum(q_rows * kn_rows, -1, keepdims=True)
    mx = jnp.maximum(jnp.maximum(jnp.max(s, -1, keepdims=True), s_self), sink_rows)
    e = jnp.exp(s - mx)
    es = jnp.exp(s_self - mx)
    dsum = jnp.sum(e, -1, keepdims=True) + es + jnp.exp(sink_rows - mx)
    e_bf = e.astype(BF16)
    pv = jnp.concatenate([_dot_nt(e_bf[(i * KVH + g) * GROUP:(i * KVH + g + 1) * GROUP], cv_ref[i, g].astype(BF16))
                          for i in range(tb) for g in range(KVH)], axis=0)
    o = (pv + es * vn_rows) / dsum
    os_ref[...] = jnp.concatenate([o, jnp.zeros((tb * AH, LANES - HD), F32)], axis=1)
    for i in range(tb):
        for g in range(KVH):
            kcol = jnp.broadcast_to(knt[g][0:HD, i:i + 1], (HD, WIN))
            vcol = jnp.broadcast_to(vnt[g][0:HD, i:i + 1], (HD, WIN))
            ko_ref[i, g] = jnp.where(newest, kcol, pltpu.roll(ck_ref[i, g], WIN - 1, 1))
            vo_ref[i, g] = jnp.where(newest, vcol, pltpu.roll(cv_ref[i, g], WIN - 1, 1))


def _sample_post_kernel(z_ref, qc_ref, os_ref, mng_ref, wouta_ref, woutb_ref, x1_ref, *, nb, tb):
    sc = z_ref[:, Z_SC:Z_SC + LANES]
    mng = mng_ref[...]
    lo = lax.broadcasted_iota(jnp.int32, (nb, LANES), 1) < HD
    ym = []
    for h in range(MH):
        v = z_ref[:, Z_V + h * DV:Z_V + (h + 1) * DV]
        num = sc[:, MH + h:MH + h + 1] * v + sc[:, h:h + 1] * qc_ref[:, h * DV:(h + 1) * DV]
        hh = num / sc[:, 2 * MH + h:2 * MH + h + 1]
        hn = hh * lax.rsqrt(jnp.mean(hh * hh, -1, keepdims=True) + RMS_EPS)
        ym.append(_sigmoid(z_ref[:, Z_MO + h * DV:Z_MO + (h + 1) * DV]) * hn * mng[:, h * DV:(h + 1) * DV])
    ym = jnp.concatenate(ym, axis=1)
    ya = []
    for pj in range(AH // 2):
        halves = []
        for par in range(2):
            halves.append(os_ref[pl.ds(2 * pj + par, nb, stride=AH), :])
        ya.append(jnp.where(lo, halves[0], pltpu.roll(halves[1], HD, 1)))
    ya = jnp.concatenate(ya, axis=1)
    mixed = _sigmoid(z_ref[:, Z_GM:Z_GM + D]) * ym + _sigmoid(z_ref[:, Z_GA:Z_GA + D]) * ya
    mixed = mixed.astype(BF16)
    r = jnp.concatenate([_dot(mixed, wouta_ref[...]), _dot(mixed, woutb_ref[...])], axis=1)
    x1_ref[...] = ALPHA * z_ref[:, Z_XN:Z_XN + D] + r


def _sample_mixer(x, c0, n0, m0, conv0, k0, v0, wts, *, tb=8):
    nb = x.shape[0]
    assert nb % tb == 0
    vmem = pltpu.CompilerParams(dimension_semantics=("arbitrary",), vmem_limit_bytes=VMEM_LIMIT)
    full = lambda shape: pl.BlockSpec(shape, lambda i: (0,) * len(shape))
    pre_in = [x, wts["ln_in_g"], wts["ln_in_b"], wts["w1"], wts["w2"], wts["w3"], wts["w_if"], wts["b_if"],
              wts["conv_w"], wts["conv_b"], conv0, n0, m0]
    z, qs, conv_new, n_new, m_new = pl.pallas_call(
        functools.partial(_sample_pre_kernel, nb=nb, tb=tb),
        grid=(1,),
        in_specs=[full(a.shape) for a in pre_in],
        out_specs=(full((nb, Z_W)), full((nb * AH, LANES)), full((CONV_W - 1, nb, QK_CH)),
                   full((nb, MH * DK)), full((nb, MH))),
        out_shape=(jax.ShapeDtypeStruct((nb, Z_W), F32), jax.ShapeDtypeStruct((nb * AH, LANES), F32),
                   jax.ShapeDtypeStruct((CONV_W - 1, nb, QK_CH), F32),
                   jax.ShapeDtypeStruct((nb, MH * DK), F32), jax.ShapeDtypeStruct((nb, MH), F32)),
        name="sample_pre", compiler_params=vmem,
    )(*pre_in)

    c_new, qc, os_, k_new, v_new = pl.pallas_call(
        functools.partial(_sample_state_kernel, tb=tb),
        grid=(nb // tb,),
        in_specs=[pl.BlockSpec((tb, Z_W), lambda i: (i, 0)),
                  pl.BlockSpec((tb * AH, LANES), lambda i: (i, 0)),
                  pl.BlockSpec((AH, 1), lambda i: (0, 0)),
                  pl.BlockSpec((tb, MH, DK, DV), lambda i: (i, 0, 0, 0)),
                  pl.BlockSpec((tb, KVH, HD, WIN), lambda i: (i, 0, 0, 0)),
                  pl.BlockSpec((tb, KVH, HD, WIN), lambda i: (i, 0, 0, 0))],
        out_specs=(pl.BlockSpec((tb, MH, DK, DV), lambda i: (i, 0, 0, 0)),
                   pl.BlockSpec((tb, D), lambda i: (i, 0)),
                   pl.BlockSpec((tb * AH, LANES), lambda i: (i, 0)),
                   pl.BlockSpec((tb, KVH, HD, WIN), lambda i: (i, 0, 0, 0)),
                   pl.BlockSpec((tb, KVH, HD, WIN), lambda i: (i, 0, 0, 0))),
        out_shape=(jax.ShapeDtypeStruct((nb, MH, DK, DV), F32), jax.ShapeDtypeStruct((nb, D), F32),
                   jax.ShapeDtypeStruct((nb * AH, LANES), F32),
                   jax.ShapeDtypeStruct((nb, KVH, HD, WIN), F32), jax.ShapeDtypeStruct((nb, KVH, HD, WIN), F32)),
        name="sample_state", compiler_params=vmem,
    )(z, qs, wts["sinks_col"], c0, k0, v0)

    post_in = [z, qc, os_, wts["m_norm_g"], wts["w_out_a"], wts["w_out_b"]]
    x1 = pl.pallas_call(
        functools.partial(_sample_post_kernel, nb=nb, tb=tb),
        grid=(1,),
        in_specs=[full(a.shape) for a in post_in],
        out_specs=full((nb, D)),
        out_shape=jax.ShapeDtypeStruct((nb, D), F32),
        name="sample_post", compiler_params=vmem,
    )(*post_in)
    return x1, c_new, n_new, m_new, conv_new, k_new, v_new


def _regroup_kernel(wt_ref, wo_ref, w1_ref, w2_ref, w3_ref, wif_ref, wift_ref, woa_ref, wob_ref):
    woa_ref[...] = wo_ref[:, 0:W_HALF].astype(BF16)
    wob_ref[...] = wo_ref[:, W_HALF:D].astype(BF16)
    cols = lambda a, b: wt_ref[a:b, :].T.astype(BF16)
    w1_ref[:, 0:2 * D] = cols(O_QK, O_MI)
    w1_ref[:, 2 * D:W1_W] = cols(O_MO, O_MO + W_HALF)
    w2_ref[:, 0:W_HALF + D] = cols(O_MO + W_HALF, O_AK)
    w2_ref[:, W_HALF + D:W2_W] = cols(O_GM, O_GA)
    w3_ref[:, 0:D] = cols(O_GA, IN_WIDTH)
    w3_ref[:, D:W3_W] = cols(O_AK, O_GM)
    gates_t = wt_ref[O_MI:O_MO, :]
    rb = gates_t.shape[1]
    wift_ref[...] = gates_t.astype(BF16)
    wif_ref[...] = jnp.concatenate([gates_t, jnp.zeros((LANES - 2 * MH, rb), F32)], axis=0).T.astype(BF16)


def _regroup_w_in(wt, wo, *, rb=256):
    assert D % rb == 0 and rb % LANES == 0
    return pl.pallas_call(
        _regroup_kernel, grid=(D // rb,),
        in_specs=[pl.BlockSpec((IN_WIDTH, rb), lambda i: (0, i)), pl.BlockSpec((rb, D), lambda i: (i, 0))],
        out_specs=(pl.BlockSpec((rb, W1_W), lambda i: (i, 0)), pl.BlockSpec((rb, W2_W), lambda i: (i, 0)),
                   pl.BlockSpec((rb, W3_W), lambda i: (i, 0)), pl.BlockSpec((rb, LANES), lambda i: (i, 0)),
                   pl.BlockSpec((2 * MH, rb), lambda i: (0, i)),
                   pl.BlockSpec((rb, W_HALF), lambda i: (i, 0)), pl.BlockSpec((rb, W_HALF), lambda i: (i, 0))),
        out_shape=(jax.ShapeDtypeStruct((D, W1_W), BF16), jax.ShapeDtypeStruct((D, W2_W), BF16),
                   jax.ShapeDtypeStruct((D, W3_W), BF16), jax.ShapeDtypeStruct((D, LANES), BF16),
                   jax.ShapeDtypeStruct((2 * MH, D), BF16),
                   jax.ShapeDtypeStruct((D, W_HALF), BF16), jax.ShapeDtypeStruct((D, W_HALF), BF16)),
        name="regroup_w_in",
        compiler_params=pltpu.CompilerParams(dimension_semantics=("arbitrary",), vmem_limit_bytes=VMEM_LIMIT),
    )(wt, wo)


def _prep_weights(ln_in_g, ln_in_b, w_in, b_igate, b_fgate, conv_w, conv_b, m_norm_g, attn_sinks, w_out,
                  ln1_g, ln1_b, w_gate_up, w_down, ln2_g, ln2_b, w_ple, w_ple_gate):
    row = lambda a: a.reshape(1, -1).astype(F32)
    w1, w2, w3, w_if, w_ift, wo_a, wo_b = _regroup_w_in(jnp.transpose(w_in[0], (1, 0)), w_out[0])
    b_if = jnp.pad(jnp.concatenate([b_igate[0], b_fgate[0]]), (0, LANES - 2 * MH))
    return {
        "ln_in_g": row(ln_in_g), "ln_in_b": row(ln_in_b),
        "w1": w1, "w2": w2, "w3": w3,
        "w_if": w_if, "w_ift": w_ift,
        "b_if": b_if.reshape(1, LANES), "b_ift": b_if[:2 * MH].reshape(2 * MH, 1),
        "conv_w": conv_w[0], "conv_b": row(conv_b[0]), "m_norm_g": row(m_norm_g[0]),
        "sinks": attn_sinks[0], "sinks_col": attn_sinks[0].reshape(AH, 1),
        "w_out_a": wo_a, "w_out_b": wo_b, "ln1_g": row(ln1_g[0]), "ln1_b": row(ln1_b[0]),
        "ffn_f32": (w_gate_up[0], w_down[0], w_ple_gate[0], w_ple[0]),
        "ln2_g": row(ln2_g[0]), "ln2_b": row(ln2_b[0]),
    }


def _prompt_path(x, p, wts, *, tm=256, lc=128, tmf=1024):
    B, S, _ = x.shape
    x1, c, n, m, conv, k, v, w_gu, w_d, w_pg, w_p = _prompt_mixer(x, wts, wts["ffn_f32"], tm=tm, lc=lc)
    wts = {**wts, "w_gu": w_gu, "w_d": w_d, "w_pg": w_pg, "w_p": w_p}
    y = _ffn(x1.reshape(B * S, D), p.reshape(B * S, PD), wts, tm=tmf, sub=tmf // 4).reshape(B, S, D)
    return wts, (y, c[None], n[None], m[None, :, :MH, 0], conv[None],
                 k.reshape(1, B, WIN, KVH, HD), v.reshape(1, B, WIN, KVH, HD))


def _sample_path(x, p, c0, n0, m0, conv0, k0, v0, wts, *, tb=8):
    nb = x.shape[0]
    x1, c, n, m, conv, k, v = _sample_mixer(
        x.reshape(nb, D), c0[0], n0[0].reshape(nb, MH * DK), m0[0],
        jnp.transpose(conv0[0], (1, 0, 2)), jnp.transpose(k0[0], (0, 2, 3, 1)),
        jnp.transpose(v0[0], (0, 2, 3, 1)), wts, tb=tb)
    y = _ffn(x1, p.reshape(nb, PD), wts, tm=nb, sub=nb).reshape(nb, 1, D)
    return (y, c[None], n.reshape(1, nb, MH, DK), m[None], jnp.transpose(conv, (1, 0, 2))[None],
            jnp.transpose(k, (0, 3, 1, 2))[None], jnp.transpose(v, (0, 3, 1, 2))[None])


def kernel(x_prompt, x_sample, state_mlstm_C, state_mlstm_n, state_mlstm_m, state_conv, cache_win_k, cache_win_v, p_prompt, p_sample, ln_in_g, ln_in_b, w_in, b_igate, b_fgate, conv_w, conv_b, m_norm_g, attn_sinks, w_out, ln1_g, ln1_b, w_gate_up, w_down, ln2_g, ln2_b, w_ple, w_ple_gate):
    wts = _prep_weights(ln_in_g, ln_in_b, w_in, b_igate, b_fgate, conv_w, conv_b, m_norm_g, attn_sinks,
                        w_out, ln1_g, ln1_b, w_gate_up, w_down, ln2_g, ln2_b, w_ple, w_ple_gate)
    wts, (yp, c_p, n_p, m_p, conv_p, k_p, v_p) = _prompt_path(x_prompt, p_prompt[0], wts)
    ys, c_s, n_s, m_s, conv_s, k_s, v_s = _sample_path(
        x_sample, p_sample[0], state_mlstm_C, state_mlstm_n, state_mlstm_m, state_conv,
        cache_win_k, cache_win_v, wts)
    return (yp, ys, c_p, n_p, m_p, conv_p, k_p, v_p, c_s, n_s, m_s, conv_s, k_s, v_s)
```

```python
import functools

import jax
import jax.numpy as jnp
from jax import lax
from jax.experimental import pallas as pl
from jax.experimental.pallas import tpu as pltpu

F32 = jnp.float32
BF16 = jnp.bfloat16

D = 1024
MH = 4
DV = D // MH
DK = DV // 2
CONV_W = 4
QK_CH = 2 * MH * DK
AH = 16
KVH = 2
HD = D // AH
GROUP = AH // KVH
WIN = 128
DFF = 2816
PD = 256
LN_EPS = 1e-5
RMS_EPS = 1e-6
ALPHA = 2.0 ** 0.25
LOG2E = 1.4426950408889634
NEG_INF = float("-inf")

LANES = 128
SUBLANES = 8
MXU_DIM = 256
VMEM_LIMIT = 56 * 1024 * 1024

O_QK = 0
O_MV = O_QK + QK_CH
O_MI = O_MV + D
O_MF = O_MI + MH
O_MO = O_MF + MH
O_AQ = O_MO + D
O_AK = O_AQ + D
O_AV = O_AK + KVH * HD
O_GM = O_AV + KVH * HD
O_GA = O_GM + D
IN_WIDTH = O_GA + D

W_HALF = D // 2
W1_W = 2 * D + W_HALF
W2_W = 2 * D + W_HALF
W3_W = D + 2 * KVH * HD
_WCOL = {"qk": ("w1", 0, D), "mv": ("w1", D, 2 * D), "mo_lo": ("w1", 2 * D, W1_W),
         "mo_hi": ("w2", 0, W_HALF), "aq": ("w2", W_HALF, W_HALF + D), "gm": ("w2", W_HALF + D, W2_W),
         "ga": ("w3", 0, D), "akv": ("w3", D, W3_W)}


def _dot(a, b):
    return jnp.dot(a, b, preferred_element_type=F32)


def _dot_nt(a, b):
    return lax.dot_general(a, b, (((1,), (1,)), ((), ())), preferred_element_type=F32)


def _dot_tn(a, b):
    return lax.dot_general(a, b, (((0,), (0,)), ((), ())), preferred_element_type=F32)


def _proj(xb, w, name):
    blk, a, b = _WCOL[name]
    return _dot(xb, w[blk][:, a:b])


def _layer_norm(x, g, b):
    mu = jnp.mean(x, -1, keepdims=True)
    xc = x - mu
    var = jnp.mean(xc * xc, -1, keepdims=True)
    return xc * lax.rsqrt(var + LN_EPS) * g + b


def _sigmoid(x):
    return 1.0 / (1.0 + jnp.exp(-x))


def _log_sigmoid(x):
    return jnp.minimum(x, 0.0) - jnp.log(1.0 + jnp.exp(-jnp.abs(x)))


def _const_spec(shape):
    zeros = (0,) * len(shape)
    return pl.BlockSpec(shape, lambda i: zeros, pipeline_mode=pl.Buffered(1))


def _dup_halves(x, lo):
    xr = pltpu.roll(x, HD, 1)
    return jnp.where(lo, x, xr), jnp.where(lo, xr, x)


_PROJ_NAMES = ("xn", "zqk", "vm", "smo", "aq", "k2", "v2", "kvl", "sga", "gcol", "grow", "brow")
_MIXER_W_NAMES = ("lng", "lnb", "w1", "w2", "w3", "wift", "bift", "cw", "cb", "mng", "sink", "wout_a", "wout_b")


def _mixer_project(x, slot, w, buf, *, tm, lc):
    xn = _layer_norm(x, w["lng"][...], w["lnb"][...])
    xb = xn.astype(BF16)
    buf["xn"][slot] = xn
    buf["zqk"][slot] = _proj(xb, w, "qk")
    buf["vm"][slot] = _proj(xb, w, "mv").astype(BF16)
    mo = jnp.concatenate([_proj(xb, w, "mo_lo"), _proj(xb, w, "mo_hi")], axis=1)
    buf["smo"][slot] = _sigmoid(mo) * w["mng"][...] * _sigmoid(_proj(xb, w, "gm"))
    aq = _proj(xb, w, "aq") * (HD ** -0.5 * LOG2E)
    lo_d = (lax.broadcasted_iota(jnp.int32, (tm, D), 1) & (LANES - 1)) < HD
    aqm = (jnp.where(lo_d, aq, 0.0).astype(BF16), jnp.where(lo_d, 0.0, aq).astype(BF16))
    for j in range(tm // WIN):
        for h in range(AH):
            buf["aq"][slot, j, h // GROUP, (h % GROUP) * WIN:(h % GROUP + 1) * WIN, :] = (
                aqm[h % 2][j * WIN:(j + 1) * WIN, (h // 2) * LANES:(h // 2 + 1) * LANES])
    akv = _proj(xb, w, "akv")
    lo = lax.broadcasted_iota(jnp.int32, (tm, LANES), 1) < HD
    k2 = _dup_halves(akv[:, 0:LANES], lo)
    v2 = _dup_halves(akv[:, LANES:2 * LANES], lo)
    for g in range(KVH):
        buf["k2"][slot, g] = k2[g].astype(BF16)
        buf["v2"][slot, g] = v2[g].astype(BF16)
    buf["kvl"][slot] = akv[tm - WIN:tm, :]
    buf["sga"][slot] = _sigmoid(_proj(xb, w, "ga"))
    grow = _dot_nt(w["wift"][...], xb) + w["bift"][...]
    buf["grow"][slot] = grow
    lsrow = _log_sigmoid(grow)
    lane = lax.broadcasted_iota(jnp.int32, (SUBLANES, lc), 1)
    pad = jnp.zeros((lc - 2 * SUBLANES, lc), F32)
    for c in range(tm // lc):
        r0 = c * lc
        seg = lsrow[:, r0:r0 + lc]
        sh = 1
        while sh < lc:
            seg = seg + jnp.where(lane >= sh, pltpu.roll(seg, sh, 1), 0.0)
            sh *= 2
        buf["brow"][slot, :, r0:r0 + lc] = seg
        buf["gcol"][slot, r0:r0 + lc, :] = jnp.concatenate([grow[:, r0:r0 + lc], seg, pad], axis=0).T


def _mixer_consume(slot, first, w, buf, st, x1_ref, row0, *, tm, lc):
    c_ref, n_ref, m_ref, ctail, kvc, ym_s, mix_s = st
    x1_ref[0, row0:row0 + tm, :] = ALPHA * buf["xn"][slot]
    z = buf["zqk"][slot]
    tail = ctail[...]
    ctail[...] = z[tm - SUBLANES:tm, :]
    cw = w["cw"][...]
    rows8 = lax.broadcasted_iota(jnp.int32, (SUBLANES, QK_CH), 0)
    y = cw[CONV_W - 1:CONV_W] * z + w["cb"][...]
    for sh in range(1, CONV_W):
        zr = pltpu.roll(z, sh, 0)
        head = jnp.where(rows8 < sh, pltpu.roll(tail, sh, 0), zr[0:SUBLANES])
        zr = jnp.concatenate([head, zr[SUBLANES:]], axis=0)
        y = y + cw[CONV_W - 1 - sh:CONV_W - sh] * zr
    qk = y * _sigmoid(y)

    ri = lax.broadcasted_iota(jnp.int32, (lc, lc), 0)
    ci = lax.broadcasted_iota(jnp.int32, (lc, lc), 1)
    causal = ci <= ri
    for c in range(tm // lc):
        r0 = c * lc
        for h in range(MH):
            qh = qk[r0:r0 + lc, h * DK:(h + 1) * DK]
            kh = qk[r0:r0 + lc, MH * DK + h * DK:MH * DK + (h + 1) * DK] * (DK ** -0.5)
            vh = buf["vm"][slot, r0:r0 + lc, h * DV:(h + 1) * DV]
            qb = qh.astype(BF16)
            bc = buf["gcol"][slot, r0:r0 + lc, SUBLANES + MH + h:SUBLANES + MH + h + 1]
            br = buf["brow"][slot, MH + h:MH + h + 1, r0:r0 + lc]
            li_c = buf["gcol"][slot, r0:r0 + lc, h:h + 1]
            li_r = buf["grow"][slot, h:h + 1, r0:r0 + lc]
            m_prev = m_ref[0, h:h + 1, 0:1]
            c_prev = c_ref[0, h]
            n_prev = n_ref[0, h:h + 1, :]

            dmat = jnp.where(causal, bc - br + li_r, NEG_INF)
            m_inter = bc + m_prev
            m_t = jnp.maximum(m_inter, jnp.max(dmat, -1, keepdims=True))
            wgt = jnp.exp(dmat - m_t) * _dot_nt(qb, kh.astype(BF16))
            inter = jnp.exp(m_inter - m_t)
            num = _dot(jnp.concatenate([wgt.astype(BF16), (inter * qh).astype(BF16)], axis=1),
                       jnp.concatenate([vh, c_prev.astype(BF16)], axis=0))
            den = jnp.sum(wgt, -1, keepdims=True) + inter * jnp.sum(qh * n_prev, -1, keepdims=True)
            denom = jnp.maximum(jnp.abs(den), jnp.exp(-m_t))
            hh = num / denom
            hn = hh * lax.rsqrt(jnp.mean(hh * hh, -1, keepdims=True) + RMS_EPS)
            ym_s[r0:r0 + lc, h * DV:(h + 1) * DV] = buf["smo"][slot, r0:r0 + lc, h * DV:(h + 1) * DV] * hn

            m_new = m_t[lc - 1:lc, :]
            bc_last = bc[lc - 1:lc, :]
            decay = jnp.exp(bc_last + m_prev - m_new)
            ks = kh * jnp.exp(bc_last - bc + li_c - m_new)
            c_ref[0, h] = decay * c_prev + _dot_tn(ks.astype(BF16), vh)
            n_ref[0, h:h + 1, :] = decay * n_prev + jnp.sum(ks, 0, keepdims=True)
            m_ref[0, h:h + 1, :] = jnp.broadcast_to(m_new, (1, LANES))

    lo_q = lax.broadcasted_iota(jnp.int32, (WIN, LANES), 1) < HD
    qi = lax.broadcasted_iota(jnp.int32, (WIN, 2 * WIN), 0)
    kj = lax.broadcasted_iota(jnp.int32, (WIN, 2 * WIN), 1)
    band = (kj >= qi) & (kj <= qi + WIN)
    ones_v = jnp.ones((2 * WIN, LANES), BF16)
    for j in range(tm // WIN):
        r0 = j * WIN
        if j == 0 and first is not None:
            mask = band & (kj >= jnp.where(first, WIN, 0))
        else:
            mask = band
        for g in range(KVH):
            if j == 0:
                kprev, vprev = kvc[g], kvc[KVH + g]
            else:
                kprev, vprev = buf["k2"][slot, g, r0 - WIN:r0, :], buf["v2"][slot, g, r0 - WIN:r0, :]
            kcat = jnp.concatenate([kprev, buf["k2"][slot, g, r0:r0 + WIN, :]], axis=0)
            vcat = jnp.concatenate([vprev, buf["v2"][slot, g, r0:r0 + WIN, :]], axis=0)
            sc_all = _dot_nt(buf["aq"][slot, j, g], kcat)
            es, ds = [], []
            for hh in range(GROUP):
                sink = w["sink"][g * GROUP + hh] * LOG2E
                sc = jnp.where(mask, sc_all[hh * WIN:(hh + 1) * WIN, :], NEG_INF)
                mx = jnp.maximum(jnp.max(sc, -1, keepdims=True), sink)
                es.append(jnp.exp2(sc - mx).astype(BF16))
                ds.append(jnp.exp2(sink - mx))
            o_all = _dot(jnp.concatenate(es, axis=0), jnp.concatenate([vcat, ones_v], axis=1))
            for pp in range(GROUP // 2):
                pj = g * GROUP // 2 + pp
                halves = []
                for hh in (2 * pp, 2 * pp + 1):
                    oh = o_all[hh * WIN:(hh + 1) * WIN, :]
                    halves.append(oh[:, 0:LANES] / (oh[:, LANES:2 * LANES] + ds[hh]))
                cols = slice(pj * LANES, (pj + 1) * LANES)
                ya = jnp.where(lo_q, halves[0], halves[1])
                mix_s[r0:r0 + WIN, cols] = (ym_s[r0:r0 + WIN, cols]
                                            + buf["sga"][slot, r0:r0 + WIN, cols] * ya).astype(BF16)
    for g in range(KVH):
        kvc[g] = buf["k2"][slot, g, tm - WIN:tm, :]
        kvc[KVH + g] = buf["v2"][slot, g, tm - WIN:tm, :]

    mixed = mix_s[...]
    r = jnp.concatenate([_dot(mixed, w["wout_a"][...]), _dot(mixed, w["wout_b"][...])], axis=1)
    x1_ref[0, row0:row0 + tm, :] = x1_ref[0, row0:row0 + tm, :] + r


def _mixer_kernel(*refs, tm, lc, npb, tps):
    nw = len(_MIXER_W_NAMES)
    x0_ref = refs[0]
    xt_refs = refs[1:1 + tps]
    w = dict(zip(_MIXER_W_NAMES, refs[1 + tps:1 + tps + nw]))
    n_in = 1 + tps + nw + N_CAST
    cast_in = refs[n_in - N_CAST:n_in]
    x1_ref, c_ref, n_ref, m_ref, conv_ref, kp_ref, vp_ref = refs[n_in:n_in + 7]
    cast_out = refs[n_in + 7:n_in + 7 + N_CAST]
    scratch = refs[n_in + 7 + N_CAST:]
    buf = dict(zip(_PROJ_NAMES, scratch[:len(_PROJ_NAMES)]))
    ctail, kvc, ym_s, mix_s = scratch[len(_PROJ_NAMES):]
    st = (c_ref, n_ref, m_ref, ctail, kvc, ym_s, mix_s)
    k = pl.program_id(0)
    first = (tps * k) % npb == 0

    @pl.when(k == 0)
    def _prologue():
        _mixer_project(x0_ref[0], 0, w, buf, tm=tm, lc=lc)

    @pl.when(first)
    def _init():
        c_ref[...] = jnp.zeros_like(c_ref)
        n_ref[...] = jnp.zeros_like(n_ref)
        m_ref[...] = jnp.zeros_like(m_ref)
        ctail[...] = jnp.zeros_like(ctail)
        kvc[...] = jnp.zeros_like(kvc)

    for i in range(tps):
        _mixer_project(xt_refs[i][0], (i + 1) % 2, w, buf, tm=tm, lc=lc)
        _mixer_consume(i % 2, first if i == 0 else None, w, buf, st, x1_ref, i * tm, tm=tm, lc=lc)

    for src, dst in zip(cast_in, cast_out):
        dst[...] = src[...].astype(BF16)

    @pl.when((tps * k + tps - 1) % npb == npb - 1)
    def _state_out():
        conv_ref[0] = ctail[SUBLANES - (CONV_W - 1):SUBLANES, :]
        kp_ref[0] = buf["kvl"][(tps - 1) % 2, :, 0:LANES]
        vp_ref[0] = buf["kvl"][(tps - 1) % 2, :, LANES:2 * LANES]


N_CAST = 4
BF16_ROWS = 16


def _cast_slab(rows, steps):
    tiles = rows // BF16_ROWS
    for d in range(1, tiles + 1):
        if tiles % d == 0 and tiles // d <= steps:
            return BF16_ROWS * d
    raise ValueError((rows, steps))


def _prompt_mixer(x, wts, ffn_w, *, tm=256, lc=128, tps=2):
    B, S, _ = x.shape
    npb = S // tm
    assert S % tm == 0 and tps % 2 == 0 and npb % tps == 0 and tm % lc == 0 and tm % WIN == 0
    nt = B * npb
    hpb = npb // tps
    cs = _const_spec

    def tile_map(off):
        def imap(k):
            t = jnp.minimum(tps * k + off, nt - 1)
            return (t // npb, t % npb, 0)
        return imap

    in_specs = [
        pl.BlockSpec((1, tm, D), lambda k: (0, 0, 0), pipeline_mode=pl.Buffered(1)),
        *[pl.BlockSpec((1, tm, D), tile_map(i + 1)) for i in range(tps)],
        cs((1, D)), cs((1, D)),
        cs((D, W1_W)), cs((D, W2_W)), cs((D, W3_W)), cs((SUBLANES, D)), cs((SUBLANES, 1)),
        cs((CONV_W, QK_CH)), cs((1, QK_CH)), cs((1, D)),
        pl.BlockSpec(memory_space=pltpu.SMEM),
        cs((D, W_HALF)), cs((D, W_HALF)),
    ]
    steps = nt // tps
    cast_specs = []
    for a in ffn_w:
        assert a.shape[0] % BF16_ROWS == 0
        slab = _cast_slab(a.shape[0], steps)
        last = a.shape[0] // slab - 1
        cast_specs.append(pl.BlockSpec((slab, a.shape[1]), lambda k, last=last: (jnp.minimum(k, last), 0)))
    in_specs += cast_specs
    out_shape = (
        jax.ShapeDtypeStruct((B, S, D), F32),
        jax.ShapeDtypeStruct((B, MH, DK, DV), F32),
        jax.ShapeDtypeStruct((B, MH, DK), F32),
        jax.ShapeDtypeStruct((B, SUBLANES, LANES), F32),
        jax.ShapeDtypeStruct((B, CONV_W - 1, QK_CH), F32),
        jax.ShapeDtypeStruct((B, WIN, KVH * HD), F32),
        jax.ShapeDtypeStruct((B, WIN, KVH * HD), F32),
        *[jax.ShapeDtypeStruct(a.shape, BF16) for a in ffn_w],
    )
    out_specs = (
        pl.BlockSpec((1, tps * tm, D), lambda k: (k // hpb, k % hpb, 0)),
        pl.BlockSpec((1, MH, DK, DV), lambda k: (k // hpb, 0, 0, 0)),
        pl.BlockSpec((1, MH, DK), lambda k: (k // hpb, 0, 0)),
        pl.BlockSpec((1, SUBLANES, LANES), lambda k: (k // hpb, 0, 0)),
        pl.BlockSpec((1, CONV_W - 1, QK_CH), lambda k: (k // hpb, 0, 0)),
        pl.BlockSpec((1, WIN, KVH * HD), lambda k: (k // hpb, 0, 0)),
        pl.BlockSpec((1, WIN, KVH * HD), lambda k: (k // hpb, 0, 0)),
        *cast_specs,
    )
    proj_scratch = {
        "xn": pltpu.VMEM((2, tm, D), F32), "zqk": pltpu.VMEM((2, tm, QK_CH), F32),
        "vm": pltpu.VMEM((2, tm, D), BF16), "smo": pltpu.VMEM((2, tm, D), F32),
        "aq": pltpu.VMEM((2, tm // WIN, KVH, GROUP * WIN, LANES), BF16),
        "k2": pltpu.VMEM((2, KVH, tm, LANES), BF16), "v2": pltpu.VMEM((2, KVH, tm, LANES), BF16),
        "kvl": pltpu.VMEM((2, WIN, 2 * LANES), F32),
        "sga": pltpu.VMEM((2, tm, D), F32),
        "gcol": pltpu.VMEM((2, tm, LANES), F32), "grow": pltpu.VMEM((2, SUBLANES, tm), F32),
        "brow": pltpu.VMEM((2, SUBLANES, tm), F32),
    }
    scratch = [proj_scratch[n] for n in _PROJ_NAMES] + [
        pltpu.VMEM((SUBLANES, QK_CH), F32),
        pltpu.VMEM((2 * KVH, WIN, LANES), BF16),
        pltpu.VMEM((tm, D), F32),
        pltpu.VMEM((tm, D), BF16),
    ]
    return pl.pallas_call(
        functools.partial(_mixer_kernel, tm=tm, lc=lc, npb=npb, tps=tps),
        grid=(nt // tps,), in_specs=in_specs, out_specs=out_specs, out_shape=out_shape,
        scratch_shapes=scratch, name="prompt_mixer",
        compiler_params=pltpu.CompilerParams(dimension_semantics=("arbitrary",), vmem_limit_bytes=VMEM_LIMIT),
    )(*([x] * (tps + 1)), wts["ln_in_g"], wts["ln_in_b"], wts["w1"], wts["w2"], wts["w3"], wts["w_ift"], wts["b_ift"],
      wts["conv_w"], wts["conv_b"], wts["m_norm_g"], wts["sinks"], wts["w_out_a"], wts["w_out_b"], *ffn_w)


FFN_CHUNKS = ((0, 6 * MXU_DIM), (6 * MXU_DIM, DFF))


def _ffn_kernel(x_ref, p_ref, l1g_ref, l1b_ref, wgu_ref, wd_ref, wpg_ref, wp_ref, g_ref, b_ref, o_ref, *, sub):
    n = x_ref.shape[0] // sub

    def pre(j):
        x = _layer_norm(x_ref[j * sub:(j + 1) * sub, :], l1g_ref[...], l1b_ref[...])
        return x, x.astype(BF16)

    def body(j, x, xb):
        acc = ALPHA * x + (_sigmoid(_dot(xb, wpg_ref[...]))
                           * _dot(p_ref[j * sub:(j + 1) * sub, :].astype(BF16), wp_ref[...]))
        for c0, c1 in FFN_CHUNKS:
            g = _dot(xb, wgu_ref[:, c0:c1])
            u = _dot(xb, wgu_ref[:, DFF + c0:DFF + c1])
            hcat = (g * _sigmoid(g) * u).astype(BF16)
            acc = acc + _dot(hcat, wd_ref[c0:c1, :])
        return acc

    def post(j, acc):
        o_ref[j * sub:(j + 1) * sub, :] = _layer_norm(acc, g_ref[...], b_ref[...])

    cur = pre(0)
    acc_prev = None
    for j in range(n):
        nxt = pre(j + 1) if j + 1 < n else None
        acc = body(j, *cur)
        if acc_prev is not None:
            post(j - 1, acc_prev)
        acc_prev, cur = acc, nxt
    post(n - 1, acc_prev)


def _ffn(x, p, wts, *, tm, sub):
    N = x.shape[0]
    assert N % tm == 0 and tm % sub == 0
    cs = _const_spec
    return pl.pallas_call(
        functools.partial(_ffn_kernel, sub=sub),
        grid=(N // tm,),
        in_specs=[pl.BlockSpec((tm, D), lambda i: (i, 0)), pl.BlockSpec((tm, PD), lambda i: (i, 0)),
                  cs((1, D)), cs((1, D)),
                  cs((D, 2 * DFF)), cs((DFF, D)), cs((D, D)), cs((PD, D)), cs((1, D)), cs((1, D))],
        out_specs=pl.BlockSpec((tm, D), lambda i: (i, 0)),
        out_shape=jax.ShapeDtypeStruct((N, D), F32),
        name="ffn",
        compiler_params=pltpu.CompilerParams(dimension_semantics=("arbitrary",), vmem_limit_bytes=VMEM_LIMIT),
    )(x, p, wts["ln1_g"], wts["ln1_b"], wts["w_gu"], wts["w_d"], wts["w_pg"], wts["w_p"],
      wts["ln2_g"], wts["ln2_b"])


Z_V = 0
Z_MO = Z_V + D
Z_GM = Z_MO + D
Z_GA = Z_GM + D
Z_XN = Z_GA + D
Z_Q = Z_XN + D
Z_KW = Z_Q + MH * DK
Z_SC = Z_KW + MH * DK
Z_KN = Z_SC + LANES
Z_VN = Z_KN + LANES
Z_W = Z_VN + LANES


def _sample_pre_kernel(x_ref, lng_ref, lnb_ref, w1_ref, w2_ref, w3_ref, wif_ref, bif_ref,
                       cw_ref, cb_ref, cs_ref, n_ref, m_ref,
                       z_ref, qs_ref, convo_ref, no_ref, mo_ref, *, nb, tb):
    w = {"w1": w1_ref, "w2": w2_ref, "w3": w3_ref}
    xn = _layer_norm(x_ref[...], lng_ref[...], lnb_ref[...])
    xb = xn.astype(BF16)
    zqk = _proj(xb, w, "qk")
    cw = cw_ref[...]
    y = cw[3:4] * zqk + cw[2:3] * cs_ref[2] + cw[1:2] * cs_ref[1] + cw[0:1] * cs_ref[0] + cb_ref[...]
    qk = y * _sigmoid(y)
    convo_ref[0] = cs_ref[1]
    convo_ref[1] = cs_ref[2]
    convo_ref[2] = zqk

    gcol = _dot(xb, wif_ref[...]) + bif_ref[...]
    ig = gcol[:, 0:MH]
    m_inter = _log_sigmoid(gcol[:, MH:2 * MH]) + m_ref[...]
    m_t = jnp.maximum(m_inter, ig)
    inter = jnp.exp(m_inter - m_t)
    wi = jnp.exp(ig - m_t)
    enm = jnp.exp(-m_t)
    mo_ref[...] = m_t

    z_ref[:, Z_V:Z_V + D] = _proj(xb, w, "mv")
    z_ref[:, Z_MO:Z_MO + W_HALF] = _proj(xb, w, "mo_lo")
    z_ref[:, Z_MO + W_HALF:Z_MO + D] = _proj(xb, w, "mo_hi")
    z_ref[:, Z_GM:Z_GM + D] = _proj(xb, w, "gm")
    z_ref[:, Z_GA:Z_GA + D] = _proj(xb, w, "ga")
    z_ref[:, Z_XN:Z_XN + D] = xn
    z_ref[:, Z_Q:Z_Q + MH * DK] = qk[:, 0:MH * DK]
    sc = jnp.zeros((nb, LANES), F32)
    lane = lax.broadcasted_iota(jnp.int32, (nb, LANES), 1)
    for h in range(MH):
        qh = qk[:, h * DK:(h + 1) * DK]
        kh = qk[:, MH * DK + h * DK:MH * DK + (h + 1) * DK] * (DK ** -0.5)
        nh = n_ref[:, h * DK:(h + 1) * DK]
        wv = wi[:, h:h + 1] * jnp.sum(qh * kh, -1, keepdims=True)
        den = wv + inter[:, h:h + 1] * jnp.sum(qh * nh, -1, keepdims=True)
        denom = jnp.maximum(jnp.abs(den), enm[:, h:h + 1])
        kw = wi[:, h:h + 1] * kh
        z_ref[:, Z_KW + h * DK:Z_KW + (h + 1) * DK] = kw
        no_ref[:, h * DK:(h + 1) * DK] = inter[:, h:h + 1] * nh + kw
        sc = jnp.where(lane == h, inter[:, h:h + 1], sc)
        sc = jnp.where(lane == MH + h, wv, sc)
        sc = jnp.where(lane == 2 * MH + h, denom, sc)
    z_ref[:, Z_SC:Z_SC + LANES] = sc

    akv = _proj(xb, w, "akv")
    z_ref[:, Z_KN:Z_KN + LANES] = akv[:, 0:LANES]
    z_ref[:, Z_VN:Z_VN + LANES] = akv[:, LANES:2 * LANES]
    aq = _proj(xb, w, "aq") * (HD ** -0.5)
    lo = lax.broadcasted_iota(jnp.int32, (nb, LANES), 1) < HD
    for h in range(AH):
        pair = aq[:, (h // 2) * LANES:(h // 2 + 1) * LANES]
        qs_ref[pl.ds(h, nb, stride=AH), :] = jnp.where(lo, pair if h % 2 == 0 else pltpu.roll(pair, HD, 1), 0.0)


C_RING = 3


def _c_block_copy(c_hbm, cbuf, sems, step, tb):
    slot = step % C_RING
    return pltpu.make_async_copy(c_hbm.at[pl.ds(step * tb, tb)], cbuf.at[slot], sems.at[slot])


def _sample_state_kernel(z_ref, qs_ref, sink_ref, c_hbm, ck_ref, cv_ref,
                         co_ref, qc_ref, os_ref, ko_ref, vo_ref, cbuf, csem, *, tb):
    s = pl.program_id(0)
    n_steps = pl.num_programs(0)

    @pl.when(s == 0)
    def _prime():
        _c_block_copy(c_hbm, cbuf, csem, 0, tb).start()
        _c_block_copy(c_hbm, cbuf, csem, 1, tb).start()

    @pl.when(s + 2 < n_steps)
    def _prefetch():
        _c_block_copy(c_hbm, cbuf, csem, s + 2, tb).start()

    _c_block_copy(c_hbm, cbuf, csem, s, tb).wait()
    c_ref = cbuf.at[s % C_RING]

    lo = lax.broadcasted_iota(jnp.int32, (WIN, LANES), 1) < HD
    newest = lax.broadcasted_iota(jnp.int32, (HD, WIN), 1) == WIN - 1
    sinks = sink_ref[...]
    q_all = z_ref[:, Z_Q:Z_Q + MH * DK]
    kw_all = z_ref[:, Z_KW:Z_KW + MH * DK]
    v_all = z_ref[:, Z_V:Z_V + D]
    sc = z_ref[:, Z_SC:Z_SC + LANES]
    kn_all = z_ref[:, Z_KN:Z_KN + LANES]
    vn_all = z_ref[:, Z_VN:Z_VN + LANES]
    kn2_all = _dup_halves(kn_all, lo[0:tb])
    vn2_all = _dup_halves(vn_all, lo[0:tb])
    rpad = jnp.zeros((LANES - tb, LANES), F32)
    knt = [jnp.concatenate([a, rpad], axis=0).T for a in kn2_all]
    vnt = [jnp.concatenate([a, rpad], axis=0).T for a in vn2_all]
    lane_sel = lax.broadcasted_iota(jnp.int32, (DK, LANES), 1)
    for h in range(MH):
        q_h = q_all[:, h * DK:(h + 1) * DK].astype(BF16)
        kwt_h = jnp.concatenate([kw_all[:, h * DK:(h + 1) * DK], rpad], axis=0).T
        v_h = jnp.concatenate([v_all[:, h * DV:(h + 1) * DV], jnp.zeros((LANES - tb, DV), F32)], axis=0).astype(BF16)
        for i in range(tb):
            cb = c_ref[i, h]
            qc = _dot(q_h, cb.astype(BF16))
            qc_ref[i:i + 1, h * DV:(h + 1) * DV] = qc[i:i + 1]
            outer = _dot(jnp.where(lane_sel == i, kwt_h, 0.0).astype(BF16), v_h)
            co_ref[i, h] = sc[i:i + 1, h:h + 1] * cb + outer

    q_rows = qs_ref[:, 0:HD]
    kn_rows = jnp.concatenate([jnp.broadcast_to(kn2_all[g][i:i + 1, 0:HD], (GROUP, HD))
                               for i in range(tb) for g in range(KVH)], axis=0)
    vn_rows = jnp.concatenate([jnp.broadcast_to(vn2_all[g][i:i + 1, 0:HD], (GROUP, HD))
                               for i in range(tb) for g in range(KVH)], axis=0)
    sink_rows = jnp.concatenate([sinks] * tb, axis=0)
    q_bf = q_rows.astype(BF16)
    s = jnp.concatenate([_dot(q_bf[(i * KVH + g) * GROUP:(i * KVH + g + 1) * GROUP], ck_ref[i, g].astype(BF16))
                         for i in range(tb) for g in range(KVH)], axis=0)
    s_self = jnp.sum(q_rows * kn_rows, -1, keepdims=True)
    mx = jnp.maximum(jnp.maximum(jnp.max(s, -1, keepdims=True), s_self), sink_rows)
    e = jnp.exp(s - mx)
    es = jnp.exp(s_self - mx)
    dsum = jnp.sum(e, -1, keepdims=True) + es + jnp.exp(sink_rows - mx)
    e_bf = e.astype(BF16)
    pv = jnp.concatenate([_dot_nt(e_bf[(i * KVH + g) * GROUP:(i * KVH + g + 1) * GROUP], cv_ref[i, g].astype(BF16))
                          for i in range(tb) for g in range(KVH)], axis=0)
    o = (pv + es * vn_rows) / dsum
    os_ref[...] = jnp.concatenate([o, jnp.zeros((tb * AH, LANES - HD), F32)], axis=1)
    for i in range(tb):
        for g in range(KVH):
            kcol = jnp.broadcast_to(knt[g][0:HD, i:i + 1], (HD, WIN))
            vcol = jnp.broadcast_to(vnt[g][0:HD, i:i + 1], (HD, WIN))
            ko_ref[i, g] = jnp.where(newest, kcol, pltpu.roll(ck_ref[i, g], WIN - 1, 1))
            vo_ref[i, g] = jnp.where(newest, vcol, pltpu.roll(cv_ref[i, g], WIN - 1, 1))


def _sample_post_kernel(z_ref, qc_ref, os_ref, mng_ref, wouta_ref, woutb_ref, x1_ref, *, nb, tb):
    sc = z_ref[:, Z_SC:Z_SC + LANES]
    mng = mng_ref[...]
    lo = lax.broadcasted_iota(jnp.int32, (nb, LANES), 1) < HD
    ym = []
    for h in range(MH):
        v = z_ref[:, Z_V + h * DV:Z_V + (h + 1) * DV]
        num = sc[:, MH + h:MH + h + 1] * v + sc[:, h:h + 1] * qc_ref[:, h * DV:(h + 1) * DV]
        hh = num / sc[:, 2 * MH + h:2 * MH + h + 1]
        hn = hh * lax.rsqrt(jnp.mean(hh * hh, -1, keepdims=True) + RMS_EPS)
        ym.append(_sigmoid(z_ref[:, Z_MO + h * DV:Z_MO + (h + 1) * DV]) * hn * mng[:, h * DV:(h + 1) * DV])
    ym = jnp.concatenate(ym, axis=1)
    ya = []
    for pj in range(AH // 2):
        halves = []
        for par in range(2):
            halves.append(os_ref[pl.ds(2 * pj + par, nb, stride=AH), :])
        ya.append(jnp.where(lo, halves[0], pltpu.roll(halves[1], HD, 1)))
    ya = jnp.concatenate(ya, axis=1)
    mixed = _sigmoid(z_ref[:, Z_GM:Z_GM + D]) * ym + _sigmoid(z_ref[:, Z_GA:Z_GA + D]) * ya
    mixed = mixed.astype(BF16)
    r = jnp.concatenate([_dot(mixed, wouta_ref[...]), _dot(mixed, woutb_ref[...])], axis=1)
    x1_ref[...] = ALPHA * z_ref[:, Z_XN:Z_XN + D] + r


def _sample_mixer(x, c0, n0, m0, conv0, k0, v0, wts, *, tb=8):
    nb = x.shape[0]
    assert nb % tb == 0 and nb // tb >= 2
    vmem = pltpu.CompilerParams(dimension_semantics=("arbitrary",), vmem_limit_bytes=VMEM_LIMIT)
    full = lambda shape: pl.BlockSpec(shape, lambda i: (0,) * len(shape))
    pre_in = [x, wts["ln_in_g"], wts["ln_in_b"], wts["w1"], wts["w2"], wts["w3"], wts["w_if"], wts["b_if"],
              wts["conv_w"], wts["conv_b"], conv0, n0, m0]
    z, qs, conv_new, n_new, m_new = pl.pallas_call(
        functools.partial(_sample_pre_kernel, nb=nb, tb=tb),
        grid=(1,),
        in_specs=[full(a.shape) for a in pre_in],
        out_specs=(full((nb, Z_W)), full((nb * AH, LANES)), full((CONV_W - 1, nb, QK_CH)),
                   full((nb, MH * DK)), full((nb, MH))),
        out_shape=(jax.ShapeDtypeStruct((nb, Z_W), F32), jax.ShapeDtypeStruct((nb * AH, LANES), F32),
                   jax.ShapeDtypeStruct((CONV_W - 1, nb, QK_CH), F32),
                   jax.ShapeDtypeStruct((nb, MH * DK), F32), jax.ShapeDtypeStruct((nb, MH), F32)),
        name="sample_pre", compiler_params=vmem,
    )(*pre_in)

    c_new, qc, os_, k_new, v_new = pl.pallas_call(
        functools.partial(_sample_state_kernel, tb=tb),
        grid=(nb // tb,),
        in_specs=[pl.BlockSpec((tb, Z_W), lambda i: (i, 0)),
                  pl.BlockSpec((tb * AH, LANES), lambda i: (i, 0)),
                  pl.BlockSpec((AH, 1), lambda i: (0, 0)),
                  pl.BlockSpec(memory_space=pl.ANY),
                  pl.BlockSpec((tb, KVH, HD, WIN), lambda i: (i, 0, 0, 0)),
                  pl.BlockSpec((tb, KVH, HD, WIN), lambda i: (i, 0, 0, 0))],
        out_specs=(pl.BlockSpec((tb, MH, DK, DV), lambda i: (i, 0, 0, 0)),
                   pl.BlockSpec((tb, D), lambda i: (i, 0)),
                   pl.BlockSpec((tb * AH, LANES), lambda i: (i, 0)),
                   pl.BlockSpec((tb, KVH, HD, WIN), lambda i: (i, 0, 0, 0)),
                   pl.BlockSpec((tb, KVH, HD, WIN), lambda i: (i, 0, 0, 0))),
        out_shape=(jax.ShapeDtypeStruct((nb, MH, DK, DV), F32), jax.ShapeDtypeStruct((nb, D), F32),
                   jax.ShapeDtypeStruct((nb * AH, LANES), F32),
                   jax.ShapeDtypeStruct((nb, KVH, HD, WIN), F32), jax.ShapeDtypeStruct((nb, KVH, HD, WIN), F32)),
        scratch_shapes=[pltpu.VMEM((C_RING, tb, MH, DK, DV), F32), pltpu.SemaphoreType.DMA((C_RING,))],
        name="sample_state", compiler_params=vmem,
    )(z, qs, wts["sinks_col"], c0, k0, v0)

    post_in = [z, qc, os_, wts["m_norm_g"], wts["w_out_a"], wts["w_out_b"]]
    x1 = pl.pallas_call(
        functools.partial(_sample_post_kernel, nb=nb, tb=tb),
        grid=(1,),
        in_specs=[full(a.shape) for a in post_in],
        out_specs=full((nb, D)),
        out_shape=jax.ShapeDtypeStruct((nb, D), F32),
        name="sample_post", compiler_params=vmem,
    )(*post_in)
    return x1, c_new, n_new, m_new, conv_new, k_new, v_new


def _regroup_kernel(wt_ref, w1_ref, w2_ref, w3_ref, wif_ref, wift_ref):
    cols = lambda a, b: wt_ref[a:b, :].T.astype(BF16)
    w1_ref[:, 0:2 * D] = cols(O_QK, O_MI)
    w1_ref[:, 2 * D:W1_W] = cols(O_MO, O_MO + W_HALF)
    w2_ref[:, 0:W_HALF + D] = cols(O_MO + W_HALF, O_AK)
    w2_ref[:, W_HALF + D:W2_W] = cols(O_GM, O_GA)
    w3_ref[:, 0:D] = cols(O_GA, IN_WIDTH)
    w3_ref[:, D:W3_W] = cols(O_AK, O_GM)
    gates_t = wt_ref[O_MI:O_MO, :]
    rb = gates_t.shape[1]
    wift_ref[...] = gates_t.astype(BF16)
    wif_ref[...] = jnp.concatenate([gates_t, jnp.zeros((LANES - 2 * MH, rb), F32)], axis=0).T.astype(BF16)


def _regroup_w_in(wt, *, rb=256):
    assert D % rb == 0 and rb % LANES == 0
    return pl.pallas_call(
        _regroup_kernel, grid=(D // rb,),
        in_specs=[pl.BlockSpec((IN_WIDTH, rb), lambda i: (0, i))],
        out_specs=(pl.BlockSpec((rb, W1_W), lambda i: (i, 0)), pl.BlockSpec((rb, W2_W), lambda i: (i, 0)),
                   pl.BlockSpec((rb, W3_W), lambda i: (i, 0)), pl.BlockSpec((rb, LANES), lambda i: (i, 0)),
                   pl.BlockSpec((2 * MH, rb), lambda i: (0, i))),
        out_shape=(jax.ShapeDtypeStruct((D, W1_W), BF16), jax.ShapeDtypeStruct((D, W2_W), BF16),
                   jax.ShapeDtypeStruct((D, W3_W), BF16), jax.ShapeDtypeStruct((D, LANES), BF16),
                   jax.ShapeDtypeStruct((2 * MH, D), BF16)),
        name="regroup_w_in",
        compiler_params=pltpu.CompilerParams(dimension_semantics=("arbitrary",), vmem_limit_bytes=VMEM_LIMIT),
    )(wt)


def _prep_weights(ln_in_g, ln_in_b, w_in, b_igate, b_fgate, conv_w, conv_b, m_norm_g, attn_sinks, w_out,
                  ln1_g, ln1_b, w_gate_up, w_down, ln2_g, ln2_b, w_ple, w_ple_gate):
    row = lambda a: a.reshape(1, -1).astype(F32)
    w1, w2, w3, w_if, w_ift = _regroup_w_in(jnp.transpose(w_in[0], (1, 0)))
    b_if = jnp.pad(jnp.concatenate([b_igate[0], b_fgate[0]]), (0, LANES - 2 * MH))
    wo = w_out[0].astype(BF16)
    return {
        "ln_in_g": row(ln_in_g), "ln_in_b": row(ln_in_b),
        "w1": w1, "w2": w2, "w3": w3,
        "w_if": w_if, "w_ift": w_ift,
        "b_if": b_if.reshape(1, LANES), "b_ift": b_if[:2 * MH].reshape(2 * MH, 1),
        "conv_w": conv_w[0], "conv_b": row(conv_b[0]), "m_norm_g": row(m_norm_g[0]),
        "sinks": attn_sinks[0], "sinks_col": attn_sinks[0].reshape(AH, 1),
        "w_out_a": wo[:, :W_HALF], "w_out_b": wo[:, W_HALF:], "ln1_g": row(ln1_g[0]), "ln1_b": row(ln1_b[0]),
        "ffn_f32": (w_gate_up[0], w_down[0], w_ple_gate[0], w_ple[0]),
        "ln2_g": row(ln2_g[0]), "ln2_b": row(ln2_b[0]),
    }


def _prompt_path(x, p, wts, *, tm=256, lc=128, tmf=1024):
    B, S, _ = x.shape
    x1, c, n, m, conv, k, v, w_gu, w_d, w_pg, w_p = _prompt_mixer(x, wts, wts["ffn_f32"], tm=tm, lc=lc)
    wts = {**wts, "w_gu": w_gu, "w_d": w_d, "w_pg": w_pg, "w_p": w_p}
    y = _ffn(x1.reshape(B * S, D), p.reshape(B * S, PD), wts, tm=tmf, sub=tmf // 4).reshape(B, S, D)
    return wts, (y, c[None], n[None], m[None, :, :MH, 0], conv[None],
                 k.reshape(1, B, WIN, KVH, HD), v.reshape(1, B, WIN, KVH, HD))


def _sample_path(x, p, c0, n0, m0, conv0, k0, v0, wts, *, tb=8):
    nb = x.shape[0]
    x1, c, n, m, conv, k, v = _sample_mixer(
        x.reshape(nb, D), c0[0], n0[0].reshape(nb, MH * DK), m0[0],
        jnp.transpose(conv0[0], (1, 0, 2)), jnp.transpose(k0[0], (0, 2, 3, 1)),
        jnp.transpose(v0[0], (0, 2, 3, 1)), wts, tb=tb)
    y = _ffn(x1, p.reshape(nb, PD), wts, tm=nb, sub=nb).reshape(nb, 1, D)
    return (y, c[None], n.reshape(1, nb, MH, DK), m[None], jnp.transpose(conv, (1, 0, 2))[None],
            jnp.transpose(k, (0, 3, 1, 2))[None], jnp.transpose(v, (0, 3, 1, 2))[None])


def kernel(x_prompt, x_sample, state_mlstm_C, state_mlstm_n, state_mlstm_m, state_conv, cache_win_k, cache_win_v, p_prompt, p_sample, ln_in_g, ln_in_b, w_in, b_igate, b_fgate, conv_w, conv_b, m_norm_g, attn_sinks, w_out, ln1_g, ln1_b, w_gate_up, w_down, ln2_g, ln2_b, w_ple, w_ple_gate):
    wts = _prep_weights(ln_in_g, ln_in_b, w_in, b_igate, b_fgate, conv_w, conv_b, m_norm_g, attn_sinks,
                        w_out, ln1_g, ln1_b, w_gate_up, w_down, ln2_g, ln2_b, w_ple, w_ple_gate)
    wts, (yp, c_p, n_p, m_p, conv_p, k_p, v_p) = _prompt_path(x_prompt, p_prompt[0], wts)
    ys, c_s, n_s, m_s, conv_s, k_s, v_s = _sample_path(
        x_sample, p_sample[0], state_mlstm_C, state_mlstm_n, state_mlstm_m, state_conv,
        cache_win_k, cache_win_v, wts)
    return (yp, ys, c_p, n_p, m_p, conv_p, k_p, v_p, c_s, n_s, m_s, conv_s, k_s, v_s)
```

```python
import functools

import jax
import jax.numpy as jnp
from jax import lax
from jax.experimental import pallas as pl
from jax.experimental.pallas import tpu as pltpu

F32 = jnp.float32
BF16 = jnp.bfloat16

D = 1024
MH = 4
DV = D // MH
DK = DV // 2
CONV_W = 4
QK_CH = 2 * MH * DK
AH = 16
KVH = 2
HD = D // AH
GROUP = AH // KVH
WIN = 128
DFF = 2816
PD = 256
LN_EPS = 1e-5
RMS_EPS = 1e-6
ALPHA = 2.0 ** 0.25
LOG2E = 1.4426950408889634
NEG_INF = float("-inf")

LANES = 128
SUBLANES = 8
MXU_DIM = 256
VMEM_LIMIT = 56 * 1024 * 1024

O_QK = 0
O_MV = O_QK + QK_CH
O_MI = O_MV + D
O_MF = O_MI + MH
O_MO = O_MF + MH
O_AQ = O_MO + D
O_AK = O_AQ + D
O_AV = O_AK + KVH * HD
O_GM = O_AV + KVH * HD
O_GA = O_GM + D
IN_WIDTH = O_GA + D

W_HALF = D // 2
W1_W = 2 * D + W_HALF
W2_W = 2 * D + W_HALF
W3_W = D + 2 * KVH * HD
_WCOL = {"qk": ("w1", 0, D), "mv": ("w1", D, 2 * D), "mo_lo": ("w1", 2 * D, W1_W),
         "mo_hi": ("w2", 0, W_HALF), "aq": ("w2", W_HALF, W_HALF + D), "gm": ("w2", W_HALF + D, W2_W),
         "ga": ("w3", 0, D), "akv": ("w3", D, W3_W)}


def _dot(a, b):
    return jnp.dot(a, b, preferred_element_type=F32)


def _dot_nt(a, b):
    return lax.dot_general(a, b, (((1,), (1,)), ((), ())), preferred_element_type=F32)


def _dot_tn(a, b):
    return lax.dot_general(a, b, (((0,), (0,)), ((), ())), preferred_element_type=F32)


def _proj(xb, w, name):
    blk, a, b = _WCOL[name]
    return _dot(xb, w[blk][:, a:b])


def _layer_norm(x, g, b):
    mu = jnp.mean(x, -1, keepdims=True)
    xc = x - mu
    var = jnp.mean(xc * xc, -1, keepdims=True)
    return xc * lax.rsqrt(var + LN_EPS) * g + b


def _sigmoid(x):
    return 1.0 / (1.0 + jnp.exp(-x))


def _log_sigmoid(x):
    return jnp.minimum(x, 0.0) - jnp.log(1.0 + jnp.exp(-jnp.abs(x)))


def _const_spec(shape):
    zeros = (0,) * len(shape)
    return pl.BlockSpec(shape, lambda i: zeros, pipeline_mode=pl.Buffered(1))


def _dup_halves(x, lo):
    xr = pltpu.roll(x, HD, 1)
    return jnp.where(lo, x, xr), jnp.where(lo, xr, x)


_PROJ_NAMES = ("xn", "zqk", "vm", "smo", "aq", "k2", "v2", "kvl", "sga", "gcol", "grow", "brow")
_MIXER_W_NAMES = ("lng", "lnb", "w1", "w2", "w3", "wift", "bift", "cw", "cb", "mng", "sink", "wout_a", "wout_b")


def _mixer_project(x, slot, w, buf, *, tm, lc):
    xn = _layer_norm(x, w["lng"][...], w["lnb"][...])
    xb = xn.astype(BF16)
    buf["xn"][slot] = xn
    buf["zqk"][slot] = _proj(xb, w, "qk")
    buf["vm"][slot] = _proj(xb, w, "mv").astype(BF16)
    mo = jnp.concatenate([_proj(xb, w, "mo_lo"), _proj(xb, w, "mo_hi")], axis=1)
    buf["smo"][slot] = _sigmoid(mo) * w["mng"][...] * _sigmoid(_proj(xb, w, "gm"))
    aq = _proj(xb, w, "aq") * (HD ** -0.5 * LOG2E)
    lo_d = (lax.broadcasted_iota(jnp.int32, (tm, D), 1) & (LANES - 1)) < HD
    aqm = (jnp.where(lo_d, aq, 0.0).astype(BF16), jnp.where(lo_d, 0.0, aq).astype(BF16))
    for j in range(tm // WIN):
        for h in range(AH):
            buf["aq"][slot, j, h // GROUP, (h % GROUP) * WIN:(h % GROUP + 1) * WIN, :] = (
                aqm[h % 2][j * WIN:(j + 1) * WIN, (h // 2) * LANES:(h // 2 + 1) * LANES])
    akv = _proj(xb, w, "akv")
    lo = lax.broadcasted_iota(jnp.int32, (tm, LANES), 1) < HD
    k2 = _dup_halves(akv[:, 0:LANES], lo)
    v2 = _dup_halves(akv[:, LANES:2 * LANES], lo)
    for g in range(KVH):
        buf["k2"][slot, g] = k2[g].astype(BF16)
        buf["v2"][slot, g] = v2[g].astype(BF16)
    buf["kvl"][slot] = akv[tm - WIN:tm, :]
    buf["sga"][slot] = _sigmoid(_proj(xb, w, "ga"))
    grow = _dot_nt(w["wift"][...], xb) + w["bift"][...]
    buf["grow"][slot] = grow
    lsrow = _log_sigmoid(grow)
    lane = lax.broadcasted_iota(jnp.int32, (SUBLANES, lc), 1)
    pad = jnp.zeros((lc - 2 * SUBLANES, lc), F32)
    for c in range(tm // lc):
        r0 = c * lc
        seg = lsrow[:, r0:r0 + lc]
        sh = 1
        while sh < lc:
            seg = seg + jnp.where(lane >= sh, pltpu.roll(seg, sh, 1), 0.0)
            sh *= 2
        buf["brow"][slot, :, r0:r0 + lc] = seg
        buf["gcol"][slot, r0:r0 + lc, :] = jnp.concatenate([grow[:, r0:r0 + lc], seg, pad], axis=0).T


def _mixer_consume(slot, first, w, buf, st, x1_ref, row0, *, tm, lc):
    c_ref, n_ref, m_ref, ctail, kvc, ym_s, mix_s = st
    x1_ref[0, row0:row0 + tm, :] = ALPHA * buf["xn"][slot]
    z = buf["zqk"][slot]
    tail = ctail[...]
    ctail[...] = z[tm - SUBLANES:tm, :]
    cw = w["cw"][...]
    rows8 = lax.broadcasted_iota(jnp.int32, (SUBLANES, QK_CH), 0)
    y = cw[CONV_W - 1:CONV_W] * z + w["cb"][...]
    for sh in range(1, CONV_W):
        zr = pltpu.roll(z, sh, 0)
        head = jnp.where(rows8 < sh, pltpu.roll(tail, sh, 0), zr[0:SUBLANES])
        zr = jnp.concatenate([head, zr[SUBLANES:]], axis=0)
        y = y + cw[CONV_W - 1 - sh:CONV_W - sh] * zr
    qk = y * _sigmoid(y)

    ri = lax.broadcasted_iota(jnp.int32, (lc, lc), 0)
    ci = lax.broadcasted_iota(jnp.int32, (lc, lc), 1)
    causal = ci <= ri
    for c in range(tm // lc):
        r0 = c * lc
        for h in range(MH):
            qh = qk[r0:r0 + lc, h * DK:(h + 1) * DK]
            kh = qk[r0:r0 + lc, MH * DK + h * DK:MH * DK + (h + 1) * DK] * (DK ** -0.5)
            vh = buf["vm"][slot, r0:r0 + lc, h * DV:(h + 1) * DV]
            qb = qh.astype(BF16)
            bc = buf["gcol"][slot, r0:r0 + lc, SUBLANES + MH + h:SUBLANES + MH + h + 1]
            br = buf["brow"][slot, MH + h:MH + h + 1, r0:r0 + lc]
            li_c = buf["gcol"][slot, r0:r0 + lc, h:h + 1]
            li_r = buf["grow"][slot, h:h + 1, r0:r0 + lc]
            m_prev = m_ref[0, h:h + 1, 0:1]
            c_prev = c_ref[0, h]
            n_prev = n_ref[0, h:h + 1, :]

            dmat = jnp.where(causal, bc - br + li_r, NEG_INF)
            m_inter = bc + m_prev
            m_t = jnp.maximum(m_inter, jnp.max(dmat, -1, keepdims=True))
            wgt = jnp.exp(dmat - m_t) * _dot_nt(qb, kh.astype(BF16))
            inter = jnp.exp(m_inter - m_t)
            num = _dot(jnp.concatenate([wgt.astype(BF16), (inter * qh).astype(BF16)], axis=1),
                       jnp.concatenate([vh, c_prev.astype(BF16)], axis=0))
            den = jnp.sum(wgt, -1, keepdims=True) + inter * jnp.sum(qh * n_prev, -1, keepdims=True)
            denom = jnp.maximum(jnp.abs(den), jnp.exp(-m_t))
            hh = num / denom
            hn = hh * lax.rsqrt(jnp.mean(hh * hh, -1, keepdims=True) + RMS_EPS)
            ym_s[r0:r0 + lc, h * DV:(h + 1) * DV] = buf["smo"][slot, r0:r0 + lc, h * DV:(h + 1) * DV] * hn

            m_new = m_t[lc - 1:lc, :]
            bc_last = bc[lc - 1:lc, :]
            decay = jnp.exp(bc_last + m_prev - m_new)
            ks = kh * jnp.exp(bc_last - bc + li_c - m_new)
            c_ref[0, h] = decay * c_prev + _dot_tn(ks.astype(BF16), vh)
            n_ref[0, h:h + 1, :] = decay * n_prev + jnp.sum(ks, 0, keepdims=True)
            m_ref[0, h:h + 1, :] = jnp.broadcast_to(m_new, (1, LANES))

    lo_q = lax.broadcasted_iota(jnp.int32, (WIN, LANES), 1) < HD
    qi = lax.broadcasted_iota(jnp.int32, (WIN, 2 * WIN), 0)
    kj = lax.broadcasted_iota(jnp.int32, (WIN, 2 * WIN), 1)
    band = (kj >= qi) & (kj <= qi + WIN)
    ones_v = jnp.ones((2 * WIN, LANES), BF16)
    for j in range(tm // WIN):
        r0 = j * WIN
        if j == 0 and first is not None:
            mask = band & (kj >= jnp.where(first, WIN, 0))
        else:
            mask = band
        for g in range(KVH):
            if j == 0:
                kprev, vprev = kvc[g], kvc[KVH + g]
            else:
                kprev, vprev = buf["k2"][slot, g, r0 - WIN:r0, :], buf["v2"][slot, g, r0 - WIN:r0, :]
            kcat = jnp.concatenate([kprev, buf["k2"][slot, g, r0:r0 + WIN, :]], axis=0)
            vcat = jnp.concatenate([vprev, buf["v2"][slot, g, r0:r0 + WIN, :]], axis=0)
            sc_all = _dot_nt(buf["aq"][slot, j, g], kcat)
            es, ds = [], []
            for hh in range(GROUP):
                sink = w["sink"][g * GROUP + hh] * LOG2E
                sc = jnp.where(mask, sc_all[hh * WIN:(hh + 1) * WIN, :], NEG_INF)
                mx = jnp.maximum(jnp.max(sc, -1, keepdims=True), sink)
                es.append(jnp.exp2(sc - mx).astype(BF16))
                ds.append(jnp.exp2(sink - mx))
            o_all = _dot(jnp.concatenate(es, axis=0), jnp.concatenate([vcat, ones_v], axis=1))
            for pp in range(GROUP // 2):
                pj = g * GROUP // 2 + pp
                halves = []
                for hh in (2 * pp, 2 * pp + 1):
                    oh = o_all[hh * WIN:(hh + 1) * WIN, :]
                    halves.append(oh[:, 0:LANES] / (oh[:, LANES:2 * LANES] + ds[hh]))
                cols = slice(pj * LANES, (pj + 1) * LANES)
                ya = jnp.where(lo_q, halves[0], halves[1])
                mix_s[r0:r0 + WIN, cols] = (ym_s[r0:r0 + WIN, cols]
                                            + buf["sga"][slot, r0:r0 + WIN, cols] * ya).astype(BF16)
    for g in range(KVH):
        kvc[g] = buf["k2"][slot, g, tm - WIN:tm, :]
        kvc[KVH + g] = buf["v2"][slot, g, tm - WIN:tm, :]

    mixed = mix_s[...]
    r = jnp.concatenate([_dot(mixed, w["wout_a"][...]), _dot(mixed, w["wout_b"][...])], axis=1)
    x1_ref[0, row0:row0 + tm, :] = x1_ref[0, row0:row0 + tm, :] + r


def _mixer_kernel(*refs, tm, lc, npb, tps):
    nw = len(_MIXER_W_NAMES)
    x0_ref = refs[0]
    xt_refs = refs[1:1 + tps]
    w = dict(zip(_MIXER_W_NAMES, refs[1 + tps:1 + tps + nw]))
    n_in = 1 + tps + nw + N_CAST
    cast_in = refs[n_in - N_CAST:n_in]
    x1_ref, c_ref, n_ref, m_ref, conv_ref, kp_ref, vp_ref = refs[n_in:n_in + 7]
    cast_out = refs[n_in + 7:n_in + 7 + N_CAST]
    scratch = refs[n_in + 7 + N_CAST:]
    buf = dict(zip(_PROJ_NAMES, scratch[:len(_PROJ_NAMES)]))
    ctail, kvc, ym_s, mix_s = scratch[len(_PROJ_NAMES):]
    st = (c_ref, n_ref, m_ref, ctail, kvc, ym_s, mix_s)
    k = pl.program_id(0)
    first = (tps * k) % npb == 0

    @pl.when(k == 0)
    def _prologue():
        _mixer_project(x0_ref[0], 0, w, buf, tm=tm, lc=lc)

    @pl.when(first)
    def _init():
        c_ref[...] = jnp.zeros_like(c_ref)
        n_ref[...] = jnp.zeros_like(n_ref)
        m_ref[...] = jnp.zeros_like(m_ref)
        ctail[...] = jnp.zeros_like(ctail)
        kvc[...] = jnp.zeros_like(kvc)

    for i in range(tps):
        _mixer_project(xt_refs[i][0], (i + 1) % 2, w, buf, tm=tm, lc=lc)
        _mixer_consume(i % 2, first if i == 0 else None, w, buf, st, x1_ref, i * tm, tm=tm, lc=lc)

    for src, dst in zip(cast_in, cast_out):
        dst[...] = src[...].astype(BF16)

    @pl.when((tps * k + tps - 1) % npb == npb - 1)
    def _state_out():
        conv_ref[0] = ctail[SUBLANES - (CONV_W - 1):SUBLANES, :]
        kp_ref[0] = buf["kvl"][(tps - 1) % 2, :, 0:LANES]
        vp_ref[0] = buf["kvl"][(tps - 1) % 2, :, LANES:2 * LANES]


N_CAST = 4
BF16_ROWS = 16


def _cast_slab(rows, steps):
    tiles = rows // BF16_ROWS
    for d in range(1, tiles + 1):
        if tiles % d == 0 and tiles // d <= steps:
            return BF16_ROWS * d
    raise ValueError((rows, steps))


def _prompt_mixer(x, wts, ffn_w, *, tm=256, lc=128, tps=2):
    B, S, _ = x.shape
    npb = S // tm
    assert S % tm == 0 and tps % 2 == 0 and npb % tps == 0 and tm % lc == 0 and tm % WIN == 0
    nt = B * npb
    hpb = npb // tps
    cs = _const_spec

    def tile_map(off):
        def imap(k):
            t = jnp.minimum(tps * k + off, nt - 1)
            return (t // npb, t % npb, 0)
        return imap

    in_specs = [
        pl.BlockSpec((1, tm, D), lambda k: (0, 0, 0), pipeline_mode=pl.Buffered(1)),
        *[pl.BlockSpec((1, tm, D), tile_map(i + 1)) for i in range(tps)],
        cs((1, D)), cs((1, D)),
        cs((D, W1_W)), cs((D, W2_W)), cs((D, W3_W)), cs((SUBLANES, D)), cs((SUBLANES, 1)),
        cs((CONV_W, QK_CH)), cs((1, QK_CH)), cs((1, D)),
        pl.BlockSpec(memory_space=pltpu.SMEM),
        cs((D, W_HALF)), cs((D, W_HALF)),
    ]
    steps = nt // tps
    cast_specs = []
    for a in ffn_w:
        assert a.shape[0] % BF16_ROWS == 0
        slab = _cast_slab(a.shape[0], steps)
        last = a.shape[0] // slab - 1
        cast_specs.append(pl.BlockSpec((slab, a.shape[1]), lambda k, last=last: (jnp.minimum(k, last), 0)))
    in_specs += cast_specs
    out_shape = (
        jax.ShapeDtypeStruct((B, S, D), F32),
        jax.ShapeDtypeStruct((B, MH, DK, DV), F32),
        jax.ShapeDtypeStruct((B, MH, DK), F32),
        jax.ShapeDtypeStruct((B, SUBLANES, LANES), F32),
        jax.ShapeDtypeStruct((B, CONV_W - 1, QK_CH), F32),
        jax.ShapeDtypeStruct((B, WIN, KVH * HD), F32),
        jax.ShapeDtypeStruct((B, WIN, KVH * HD), F32),
        *[jax.ShapeDtypeStruct(a.shape, BF16) for a in ffn_w],
    )
    out_specs = (
        pl.BlockSpec((1, tps * tm, D), lambda k: (k // hpb, k % hpb, 0)),
        pl.BlockSpec((1, MH, DK, DV), lambda k: (k // hpb, 0, 0, 0)),
        pl.BlockSpec((1, MH, DK), lambda k: (k // hpb, 0, 0)),
        pl.BlockSpec((1, SUBLANES, LANES), lambda k: (k // hpb, 0, 0)),
        pl.BlockSpec((1, CONV_W - 1, QK_CH), lambda k: (k // hpb, 0, 0)),
        pl.BlockSpec((1, WIN, KVH * HD), lambda k: (k // hpb, 0, 0)),
        pl.BlockSpec((1, WIN, KVH * HD), lambda k: (k // hpb, 0, 0)),
        *cast_specs,
    )
    proj_scratch = {
        "xn": pltpu.VMEM((2, tm, D), F32), "zqk": pltpu.VMEM((2, tm, QK_CH), F32),
        "vm": pltpu.VMEM((2, tm, D), BF16), "smo": pltpu.VMEM((2, tm, D), F32),
        "aq": pltpu.VMEM((2, tm // WIN, KVH, GROUP * WIN, LANES), BF16),
        "k2": pltpu.VMEM((2, KVH, tm, LANES), BF16), "v2": pltpu.VMEM((2, KVH, tm, LANES), BF16),
        "kvl": pltpu.VMEM((2, WIN, 2 * LANES), F32),
        "sga": pltpu.VMEM((2, tm, D), F32),
        "gcol": pltpu.VMEM((2, tm, LANES), F32), "grow": pltpu.VMEM((2, SUBLANES, tm), F32),
        "brow": pltpu.VMEM((2, SUBLANES, tm), F32),
    }
    scratch = [proj_scratch[n] for n in _PROJ_NAMES] + [
        pltpu.VMEM((SUBLANES, QK_CH), F32),
        pltpu.VMEM((2 * KVH, WIN, LANES), BF16),
        pltpu.VMEM((tm, D), F32),
        pltpu.VMEM((tm, D), BF16),
    ]
    return pl.pallas_call(
        functools.partial(_mixer_kernel, tm=tm, lc=lc, npb=npb, tps=tps),
        grid=(nt // tps,), in_specs=in_specs, out_specs=out_specs, out_shape=out_shape,
        scratch_shapes=scratch, name="prompt_mixer",
        compiler_params=pltpu.CompilerParams(dimension_semantics=("arbitrary",), vmem_limit_bytes=VMEM_LIMIT),
    )(*([x] * (tps + 1)), wts["ln_in_g"], wts["ln_in_b"], wts["w1"], wts["w2"], wts["w3"], wts["w_ift"], wts["b_ift"],
      wts["conv_w"], wts["conv_b"], wts["m_norm_g"], wts["sinks"], wts["w_out_a"], wts["w_out_b"], *ffn_w)


FFN_CHUNKS = ((0, 6 * MXU_DIM), (6 * MXU_DIM, DFF))


def _ffn_kernel(x_ref, p_ref, l1g_ref, l1b_ref, wgu_ref, wd_ref, wpg_ref, wp_ref, g_ref, b_ref, o_ref, *, sub):
    n = x_ref.shape[0] // sub

    def pre(j):
        x = _layer_norm(x_ref[j * sub:(j + 1) * sub, :], l1g_ref[...], l1b_ref[...])
        return x, x.astype(BF16)

    def body(j, x, xb):
        acc = ALPHA * x + (_sigmoid(_dot(xb, wpg_ref[...]))
                           * _dot(p_ref[j * sub:(j + 1) * sub, :].astype(BF16), wp_ref[...]))
        for c0, c1 in FFN_CHUNKS:
            g = _dot(xb, wgu_ref[:, c0:c1])
            u = _dot(xb, wgu_ref[:, DFF + c0:DFF + c1])
            hcat = (g * _sigmoid(g) * u).astype(BF16)
            acc = acc + _dot(hcat, wd_ref[c0:c1, :])
        return acc

    def post(j, acc):
        o_ref[j * sub:(j + 1) * sub, :] = _layer_norm(acc, g_ref[...], b_ref[...])

    cur = pre(0)
    acc_prev = None
    for j in range(n):
        nxt = pre(j + 1) if j + 1 < n else None
        acc = body(j, *cur)
        if acc_prev is not None:
            post(j - 1, acc_prev)
        acc_prev, cur = acc, nxt
    post(n - 1, acc_prev)


def _ffn(x, p, wts, *, tm, sub):
    N = x.shape[0]
    assert N % tm == 0 and tm % sub == 0
    cs = _const_spec
    return pl.pallas_call(
        functools.partial(_ffn_kernel, sub=sub),
        grid=(N // tm,),
        in_specs=[pl.BlockSpec((tm, D), lambda i: (i, 0)), pl.BlockSpec((tm, PD), lambda i: (i, 0)),
                  cs((1, D)), cs((1, D)),
                  cs((D, 2 * DFF)), cs((DFF, D)), cs((D, D)), cs((PD, D)), cs((1, D)), cs((1, D))],
        out_specs=pl.BlockSpec((tm, D), lambda i: (i, 0)),
        out_shape=jax.ShapeDtypeStruct((N, D), F32),
        name="ffn",
        compiler_params=pltpu.CompilerParams(dimension_semantics=("arbitrary",), vmem_limit_bytes=VMEM_LIMIT),
    )(x, p, wts["ln1_g"], wts["ln1_b"], wts["w_gu"], wts["w_d"], wts["w_pg"], wts["w_p"],
      wts["ln2_g"], wts["ln2_b"])


Z_V = 0
Z_MO = Z_V + D
Z_GM = Z_MO + D
Z_GA = Z_GM + D
Z_XN = Z_GA + D
Z_Q = Z_XN + D
Z_KW = Z_Q + MH * DK
Z_SC = Z_KW + MH * DK
Z_KN = Z_SC + LANES
Z_VN = Z_KN + LANES
Z_W = Z_VN + LANES


def _sample_pre_kernel(x_ref, lng_ref, lnb_ref, w1_ref, w2_ref, w3_ref, wif_ref, bif_ref,
                       cw_ref, cb_ref, cs_ref, n_ref, m_ref,
                       z_ref, qs_ref, convo_ref, no_ref, mo_ref, *, nb, tb):
    w = {"w1": w1_ref, "w2": w2_ref, "w3": w3_ref}
    xn = _layer_norm(x_ref[...], lng_ref[...], lnb_ref[...])
    xb = xn.astype(BF16)
    zqk = _proj(xb, w, "qk")
    cw = cw_ref[...]
    y = cw[3:4] * zqk + cw[2:3] * cs_ref[2] + cw[1:2] * cs_ref[1] + cw[0:1] * cs_ref[0] + cb_ref[...]
    qk = y * _sigmoid(y)
    convo_ref[0] = cs_ref[1]
    convo_ref[1] = cs_ref[2]
    convo_ref[2] = zqk

    gcol = _dot(xb, wif_ref[...]) + bif_ref[...]
    ig = gcol[:, 0:MH]
    m_inter = _log_sigmoid(gcol[:, MH:2 * MH]) + m_ref[...]
    m_t = jnp.maximum(m_inter, ig)
    inter = jnp.exp(m_inter - m_t)
    wi = jnp.exp(ig - m_t)
    enm = jnp.exp(-m_t)
    mo_ref[...] = m_t

    z_ref[:, Z_V:Z_V + D] = _proj(xb, w, "mv")
    z_ref[:, Z_MO:Z_MO + W_HALF] = _proj(xb, w, "mo_lo")
    z_ref[:, Z_MO + W_HALF:Z_MO + D] = _proj(xb, w, "mo_hi")
    z_ref[:, Z_GM:Z_GM + D] = _proj(xb, w, "gm")
    z_ref[:, Z_GA:Z_GA + D] = _proj(xb, w, "ga")
    z_ref[:, Z_XN:Z_XN + D] = xn
    z_ref[:, Z_Q:Z_Q + MH * DK] = qk[:, 0:MH * DK]
    sc = jnp.zeros((nb, LANES), F32)
    lane = lax.broadcasted_iota(jnp.int32, (nb, LANES), 1)
    for h in range(MH):
        qh = qk[:, h * DK:(h + 1) * DK]
        kh = qk[:, MH * DK + h * DK:MH * DK + (h + 1) * DK] * (DK ** -0.5)
        nh = n_ref[:, h * DK:(h + 1) * DK]
        wv = wi[:, h:h + 1] * jnp.sum(qh * kh, -1, keepdims=True)
        den = wv + inter[:, h:h + 1] * jnp.sum(qh * nh, -1, keepdims=True)
        denom = jnp.maximum(jnp.abs(den), enm[:, h:h + 1])
        kw = wi[:, h:h + 1] * kh
        z_ref[:, Z_KW + h * DK:Z_KW + (h + 1) * DK] = kw
        no_ref[:, h * DK:(h + 1) * DK] = inter[:, h:h + 1] * nh + kw
        sc = jnp.where(lane == h, inter[:, h:h + 1], sc)
        sc = jnp.where(lane == MH + h, wv, sc)
        sc = jnp.where(lane == 2 * MH + h, denom, sc)
    z_ref[:, Z_SC:Z_SC + LANES] = sc

    akv = _proj(xb, w, "akv")
    z_ref[:, Z_KN:Z_KN + LANES] = akv[:, 0:LANES]
    z_ref[:, Z_VN:Z_VN + LANES] = akv[:, LANES:2 * LANES]
    aq = _proj(xb, w, "aq") * (HD ** -0.5)
    lo = lax.broadcasted_iota(jnp.int32, (nb, LANES), 1) < HD
    for h in range(AH):
        pair = aq[:, (h // 2) * LANES:(h // 2 + 1) * LANES]
        qs_ref[pl.ds(h, nb, stride=AH), :] = jnp.where(lo, pair if h % 2 == 0 else pltpu.roll(pair, HD, 1), 0.0)


C_RING = 3


def _c_block_copy(c_hbm, cbuf, sems, step, tb):
    slot = step % C_RING
    return pltpu.make_async_copy(c_hbm.at[pl.ds(step * tb, tb)], cbuf.at[slot], sems.at[slot])


def _sample_state_kernel(z_ref, qs_ref, sink_ref, c_hbm, ck_ref, cv_ref,
                         co_ref, qc_ref, os_ref, ko_ref, vo_ref, cbuf, csem, *, tb):
    s = pl.program_id(0)
    n_steps = pl.num_programs(0)

    @pl.when(s == 0)
    def _prime():
        _c_block_copy(c_hbm, cbuf, csem, 0, tb).start()
        _c_block_copy(c_hbm, cbuf, csem, 1, tb).start()

    @pl.when(s + 2 < n_steps)
    def _prefetch():
        _c_block_copy(c_hbm, cbuf, csem, s + 2, tb).start()

    _c_block_copy(c_hbm, cbuf, csem, s, tb).wait()
    c_ref = cbuf.at[s % C_RING]

    lo = lax.broadcasted_iota(jnp.int32, (WIN, LANES), 1) < HD
    newest = lax.broadcasted_iota(jnp.int32, (HD, WIN), 1) == WIN - 1
    sinks = sink_ref[...]
    q_all = z_ref[:, Z_Q:Z_Q + MH * DK]
    kw_all = z_ref[:, Z_KW:Z_KW + MH * DK]
    v_all = z_ref[:, Z_V:Z_V + D]
    sc = z_ref[:, Z_SC:Z_SC + LANES]
    kn_all = z_ref[:, Z_KN:Z_KN + LANES]
    vn_all = z_ref[:, Z_VN:Z_VN + LANES]
    kn2_all = _dup_halves(kn_all, lo[0:tb])
    vn2_all = _dup_halves(vn_all, lo[0:tb])
    rpad = jnp.zeros((LANES - tb, LANES), F32)
    knt = [jnp.concatenate([a, rpad], axis=0).T for a in kn2_all]
    vnt = [jnp.concatenate([a, rpad], axis=0).T for a in vn2_all]
    lane_sel = lax.broadcasted_iota(jnp.int32, (DK, LANES), 1)
    for h in range(MH):
        q_h = q_all[:, h * DK:(h + 1) * DK].astype(BF16)
        kwt_h = jnp.concatenate([kw_all[:, h * DK:(h + 1) * DK], rpad], axis=0).T
        v_h = jnp.concatenate([v_all[:, h * DV:(h + 1) * DV], jnp.zeros((LANES - tb, DV), F32)], axis=0).astype(BF16)
        for i in range(tb):
            cb = c_ref[i, h]
            qc = _dot(q_h, cb.astype(BF16))
            qc_ref[i:i + 1, h * DV:(h + 1) * DV] = qc[i:i + 1]
            outer = _dot(jnp.where(lane_sel == i, kwt_h, 0.0).astype(BF16), v_h)
            co_ref[i, h] = sc[i:i + 1, h:h + 1] * cb + outer

    q_rows = qs_ref[:, 0:HD]
    kn_rows = jnp.concatenate([jnp.broadcast_to(kn2_all[g][i:i + 1, 0:HD], (GROUP, HD))
                               for i in range(tb) for g in range(KVH)], axis=0)
    vn_rows = jnp.concatenate([jnp.broadcast_to(vn2_all[g][i:i + 1, 0:HD], (GROUP, HD))
                               for i in range(tb) for g in range(KVH)], axis=0)
    sink_rows = jnp.concatenate([sinks] * tb, axis=0)
    q_bf = q_rows.astype(BF16)
    s = jnp.concatenate([_dot(q_bf[(i * KVH + g) * GROUP:(i * KVH + g + 1) * GROUP], ck_ref[i, g].astype(BF16))
                         for i in range(tb) for g in range(KVH)], axis=0)
    s_self = jnp.sum(q_rows * kn_rows, -1, keepdims=True)
    mx = jnp.maximum(jnp.maximum(jnp.max(s, -1, keepdims=True), s_self), sink_rows)
    e = jnp.exp(s - mx)
    es = jnp.exp(s_self - mx)
    dsum = jnp.sum(e, -1, keepdims=True) + es + jnp.exp(sink_rows - mx)
    e_bf = e.astype(BF16)
    pv = jnp.concatenate([_dot_nt(e_bf[(i * KVH + g) * GROUP:(i * KVH + g + 1) * GROUP], cv_ref[i, g].astype(BF16))
                          for i in range(tb) for g in range(KVH)], axis=0)
    o = (pv + es * vn_rows) / dsum
    os_ref[...] = jnp.concatenate([o, jnp.zeros((tb * AH, LANES - HD), F32)], axis=1)
    for i in range(tb):
        for g in range(KVH):
            kcol = jnp.broadcast_to(knt[g][0:HD, i:i + 1], (HD, WIN))
            vcol = jnp.broadcast_to(vnt[g][0:HD, i:i + 1], (HD, WIN))
            ko_ref[i, g] = jnp.where(newest, kcol, pltpu.roll(ck_ref[i, g], WIN - 1, 1))
            vo_ref[i, g] = jnp.where(newest, vcol, pltpu.roll(cv_ref[i, g], WIN - 1, 1))


def _sample_post_kernel(z_ref, qc_ref, os_ref, mng_ref, wouta_ref, woutb_ref, x1_ref, *, nb, tb):
    sc = z_ref[:, Z_SC:Z_SC + LANES]
    mng = mng_ref[...]
    lo = lax.broadcasted_iota(jnp.int32, (nb, LANES), 1) < HD
    ym = []
    for h in range(MH):
        v = z_ref[:, Z_V + h * DV:Z_V + (h + 1) * DV]
        num = sc[:, MH + h:MH + h + 1] * v + sc[:, h:h + 1] * qc_ref[:, h * DV:(h + 1) * DV]
        hh = num / sc[:, 2 * MH + h:2 * MH + h + 1]
        hn = hh * lax.rsqrt(jnp.mean(hh * hh, -1, keepdims=True) + RMS_EPS)
        ym.append(_sigmoid(z_ref[:, Z_MO + h * DV:Z_MO + (h + 1) * DV]) * hn * mng[:, h * DV:(h + 1) * DV])
    ym = jnp.concatenate(ym, axis=1)
    ya = []
    for pj in range(AH // 2):
        halves = []
        for par in range(2):
            halves.append(os_ref[pl.ds(2 * pj + par, nb, stride=AH), :])
        ya.append(jnp.where(lo, halves[0], pltpu.roll(halves[1], HD, 1)))
    ya = jnp.concatenate(ya, axis=1)
    mixed = _sigmoid(z_ref[:, Z_GM:Z_GM + D]) * ym + _sigmoid(z_ref[:, Z_GA:Z_GA + D]) * ya
    mixed = mixed.astype(BF16)
    r = jnp.concatenate([_dot(mixed, wouta_ref[...]), _dot(mixed, woutb_ref[...])], axis=1)
    x1_ref[...] = ALPHA * z_ref[:, Z_XN:Z_XN + D] + r


def _sample_post_ffn_kernel(z_ref, qc_ref, os_ref, mng_ref, wouta_ref, woutb_ref, p_ref, l1g_ref, l1b_ref,
                            wgu_ref, wd_ref, wpg_ref, wp_ref, g_ref, b_ref, o_ref, x1_s, *, nb, tb):
    _sample_post_kernel(z_ref, qc_ref, os_ref, mng_ref, wouta_ref, woutb_ref, x1_s, nb=nb, tb=tb)
    _ffn_kernel(x1_s, p_ref, l1g_ref, l1b_ref, wgu_ref, wd_ref, wpg_ref, wp_ref, g_ref, b_ref, o_ref, sub=nb)


def _sample_mixer(x, p, c0, n0, m0, conv0, k0, v0, wts, *, tb=8):
    nb = x.shape[0]
    assert nb % tb == 0 and nb // tb >= 2
    vmem = pltpu.CompilerParams(dimension_semantics=("arbitrary",), vmem_limit_bytes=VMEM_LIMIT)
    full = lambda shape: pl.BlockSpec(shape, lambda i: (0,) * len(shape))
    pre_in = [x, wts["ln_in_g"], wts["ln_in_b"], wts["w1"], wts["w2"], wts["w3"], wts["w_if"], wts["b_if"],
              wts["conv_w"], wts["conv_b"], conv0, n0, m0]
    z, qs, conv_new, n_new, m_new = pl.pallas_call(
        functools.partial(_sample_pre_kernel, nb=nb, tb=tb),
        grid=(1,),
        in_specs=[full(a.shape) for a in pre_in],
        out_specs=(full((nb, Z_W)), full((nb * AH, LANES)), full((CONV_W - 1, nb, QK_CH)),
                   full((nb, MH * DK)), full((nb, MH))),
        out_shape=(jax.ShapeDtypeStruct((nb, Z_W), F32), jax.ShapeDtypeStruct((nb * AH, LANES), F32),
                   jax.ShapeDtypeStruct((CONV_W - 1, nb, QK_CH), F32),
                   jax.ShapeDtypeStruct((nb, MH * DK), F32), jax.ShapeDtypeStruct((nb, MH), F32)),
        name="sample_pre", compiler_params=vmem,
    )(*pre_in)

    c_new, qc, os_, k_new, v_new = pl.pallas_call(
        functools.partial(_sample_state_kernel, tb=tb),
        grid=(nb // tb,),
        in_specs=[pl.BlockSpec((tb, Z_W), lambda i: (i, 0)),
                  pl.BlockSpec((tb * AH, LANES), lambda i: (i, 0)),
                  pl.BlockSpec((AH, 1), lambda i: (0, 0)),
                  pl.BlockSpec(memory_space=pl.ANY),
                  pl.BlockSpec((tb, KVH, HD, WIN), lambda i: (i, 0, 0, 0)),
                  pl.BlockSpec((tb, KVH, HD, WIN), lambda i: (i, 0, 0, 0))],
        out_specs=(pl.BlockSpec((tb, MH, DK, DV), lambda i: (i, 0, 0, 0)),
                   pl.BlockSpec((tb, D), lambda i: (i, 0)),
                   pl.BlockSpec((tb * AH, LANES), lambda i: (i, 0)),
                   pl.BlockSpec((tb, KVH, HD, WIN), lambda i: (i, 0, 0, 0)),
                   pl.BlockSpec((tb, KVH, HD, WIN), lambda i: (i, 0, 0, 0))),
        out_shape=(jax.ShapeDtypeStruct((nb, MH, DK, DV), F32), jax.ShapeDtypeStruct((nb, D), F32),
                   jax.ShapeDtypeStruct((nb * AH, LANES), F32),
                   jax.ShapeDtypeStruct((nb, KVH, HD, WIN), F32), jax.ShapeDtypeStruct((nb, KVH, HD, WIN), F32)),
        scratch_shapes=[pltpu.VMEM((C_RING, tb, MH, DK, DV), F32), pltpu.SemaphoreType.DMA((C_RING,))],
        name="sample_state", compiler_params=vmem,
    )(z, qs, wts["sinks_col"], c0, k0, v0)

    tail_in = [z, qc, os_, wts["m_norm_g"], wts["w_out_a"], wts["w_out_b"], p, wts["ln1_g"], wts["ln1_b"],
               wts["w_gu"], wts["w_d"], wts["w_pg"], wts["w_p"], wts["ln2_g"], wts["ln2_b"]]
    y = pl.pallas_call(
        functools.partial(_sample_post_ffn_kernel, nb=nb, tb=tb),
        grid=(1,),
        in_specs=[_const_spec(a.shape) for a in tail_in],
        out_specs=full((nb, D)),
        out_shape=jax.ShapeDtypeStruct((nb, D), F32),
        scratch_shapes=[pltpu.VMEM((nb, D), F32)],
        name="sample_post_ffn", compiler_params=vmem,
    )(*tail_in)
    return y, c_new, n_new, m_new, conv_new, k_new, v_new


def _regroup_kernel(wt_ref, w1_ref, w2_ref, w3_ref, wif_ref, wift_ref):
    cols = lambda a, b: wt_ref[a:b, :].T.astype(BF16)
    w1_ref[:, 0:2 * D] = cols(O_QK, O_MI)
    w1_ref[:, 2 * D:W1_W] = cols(O_MO, O_MO + W_HALF)
    w2_ref[:, 0:W_HALF + D] = cols(O_MO + W_HALF, O_AK)
    w2_ref[:, W_HALF + D:W2_W] = cols(O_GM, O_GA)
    w3_ref[:, 0:D] = cols(O_GA, IN_WIDTH)
    w3_ref[:, D:W3_W] = cols(O_AK, O_GM)
    gates_t = wt_ref[O_MI:O_MO, :]
    rb = gates_t.shape[1]
    wift_ref[...] = gates_t.astype(BF16)
    wif_ref[...] = jnp.concatenate([gates_t, jnp.zeros((LANES - 2 * MH, rb), F32)], axis=0).T.astype(BF16)


def _regroup_w_in(wt, *, rb=256):
    assert D % rb == 0 and rb % LANES == 0
    return pl.pallas_call(
        _regroup_kernel, grid=(D // rb,),
        in_specs=[pl.BlockSpec((IN_WIDTH, rb), lambda i: (0, i))],
        out_specs=(pl.BlockSpec((rb, W1_W), lambda i: (i, 0)), pl.BlockSpec((rb, W2_W), lambda i: (i, 0)),
                   pl.BlockSpec((rb, W3_W), lambda i: (i, 0)), pl.BlockSpec((rb, LANES), lambda i: (i, 0)),
                   pl.BlockSpec((2 * MH, rb), lambda i: (0, i))),
        out_shape=(jax.ShapeDtypeStruct((D, W1_W), BF16), jax.ShapeDtypeStruct((D, W2_W), BF16),
                   jax.ShapeDtypeStruct((D, W3_W), BF16), jax.ShapeDtypeStruct((D, LANES), BF16),
                   jax.ShapeDtypeStruct((2 * MH, D), BF16)),
        name="regroup_w_in",
        compiler_params=pltpu.CompilerParams(dimension_semantics=("arbitrary",), vmem_limit_bytes=VMEM_LIMIT),
    )(wt)


def _prep_weights(ln_in_g, ln_in_b, w_in, b_igate, b_fgate, conv_w, conv_b, m_norm_g, attn_sinks, w_out,
                  ln1_g, ln1_b, w_gate_up, w_down, ln2_g, ln2_b, w_ple, w_ple_gate):
    row = lambda a: a.reshape(1, -1).astype(F32)
    w1, w2, w3, w_if, w_ift = _regroup_w_in(jnp.transpose(w_in[0], (1, 0)))
    b_if = jnp.pad(jnp.concatenate([b_igate[0], b_fgate[0]]), (0, LANES - 2 * MH))
    wo = w_out[0].astype(BF16)
    return {
        "ln_in_g": row(ln_in_g), "ln_in_b": row(ln_in_b),
        "w1": w1, "w2": w2, "w3": w3,
        "w_if": w_if, "w_ift": w_ift,
        "b_if": b_if.reshape(1, LANES), "b_ift": b_if[:2 * MH].reshape(2 * MH, 1),
        "conv_w": conv_w[0], "conv_b": row(conv_b[0]), "m_norm_g": row(m_norm_g[0]),
        "sinks": attn_sinks[0], "sinks_col": attn_sinks[0].reshape(AH, 1),
        "w_out_a": wo[:, :W_HALF], "w_out_b": wo[:, W_HALF:], "ln1_g": row(ln1_g[0]), "ln1_b": row(ln1_b[0]),
        "ffn_f32": (w_gate_up[0], w_down[0], w_ple_gate[0], w_ple[0]),
        "ln2_g": row(ln2_g[0]), "ln2_b": row(ln2_b[0]),
    }


def _prompt_path(x, p, wts, *, tm=256, lc=128, tmf=1024):
    B, S, _ = x.shape
    x1, c, n, m, conv, k, v, w_gu, w_d, w_pg, w_p = _prompt_mixer(x, wts, wts["ffn_f32"], tm=tm, lc=lc)
    wts = {**wts, "w_gu": w_gu, "w_d": w_d, "w_pg": w_pg, "w_p": w_p}
    y = _ffn(x1.reshape(B * S, D), p.reshape(B * S, PD), wts, tm=tmf, sub=tmf // 4).reshape(B, S, D)
    return wts, (y, c[None], n[None], m[None, :, :MH, 0], conv[None],
                 k.reshape(1, B, WIN, KVH, HD), v.reshape(1, B, WIN, KVH, HD))


def _sample_path(x, p, c0, n0, m0, conv0, k0, v0, wts, *, tb=8):
    nb = x.shape[0]
    y, c, n, m, conv, k, v = _sample_mixer(
        x.reshape(nb, D), p.reshape(nb, PD), c0[0], n0[0].reshape(nb, MH * DK), m0[0],
        jnp.transpose(conv0[0], (1, 0, 2)), jnp.transpose(k0[0], (0, 2, 3, 1)),
        jnp.transpose(v0[0], (0, 2, 3, 1)), wts, tb=tb)
    return (y.reshape(nb, 1, D), c[None], n.reshape(1, nb, MH, DK), m[None], jnp.transpose(conv, (1, 0, 2))[None],
            jnp.transpose(k, (0, 3, 1, 2))[None], jnp.transpose(v, (0, 3, 1, 2))[None])


def kernel(x_prompt, x_sample, state_mlstm_C, state_mlstm_n, state_mlstm_m, state_conv, cache_win_k, cache_win_v, p_prompt, p_sample, ln_in_g, ln_in_b, w_in, b_igate, b_fgate, conv_w, conv_b, m_norm_g, attn_sinks, w_out, ln1_g, ln1_b, w_gate_up, w_down, ln2_g, ln2_b, w_ple, w_ple_gate):
    wts = _prep_weights(ln_in_g, ln_in_b, w_in, b_igate, b_fgate, conv_w, conv_b, m_norm_g, attn_sinks,
                        w_out, ln1_g, ln1_b, w_gate_up, w_down, ln2_g, ln2_b, w_ple, w_ple_gate)
    wts, (yp, c_p, n_p, m_p, conv_p, k_p, v_p) = _prompt_path(x_prompt, p_prompt[0], wts)
    ys, c_s, n_s, m_s, conv_s, k_s, v_s = _sample_path(
        x_sample, p_sample[0], state_mlstm_C, state_mlstm_n, state_mlstm_m, state_conv,
        cache_win_k, cache_win_v, wts)
    return (yp, ys, c_p, n_p, m_p, conv_p, k_p, v_p, c_s, n_s, m_s, conv_s, k_s, v_s)
```
